```python
import math
import jax, jax.numpy as jnp
from jax import lax
import numpy as np

D_MODEL = 1024
BATCH = 2
SEQ = 8192
DEPTH = 4
DEC_BATCH = 128
DEC_SEQ = 1
PAST_LEN = 8192
PAGE_SIZE = 128

MIX_WIDTH = D_MODEL
HEAD_DIM = 64
ATTN_WIDTH = MIX_WIDTH // 2
N_HEADS = ATTN_WIDTH // HEAD_DIM
N_KV_HEADS = N_HEADS // 4
KV_REP = N_HEADS // N_KV_HEADS
WINDOW = 128
CONV_CH = MIX_WIDTH // 4
CONV_WIDTH = 31
SSM_CH = MIX_WIDTH - ATTN_WIDTH - CONV_CH
SSM_GROUP = 16
SSM_GROUPS = SSM_CH // SSM_GROUP
SSM_STATE = 64
D_FF = -(-(-(-8 * D_MODEL // 3)) // 256) * 256
EPS = 1e-6

Q_END = ATTN_WIDTH
K_END = Q_END + N_KV_HEADS * HEAD_DIM
V_END = K_END + N_KV_HEADS * HEAD_DIM
C_END = V_END + 2 * CONV_CH
IN_COLS = C_END + SSM_CH

kernel_name = "hymba_swa_conformer_s5_decoder_step"


def rmsnorm(x, g):
    xf = x.astype(jnp.float32)
    y = xf * lax.rsqrt(jnp.mean(xf * xf, -1, keepdims=True) + EPS)
    return (y * g.astype(jnp.float32)).astype(x.dtype)


def alibi_slopes():
    m = 2.0 ** (-8.0 * jnp.arange(1, N_HEADS + 1, dtype=jnp.float32) / N_HEADS)
    return m.reshape(N_KV_HEADS, KV_REP, 1, 1)


def sink_attention(q, k, v, dist, valid, sinks):
    scale = 1.0 / math.sqrt(HEAD_DIM)
    s = jnp.einsum('...qgrd,...kgd->...grqk', q.astype(jnp.float32), k.astype(jnp.float32)) * scale
    s = s - alibi_slopes() * dist.astype(jnp.float32)
    s = jnp.where(valid, s, -jnp.inf)
    sink = sinks.astype(jnp.float32).reshape(N_KV_HEADS, KV_REP, 1, 1)
    m = jnp.maximum(jnp.max(s, -1, keepdims=True), sink)
    p = jnp.exp(s - m)
    denom = jnp.sum(p, -1, keepdims=True) + jnp.exp(sink - m)
    o = jnp.einsum('...grqk,...kgd->...qgrd', p / denom, v.astype(jnp.float32))
    return o.astype(q.dtype)


def swa_prompt(q, k, v, sinks):
    n, t = q.shape[:2]
    nb = t // WINDOW
    qb = q.reshape(n, nb, WINDOW, N_KV_HEADS, KV_REP, HEAD_DIM)
    pad = jnp.zeros((n, WINDOW, N_KV_HEADS, HEAD_DIM), k.dtype)
    kp = jnp.concatenate([pad, k], 1).reshape(n, nb + 1, WINDOW, N_KV_HEADS, HEAD_DIM)
    vp = jnp.concatenate([pad, v], 1).reshape(n, nb + 1, WINDOW, N_KV_HEADS, HEAD_DIM)
    kb = jnp.concatenate([kp[:, :-1], kp[:, 1:]], 2)
    vb = jnp.concatenate([vp[:, :-1], vp[:, 1:]], 2)
    a = jnp.arange(WINDOW)[:, None]
    b = jnp.arange(2 * WINDOW)[None, :]
    dist = a - b + WINDOW
    key_pos = jnp.arange(nb)[:, None, None] * WINDOW + b[None] - WINDOW
    valid = (dist >= 0) & (dist < WINDOW) & (key_pos >= 0)
    o = sink_attention(qb, kb, vb, dist, valid[:, None, None], sinks)
    keep = min(WINDOW, t)
    return o.reshape(n, t, ATTN_WIDTH), k[:, t - keep:], v[:, t - keep:]


def make_swa_sample(buf_k, buf_v):
    def attend(q, k, v, sinks):
        n, s_len = q.shape[:2]
        w = buf_k.shape[1]
        kk = jnp.concatenate([buf_k.astype(k.dtype), k], 1)
        vv = jnp.concatenate([buf_v.astype(v.dtype), v], 1)
        dist = jnp.arange(s_len)[:, None] - jnp.arange(-w, s_len)[None, :]
        valid = (dist >= 0) & (dist < WINDOW)
        o = sink_attention(q, kk, vv, dist, valid, sinks)
        return o.reshape(n, s_len, ATTN_WIDTH), kk[:, -w:], vv[:, -w:]
    return attend


def conv_module(ag, ctx, dw_w, dw_b, ln_g, ln_b):
    a, g = jnp.split(ag, 2, -1)
    u = a * jax.nn.sigmoid(g)
    full = jnp.concatenate([ctx.astype(u.dtype), u], 1)
    y = lax.conv_general_dilated(full, dw_w[:, None, :].astype(u.dtype), window_strides=(1,),
                                 padding='VALID', dimension_numbers=('NWC', 'WIO', 'NWC'),
                                 feature_group_count=CONV_CH)
    yf = y.astype(jnp.float32) + dw_b.astype(jnp.float32)
    mu = jnp.mean(yf, -1, keepdims=True)
    var = jnp.mean(jnp.square(yf - mu), -1, keepdims=True)
    yn = (yf - mu) * lax.rsqrt(var + EPS) * ln_g.astype(jnp.float32) + ln_b.astype(jnp.float32)
    return jax.nn.silu(yn).astype(u.dtype), full[:, -(CONV_WIDTH - 1):]


def ssm_module(u, h0_re, h0_im, a_re, a_im, log_dt, b_re, b_im, c_re, c_im, d, glu_w, glu_b):
    n, t, _ = u.shape
    f32 = jnp.float32
    uf = u.astype(f32)
    A = lax.complex(a_re.astype(f32), a_im.astype(f32))
    dt = jnp.exp(log_dt.astype(f32))[:, None]
    abar = jnp.exp(A * dt)
    Bc = lax.complex(b_re.astype(f32), b_im.astype(f32))
    bbar = ((abar - 1.0) / A)[..., None] * Bc
    ug = uf.reshape(n, t, SSM_GROUPS, SSM_GROUP).astype(jnp.complex64)
    bu = jnp.einsum('gpc,ntgc->ntgp', bbar, ug)
    h0 = lax.complex(h0_re.astype(f32), h0_im.astype(f32))
    bu = bu.at[:, 0].add(abar * h0)
    a_seq = jnp.broadcast_to(abar, bu.shape)

    def combine(left, right):
        a1, b1 = left
        a2, b2 = right
        return a1 * a2, a2 * b1 + b2

    _, h = lax.associative_scan(combine, (a_seq, bu), axis=1)
    C = lax.complex(c_re.astype(f32), c_im.astype(f32))
    y = jnp.real(jnp.einsum('gcp,ntgp->ntgc', C, h)).reshape(n, t, SSM_CH) + d.astype(f32) * uf
    y = jax.nn.gelu(y)
    out = y * jax.nn.sigmoid(y @ glu_w.astype(f32) + glu_b.astype(f32))
    h_last = h[:, -1]
    return out.astype(u.dtype), jnp.real(h_last), jnp.imag(h_last)


def trunk_layer(x, attend, conv_ctx, h0_re, h0_im, norm_mix_g, w_in, attn_sinks,
                conv_dw_w, conv_dw_b, conv_ln_g, conv_ln_b,
                ssm_a_re, ssm_a_im, ssm_log_dt, ssm_b_re, ssm_b_im, ssm_c_re, ssm_c_im,
                ssm_d, ssm_glu_w, ssm_glu_b, w_out, norm_ffn_g, w_ff_gate, w_ff_up, w_ff_down):
    n, t, _ = x.shape
    h = rmsnorm(x, norm_mix_g)
    z = h @ w_in
    q = z[..., :Q_END].reshape(n, t, N_KV_HEADS, KV_REP, HEAD_DIM)
    k = z[..., Q_END:K_END].reshape(n, t, N_KV_HEADS, HEAD_DIM)
    v = z[..., K_END:V_END].reshape(n, t, N_KV_HEADS, HEAD_DIM)
    attn_out, k_rows, v_rows = attend(q, k, v, attn_sinks)
    conv_out, new_ctx = conv_module(z[..., V_END:C_END], conv_ctx, conv_dw_w, conv_dw_b, conv_ln_g, conv_ln_b)
    ssm_out, h_re, h_im = ssm_module(z[..., C_END:], h0_re, h0_im, ssm_a_re, ssm_a_im, ssm_log_dt,
                                     ssm_b_re, ssm_b_im, ssm_c_re, ssm_c_im, ssm_d, ssm_glu_w, ssm_glu_b)
    x = x + jnp.concatenate([attn_out, conv_out, ssm_out], -1) @ w_out
    hf = rmsnorm(x, norm_ffn_g)
    x = x + (jax.nn.silu(hf @ w_ff_gate) * (hf @ w_ff_up)) @ w_ff_down
    return x, (k_rows, v_rows, new_ctx, h_re, h_im)


def setup_inputs(seed: int = 0) -> dict:
    key = jax.random.key(seed)
    ks = jax.random.split(key, 40)
    f32 = jnp.float32
    nrm = lambda i, shape, s: jax.random.normal(ks[i], shape, f32) * s
    win = min(WINDOW, PAST_LEN)
    n_idx = jnp.arange(SSM_STATE, dtype=f32)
    log_dt = jax.random.uniform(ks[20], (DEPTH, SSM_GROUPS), f32, math.log(1e-3), math.log(1e-1))
    return {
        "x_prompt": nrm(0, (BATCH, SEQ, D_MODEL), 1.0),
        "x_sample": nrm(1, (DEC_BATCH, DEC_SEQ, D_MODEL), 1.0),
        "cache_swa_k": nrm(2, (DEPTH, DEC_BATCH, win, N_KV_HEADS, HEAD_DIM), 1.0),
        "cache_swa_v": nrm(3, (DEPTH, DEC_BATCH, win, N_KV_HEADS, HEAD_DIM), 1.0),
        "cache_conv": nrm(4, (DEPTH, DEC_BATCH, CONV_WIDTH - 1, CONV_CH), 0.5),
        "state_ssm_re": nrm(5, (DEPTH, DEC_BATCH, SSM_GROUPS, SSM_STATE), 0.2),
        "state_ssm_im": nrm(6, (DEPTH, DEC_BATCH, SSM_GROUPS, SSM_STATE), 0.2),
        "norm_mix_g": 1.0 + nrm(7, (DEPTH, D_MODEL), 0.02),
        "w_in": nrm(8, (DEPTH, D_MODEL, IN_COLS), D_MODEL ** -0.5),
        "attn_sinks": nrm(9, (DEPTH, N_HEADS), 0.5),
        "conv_dw_w": nrm(10, (DEPTH, CONV_WIDTH, CONV_CH), CONV_WIDTH ** -0.5),
        "conv_dw_b": nrm(11, (DEPTH, CONV_CH), 0.01),
        "conv_ln_g": 1.0 + nrm(12, (DEPTH, CONV_CH), 0.02),
        "conv_ln_b": nrm(13, (DEPTH, CONV_CH), 0.01),
        "ssm_a_re": -0.5 + nrm(14, (DEPTH, SSM_GROUPS, SSM_STATE), 0.01),
        "ssm_a_im": math.pi * n_idx + nrm(15, (DEPTH, SSM_GROUPS, SSM_STATE), 0.01),
        "ssm_log_dt": log_dt,
        "ssm_b_re": nrm(16, (DEPTH, SSM_GROUPS, SSM_STATE, SSM_GROUP), (2 * SSM_GROUP) ** -0.5),
        "ssm_b_im": nrm(17, (DEPTH, SSM_GROUPS, SSM_STATE, SSM_GROUP), (2 * SSM_GROUP) ** -0.5),
        "ssm_c_re": nrm(18, (DEPTH, SSM_GROUPS, SSM_GROUP, SSM_STATE), (2 * SSM_STATE) ** -0.5),
        "ssm_c_im": nrm(19, (DEPTH, SSM_GROUPS, SSM_GROUP, SSM_STATE), (2 * SSM_STATE) ** -0.5),
        "ssm_d": nrm(21, (DEPTH, SSM_CH), 1.0),
        "ssm_glu_w": nrm(22, (DEPTH, SSM_CH, SSM_CH), SSM_CH ** -0.5),
        "ssm_glu_b": nrm(23, (DEPTH, SSM_CH), 0.01),
        "w_out": nrm(24, (DEPTH, MIX_WIDTH, D_MODEL), MIX_WIDTH ** -0.5),
        "norm_ffn_g": 1.0 + nrm(25, (DEPTH, D_MODEL), 0.02),
        "w_ff_gate": nrm(26, (DEPTH, D_MODEL, D_FF), D_MODEL ** -0.5),
        "w_ff_up": nrm(27, (DEPTH, D_MODEL, D_FF), D_MODEL ** -0.5),
        "w_ff_down": nrm(28, (DEPTH, D_FF, D_MODEL), D_FF ** -0.5),
        "norm_final_g": 1.0 + nrm(29, (D_MODEL,), 0.02),
    }


def reference(x_prompt, x_sample, cache_swa_k, cache_swa_v, cache_conv, state_ssm_re, state_ssm_im,
              norm_mix_g, w_in, attn_sinks, conv_dw_w, conv_dw_b, conv_ln_g, conv_ln_b,
              ssm_a_re, ssm_a_im, ssm_log_dt, ssm_b_re, ssm_b_im, ssm_c_re, ssm_c_im,
              ssm_d, ssm_glu_w, ssm_glu_b, w_out, norm_ffn_g, w_ff_gate, w_ff_up, w_ff_down,
              norm_final_g):
    n_p = x_prompt.shape[0]
    xp, xs = x_prompt, x_sample
    prompt_states, sample_states = [], []
    for l in range(DEPTH):
        lp = [a[l] for a in (norm_mix_g, w_in, attn_sinks, conv_dw_w, conv_dw_b, conv_ln_g, conv_ln_b,
                             ssm_a_re, ssm_a_im, ssm_log_dt, ssm_b_re, ssm_b_im, ssm_c_re, ssm_c_im,
                             ssm_d, ssm_glu_w, ssm_glu_b, w_out, norm_ffn_g, w_ff_gate, w_ff_up, w_ff_down)]
        zero_ctx = jnp.zeros((n_p, CONV_WIDTH - 1, CONV_CH), xp.dtype)
        zero_h = jnp.zeros((n_p, SSM_GROUPS, SSM_STATE), jnp.float32)
        xp, st_p = trunk_layer(xp, swa_prompt, zero_ctx, zero_h, zero_h, *lp)
        xs, st_s = trunk_layer(xs, make_swa_sample(cache_swa_k[l], cache_swa_v[l]), cache_conv[l],
                               state_ssm_re[l], state_ssm_im[l], *lp)
        prompt_states.append(st_p)
        sample_states.append(st_s)
    y_prompt = rmsnorm(xp, norm_final_g)
    y_sample = rmsnorm(xs, norm_final_g)
    new_swa_k_prompt = jnp.stack([s[0] for s in prompt_states])
    new_swa_v_prompt = jnp.stack([s[1] for s in prompt_states])
    new_conv_prompt = jnp.stack([s[2] for s in prompt_states])
    new_ssm_re_prompt = jnp.stack([s[3] for s in prompt_states])
    new_ssm_im_prompt = jnp.stack([s[4] for s in prompt_states])
    new_swa_k_sample = jnp.stack([s[0] for s in sample_states])
    new_swa_v_sample = jnp.stack([s[1] for s in sample_states])
    new_conv_sample = jnp.stack([s[2] for s in sample_states])
    new_ssm_re_sample = jnp.stack([s[3] for s in sample_states])
    new_ssm_im_sample = jnp.stack([s[4] for s in sample_states])
    return (y_prompt, y_sample,
            new_swa_k_prompt, new_swa_v_prompt, new_conv_prompt, new_ssm_re_prompt, new_ssm_im_prompt,
            new_swa_k_sample, new_swa_v_sample, new_conv_sample, new_ssm_re_sample, new_ssm_im_sample)
```

```python
import functools
import math

import jax
import jax.numpy as jnp
from jax import lax
from jax.experimental import pallas as pl
from jax.experimental.pallas import tpu as pltpu

D_MODEL = 1024
DEPTH = 4
HEAD_DIM = 64
ATTN_WIDTH = 512
N_HEADS = 8
N_KV_HEADS = 2
KV_REP = 4
WINDOW = 128
CONV_CH = 256
CONV_WIDTH = 31
SSM_CH = 256
SSM_GROUP = 16
SSM_GROUPS = 16
SSM_STATE = 64
D_FF = 2816
EPS = 1e-6

Q_END = ATTN_WIDTH
K_END = Q_END + N_KV_HEADS * HEAD_DIM
V_END = K_END + N_KV_HEADS * HEAD_DIM
C_END = V_END + 2 * CONV_CH
IN_COLS = C_END + SSM_CH

N_STATE = SSM_GROUPS * SSM_STATE
LANES = 128
SSM_CHUNK = 8
CHUNK_COLS = SSM_CHUNK * SSM_CH
NEG = -1e30

TM_ROWS = 512
TS_ROWS = 2048
FF_CHUNK = 256
DEC_BLOCK = 16
CONV_ROWS = 64
VMEM_LIMIT = 56 * 1024 * 1024

F32 = jnp.float32
BF16 = jnp.bfloat16


def _params(n_axes):
    return pltpu.CompilerParams(dimension_semantics=("arbitrary",) * n_axes,
                                vmem_limit_bytes=VMEM_LIMIT)


def _resident(shape, index_map):
    return pl.BlockSpec(shape, index_map, pipeline_mode=pl.Buffered(1))


def _rms(x, g):
    return x * lax.rsqrt(jnp.mean(x * x, -1, keepdims=True) + EPS) * g


def _sigmoid(x):
    return 1.0 / (1.0 + jnp.exp(-x))


def _gelu_tanh(x):
    c = math.sqrt(2.0 / math.pi)
    return 0.5 * x * (1.0 + jnp.tanh(c * (x + 0.044715 * (x * x * x))))


def _dot(a, b):
    return jnp.dot(a, b, preferred_element_type=F32)


def _inproj_body(x_ref, g_ref, w_ref, z_ref, u_ref):
    h = _rms(x_ref[...], g_ref[...]).astype(BF16)
    z = _dot(h, w_ref[...])
    z_ref[...] = z[:, :C_END]
    u_ref[0] = z[:, C_END:C_END + LANES]
    u_ref[1] = z[:, C_END + LANES:]


def _inproj(x, g3, w_in, layer, tm):
    rows = x.shape[0]
    return pl.pallas_call(
        _inproj_body,
        grid=(rows // tm,),
        in_specs=[
            pl.BlockSpec((tm, D_MODEL), lambda i: (i, 0)),
            pl.BlockSpec((None, 1, D_MODEL), lambda i: (layer, 0, 0)),
            _resident((None, D_MODEL, IN_COLS), lambda i: (layer, 0, 0)),
        ],
        out_specs=[
            pl.BlockSpec((tm, C_END), lambda i: (i, 0)),
            pl.BlockSpec((2, tm, LANES), lambda i: (0, i, 0)),
        ],
        out_shape=[jax.ShapeDtypeStruct((rows, C_END), F32),
                   jax.ShapeDtypeStruct((2, rows, LANES), F32)],
        compiler_params=_params(1),
        name="inproj",
    )(x, g3, w_in)


def _tail_body(n_parts, final, *refs):
    x_ref, g_ref = refs[0], refs[1]
    parts = refs[2:2 + 2 * n_parts]
    wg_ref, wu_ref, wd_ref = refs[2 + 2 * n_parts:5 + 2 * n_parts]
    gf_ref = refs[5 + 2 * n_parts] if final else None
    o_ref, acc_s, hf_s = refs[-3:]

    x1 = x_ref[...]
    for p in range(n_parts):
        x1 = x1 + _dot(parts[2 * p][...].astype(BF16), parts[2 * p + 1][...])
    hf_s[...] = _rms(x1, g_ref[...]).astype(BF16)
    acc_s[...] = x1

    def ff_step(c, carry):
        hf = hf_s[...]
        gate = _dot(hf, wg_ref[c])
        up = _dot(hf, wu_ref[c])
        act = (gate * _sigmoid(gate) * up).astype(BF16)
        acc_s[...] += _dot(act, wd_ref[c])
        return carry

    lax.fori_loop(0, D_FF // FF_CHUNK, ff_step, 0)
    acc = acc_s[...]
    if final:
        acc = _rms(acc, gf_ref[...])
    o_ref[...] = acc


def _tail(x, g3, parts, wg, wu, wd, layer, tm, final_g=None):
    rows = x.shape[0]
    final = final_g is not None
    n_ff = D_FF // FF_CHUNK
    in_specs = [pl.BlockSpec((tm, D_MODEL), lambda i: (i, 0)),
                pl.BlockSpec((None, 1, D_MODEL), lambda i: (layer, 0, 0))]
    args = [x, g3]
    for act, spec, w in parts:
        in_specs.append(spec)
        in_specs.append(pl.BlockSpec((None,) + w.shape[1:], lambda i: (layer, 0, 0)))
        args += [act, w]
    in_specs += [
        _resident((None, n_ff, D_MODEL, FF_CHUNK), lambda i: (layer, 0, 0, 0)),
        _resident((None, n_ff, D_MODEL, FF_CHUNK), lambda i: (layer, 0, 0, 0)),
        _resident((None, n_ff, FF_CHUNK, D_MODEL), lambda i: (layer, 0, 0, 0)),
    ]
    args += [wg, wu, wd]
    if final:
        in_specs.append(pl.BlockSpec((1, D_MODEL), lambda i: (0, 0)))
        args.append(final_g)
    return pl.pallas_call(
        functools.partial(_tail_body, len(parts), final),
        grid=(rows // tm,),
        in_specs=in_specs,
        out_specs=pl.BlockSpec((tm, D_MODEL), lambda i: (i, 0)),
        out_shape=jax.ShapeDtypeStruct((rows, D_MODEL), F32),
        scratch_shapes=[pltpu.VMEM((tm, D_MODEL), F32), pltpu.VMEM((tm, D_MODEL), BF16)],
        compiler_params=_params(1),
        name="tail",
    )(*args)


def _ssm_body(n_chunks, u_ref, w1_ref, w2_ref, w4_ref, al_ref, d_ref, gw_ref, gb_ref,
              o_ref, hl_ref, x_s, g_s, hp_s, hc_s):
    @pl.when(pl.program_id(1) == 0)
    def _():
        hc_s[...] = jnp.zeros_like(hc_s)

    for l in range(SSM_CHUNK):
        for s in range(2):
            c0 = l * SSM_CH + s * LANES
            x_s[:, c0:c0 + LANES] = u_ref[s, pl.ds(l, n_chunks, stride=SSM_CHUNK), :]
    xb = x_s[...].astype(BF16)

    g_s[...] = _dot(xb, w2_ref[...])

    a_re = al_ref[:, :N_STATE]
    a_im = al_ref[:, N_STATE:]

    def step(k, carry):
        h_re, h_im = carry
        hp_s[pl.ds(k, 1), :N_STATE] = h_re
        hp_s[pl.ds(k, 1), N_STATE:] = h_im
        g_re = g_s[pl.ds(k, 1), :N_STATE]
        g_im = g_s[pl.ds(k, 1), N_STATE:]
        return (a_re * h_re - a_im * h_im + g_re, a_re * h_im + a_im * h_re + g_im)

    h_re, h_im = lax.fori_loop(0, n_chunks, step, (hc_s[:, :N_STATE], hc_s[:, N_STATE:]))
    hc_s[:, :N_STATE] = h_re
    hc_s[:, N_STATE:] = h_im
    hl_ref[:, :N_STATE] = h_re
    hl_ref[:, N_STATE:] = h_im

    hb = hp_s[...].astype(BF16)
    for l in range(SSM_CHUNK):
        c0, c1 = l * SSM_CH, (l + 1) * SSM_CH
        y = _dot(xb[:, :c1], w1_ref[:c1, c0:c1]) + _dot(hb, w4_ref[:, c0:c1])
        y = _gelu_tanh(y + d_ref[...] * x_s[:, c0:c1])
        gate = _dot(y.astype(BF16), gw_ref[...]) + gb_ref[...]
        out = y * _sigmoid(gate)
        for s in range(2):
            o_ref[s, pl.ds(l, n_chunks, stride=SSM_CHUNK), :] = out[:, s * LANES:(s + 1) * LANES]


def _ssm_prompt(u, w1, w2, w4, al, d3, gw, gb3, layer, n_seq, seq_len):
    ts = TS_ROWS
    n_tiles = seq_len // ts
    n_chunks = ts // SSM_CHUNK
    rows = u.shape[1]
    wspec = _resident((None, CHUNK_COLS, CHUNK_COLS), lambda s, i: (layer, 0, 0))
    return pl.pallas_call(
        functools.partial(_ssm_body, n_chunks),
        grid=(n_seq, n_tiles),
        in_specs=[
            pl.BlockSpec((2, ts, LANES), lambda s, i: (0, s * n_tiles + i, 0)),
            wspec, wspec, wspec,
            pl.BlockSpec((None, 1, 2 * N_STATE), lambda s, i: (layer, 0, 0)),
            pl.BlockSpec((None, 1, SSM_CH), lambda s, i: (layer, 0, 0)),
            pl.BlockSpec((None, SSM_CH, SSM_CH), lambda s, i: (layer, 0, 0)),
            pl.BlockSpec((None, 1, SSM_CH), lambda s, i: (layer, 0, 0)),
        ],
        out_specs=[
            pl.BlockSpec((2, ts, LANES), lambda s, i: (0, s * n_tiles + i, 0)),
            pl.BlockSpec((None, 1, 2 * N_STATE), lambda s, i: (s, 0, 0)),
        ],
        out_shape=[jax.ShapeDtypeStruct((2, rows, LANES), F32),
                   jax.ShapeDtypeStruct((n_seq, 1, 2 * N_STATE), F32)],
        scratch_shapes=[
            pltpu.VMEM((n_chunks, CHUNK_COLS), F32),
            pltpu.VMEM((n_chunks, 2 * N_STATE), F32),
            pltpu.VMEM((n_chunks, 2 * N_STATE), F32),
            pltpu.VMEM((1, 2 * N_STATE), F32),
        ],
        compiler_params=_params(2),
        name="ssm_prompt",
    )(u, w1, w2, w4, al, d3, gw, gb3)


def _mix_body(tm, q_ref, kv_ref, a_ref, gg_ref, sink_ref, cw_ref, cb_ref, lg_ref, lb_ref,
              o_ref, ctx_ref, kv_s, u_s):
    first_tile = pl.program_id(1) == 0
    pad = 32
    off = pad - (CONV_WIDTH - 1)

    @pl.when(first_tile)
    def _():
        kv_s[0:WINDOW, :] = jnp.zeros((WINDOW, 2 * LANES), F32)
        u_s[0:pad, :] = jnp.zeros((pad, CONV_CH), F32)

    kv_s[WINDOW:, :] = kv_ref[...]
    u_s[pad:, :] = a_ref[...] * _sigmoid(gg_ref[...])

    for r0 in range(0, tm, CONV_ROWS):
        acc = jnp.zeros((CONV_ROWS, CONV_CH), F32) + cb_ref[...]
        for j in range(CONV_WIDTH):
            acc = acc + cw_ref[j:j + 1, :] * u_s[r0 + off + j:r0 + off + j + CONV_ROWS, :]
        mu = jnp.mean(acc, -1, keepdims=True)
        cen = acc - mu
        var = jnp.mean(cen * cen, -1, keepdims=True)
        yn = cen * lax.rsqrt(var + EPS) * lg_ref[...] + lb_ref[...]
        o_ref[r0:r0 + CONV_ROWS, ATTN_WIDTH:] = yn * _sigmoid(yn)
    ctx_ref[...] = u_s[tm + off:tm + pad, :]

    qi = lax.broadcasted_iota(jnp.int32, (WINDOW, 2 * WINDOW), 0)
    ki = lax.broadcasted_iota(jnp.int32, (WINDOW, 2 * WINDOW), 1)
    dist = qi - ki + WINDOW
    valid = (dist >= 0) & (dist < WINDOW)
    distf = dist.astype(F32)
    valid_first = valid & ((ki >= WINDOW) | jnp.logical_not(first_tile))
    scale = 1.0 / math.sqrt(HEAD_DIM)

    for blk in range(tm // WINDOW):
        r0 = blk * WINDOW
        vmask = valid_first if blk == 0 else valid
        for g in range(N_KV_HEADS):
            k = kv_s[r0:r0 + 2 * WINDOW, g * HEAD_DIM:(g + 1) * HEAD_DIM].astype(BF16)
            v = kv_s[r0:r0 + 2 * WINDOW, LANES + g * HEAD_DIM:LANES + (g + 1) * HEAD_DIM].astype(BF16)
            outs = []
            for r in range(KV_REP):
                h = g * KV_REP + r
                slope = 2.0 ** (-8.0 * (h + 1) / N_HEADS)
                sink = sink_ref[:, h:h + 1]
                q = (q_ref[r0:r0 + WINDOW, h * HEAD_DIM:(h + 1) * HEAD_DIM] * scale).astype(BF16)
                s = lax.dot_general(q, k, (((1,), (1,)), ((), ())), preferred_element_type=F32)
                s = jnp.where(vmask, s - slope * distf, NEG)
                m = jnp.maximum(jnp.max(s, -1, keepdims=True), sink)
                p = jnp.exp(s - m)
                denom = jnp.sum(p, -1, keepdims=True) + jnp.exp(sink - m)
                outs.append(_dot(p.astype(BF16), v) / denom)
            o_ref[r0:r0 + WINDOW, g * 2 * LANES:(g + 1) * 2 * LANES] = jnp.concatenate(outs, -1)

    kv_s[0:WINDOW, :] = kv_s[tm:tm + WINDOW, :]
    u_s[0:pad, :] = u_s[tm:tm + pad, :]


def _mix_prompt(z, sinks3, cw, cb3, lg3, lb3, layer, n_seq, seq_len):
    tm = TM_ROWS
    n_tiles = seq_len // tm
    rows = n_seq * seq_len
    row = lambda s, i: s * n_tiles + i
    vec = lambda width: pl.BlockSpec((None, 1, width), lambda s, i: (layer, 0, 0))
    return pl.pallas_call(
        functools.partial(_mix_body, tm),
        grid=(n_seq, n_tiles),
        in_specs=[
            pl.BlockSpec((tm, ATTN_WIDTH), lambda s, i: (row(s, i), 0)),
            pl.BlockSpec((tm, 2 * LANES), lambda s, i: (row(s, i), Q_END // (2 * LANES))),
            pl.BlockSpec((tm, CONV_CH), lambda s, i: (row(s, i), V_END // CONV_CH)),
            pl.BlockSpec((tm, CONV_CH), lambda s, i: (row(s, i), V_END // CONV_CH + 1)),
            vec(N_HEADS),
            pl.BlockSpec((None, CONV_WIDTH, CONV_CH), lambda s, i: (layer, 0, 0)),
            vec(CONV_CH), vec(CONV_CH), vec(CONV_CH),
        ],
        out_specs=[
            pl.BlockSpec((tm, V_END), lambda s, i: (row(s, i), 0)),
            pl.BlockSpec((None, CONV_WIDTH - 1, CONV_CH), lambda s, i: (s, 0, 0)),
        ],
        out_shape=[jax.ShapeDtypeStruct((rows, V_END), F32),
                   jax.ShapeDtypeStruct((n_seq, CONV_WIDTH - 1, CONV_CH), F32)],
        scratch_shapes=[pltpu.VMEM((tm + WINDOW, 2 * LANES), F32),
                        pltpu.VMEM((tm + 32, CONV_CH), F32)],
        compiler_params=_params(2),
        name="mix_prompt",
    )(z, z, z, z, sinks3, cw, cb3, lg3, lb3)


def _dec_body(q_ref, kn_ref, vn_ref, ck_ref, cv_ref, a_ref, gg_ref, cc_ref, u_ref, hr_ref, hi_ref,
              sink_ref, cw_ref, cb_ref, lg_ref, lb_ref, bm_ref, cm_ref, ab_ref, d_ref, gw_ref, gb_ref,
              o_ref, ok_ref, ov_ref, co_ref, oc_ref, so_ref, or_ref, oi_ref):
    nb = DEC_BLOCK
    win = ck_ref.shape[1]

    q3 = q_ref[...].reshape(nb, N_HEADS, LANES)
    kn = kn_ref[...]
    vn = vn_ref[...]
    kb = ck_ref[...].astype(BF16)
    vb = cv_ref[...].astype(BF16)
    s = jnp.einsum("nsc,njc->nsj", q3, kb, preferred_element_type=F32)
    si = lax.broadcasted_iota(jnp.int32, (N_HEADS, win), 0)
    ji = lax.broadcasted_iota(jnp.int32, (N_HEADS, win), 1)
    head = (si % 2) * KV_REP + si // 2
    slope = jnp.zeros((N_HEADS, win), F32)
    for h in range(N_HEADS):
        slope = jnp.where(head == h, 2.0 ** (-8.0 * (h + 1) / N_HEADS), slope)
    dist = win - ji
    bias = jnp.where(dist < WINDOW, -slope * dist.astype(F32), NEG)
    s = s + bias[None]
    s_new = jnp.sum(q3.astype(F32) * kn.astype(BF16).astype(F32), -1, keepdims=True)
    sink = sink_ref[...][None]
    m = jnp.maximum(jnp.maximum(jnp.max(s, -1, keepdims=True), s_new), sink)
    p = jnp.exp(s - m)
    p_new = jnp.exp(s_new - m)
    denom = jnp.sum(p, -1, keepdims=True) + p_new + jnp.exp(sink - m)
    o = jnp.einsum("nsj,njc->nsc", p.astype(BF16), vb, preferred_element_type=F32)
    o = o + p_new.astype(BF16).astype(F32) * vn.astype(BF16).astype(F32)
    o_ref[...] = (o / denom).reshape(nb * N_HEADS, LANES)
    ok_ref[:, 0:win - 1, :] = ck_ref[:, 1:win, :]
    ok_ref[:, win - 1:win, :] = kn
    ov_ref[:, 0:win - 1, :] = cv_ref[:, 1:win, :]
    ov_ref[:, win - 1:win, :] = vn

    u = a_ref[...] * _sigmoid(gg_ref[...])
    acc = cb_ref[...] + cw_ref[CONV_WIDTH - 1:CONV_WIDTH, :] * u
    for j in range(CONV_WIDTH - 1):
        acc = acc + cw_ref[j:j + 1, :] * cc_ref[j]
    mu = jnp.mean(acc, -1, keepdims=True)
    cen = acc - mu
    var = jnp.mean(cen * cen, -1, keepdims=True)
    yn = cen * lax.rsqrt(var + EPS) * lg_ref[...] + lb_ref[...]
    co_ref[...] = yn * _sigmoid(yn)
    for j in range(CONV_WIDTH - 2):
        oc_ref[j] = cc_ref[j + 1]
    oc_ref[CONV_WIDTH - 2] = u

    us = jnp.concatenate([u_ref[0], u_ref[1]], -1)
    bu = _dot(us.astype(BF16), bm_ref[...])
    a_re = ab_ref[:, :N_STATE]
    a_im = ab_ref[:, N_STATE:]
    h_re = hr_ref[...]
    h_im = hi_ref[...]
    n_re = a_re * h_re - a_im * h_im + bu[:, :N_STATE]
    n_im = a_re * h_im + a_im * h_re + bu[:, N_STATE:]
    or_ref[...] = n_re
    oi_ref[...] = n_im
    hcat = jnp.concatenate([n_re, n_im], -1).astype(BF16)
    y = _gelu_tanh(_dot(hcat, cm_ref[...]) + d_ref[...] * us)
    gate = _dot(y.astype(BF16), gw_ref[...]) + gb_ref[...]
    so_ref[...] = y * _sigmoid(gate)


def _dec_mix(q3, kn3, vn3, ck, cv, z, cct, u, hr, hi, sinks3, cw, cb3, lg3, lb3,
             bmat, cmat, ab3, d3, gw, gb3, layer):
    n = kn3.shape[0]
    win = ck.shape[2]
    nb = DEC_BLOCK
    vec = lambda width: pl.BlockSpec((None, 1, width), lambda i: (layer, 0, 0))
    mat = lambda r, c: pl.BlockSpec((None, r, c), lambda i: (layer, 0, 0))
    cache = pl.BlockSpec((None, nb, win, LANES), lambda i: (layer, i, 0, 0))
    ctx = pl.BlockSpec((None, CONV_WIDTH - 1, nb, CONV_CH), lambda i: (layer, 0, i, 0))
    state = pl.BlockSpec((None, nb, N_STATE), lambda i: (layer, i, 0))
    rowblk = lambda width: pl.BlockSpec((nb, width), lambda i: (i, 0))
    new3 = pl.BlockSpec((nb, 1, LANES), lambda i: (i, 0, 0))
    return pl.pallas_call(
        _dec_body,
        grid=(n // nb,),
        in_specs=[
            pl.BlockSpec((nb * N_HEADS, LANES), lambda i: (i, 0)), new3, new3, cache, cache,
            pl.BlockSpec((nb, CONV_CH), lambda i: (i, V_END // CONV_CH)),
            pl.BlockSpec((nb, CONV_CH), lambda i: (i, V_END // CONV_CH + 1)),
            ctx,
            pl.BlockSpec((2, nb, LANES), lambda i: (0, i, 0)),
            state, state,
            mat(N_HEADS, 1), mat(CONV_WIDTH, CONV_CH), vec(CONV_CH), vec(CONV_CH), vec(CONV_CH),
            mat(SSM_CH, 2 * N_STATE), mat(2 * N_STATE, SSM_CH), vec(2 * N_STATE), vec(SSM_CH),
            mat(SSM_CH, SSM_CH), vec(SSM_CH),
        ],
        out_specs=[
            pl.BlockSpec((nb * N_HEADS, LANES), lambda i: (i, 0)),
            pl.BlockSpec((nb, win, LANES), lambda i: (i, 0, 0)),
            pl.BlockSpec((nb, win, LANES), lambda i: (i, 0, 0)),
            rowblk(CONV_CH),
            pl.BlockSpec((CONV_WIDTH - 1, nb, CONV_CH), lambda i: (0, i, 0)),
            rowblk(SSM_CH), rowblk(N_STATE), rowblk(N_STATE),
        ],
        out_shape=[
            jax.ShapeDtypeStruct((n * N_HEADS, LANES), F32),
            jax.ShapeDtypeStruct((n, win, LANES), F32),
            jax.ShapeDtypeStruct((n, win, LANES), F32),
            jax.ShapeDtypeStruct((n, CONV_CH), F32),
            jax.ShapeDtypeStruct((CONV_WIDTH - 1, n, CONV_CH), F32),
            jax.ShapeDtypeStruct((n, SSM_CH), F32),
            jax.ShapeDtypeStruct((n, N_STATE), F32),
            jax.ShapeDtypeStruct((n, N_STATE), F32),
        ],
        compiler_params=_params(1),
        name="dec_mix",
    )(q3, kn3, vn3, ck, cv, z, z, cct, u, hr, hi, sinks3, cw, cb3, lg3, lb3,
      bmat, cmat, ab3, d3, gw, gb3)


def _ssm_operands(a_re, a_im, log_dt, b_re, b_im, c_re, c_im):
    hi = lax.Precision.HIGHEST
    g_n, p_n, c_n, l_n = SSM_GROUPS, SSM_STATE, SSM_GROUP, SSM_CHUNK
    a = lax.complex(a_re, a_im)
    dt = jnp.exp(log_dt)[:, None]
    abar = jnp.exp(a * dt)
    bbar = ((abar - 1.0) / a)[..., None] * lax.complex(b_re, b_im)
    cc = lax.complex(c_re, c_im)
    pows = [jnp.ones_like(abar)]
    for _ in range(l_n):
        pows.append(pows[-1] * abar)
    pows_rev = jnp.stack(pows[l_n - 1::-1])
    pows = jnp.stack(pows)
    eye = jnp.eye(g_n, dtype=F32)

    ab = pows[:l_n, :, :, None] * bbar[None]
    ktau = (jnp.einsum("gcp,tgpd->tgcd", jnp.real(cc), jnp.real(ab), precision=hi)
            - jnp.einsum("gcp,tgpd->tgcd", jnp.imag(cc), jnp.imag(ab), precision=hi))
    lag = jnp.arange(l_n)[None, :] - jnp.arange(l_n)[:, None]
    kt = jnp.where((lag >= 0)[:, :, None, None, None],
                   ktau[jnp.clip(lag, 0, l_n - 1)], 0.0)
    w1 = jnp.einsum("abgcd,gh->agdbhc", kt, eye).reshape(CHUNK_COLS, CHUNK_COLS)

    m2 = pows_rev[:, :, :, None] * bbar[None]
    m2 = jnp.stack([jnp.real(m2), jnp.imag(m2)])
    w2 = jnp.einsum("rlgpc,gh->lgcrhp", m2, eye).reshape(CHUNK_COLS, 2 * N_STATE)

    m4 = cc[None] * pows[1:, :, None, :]
    m4 = jnp.stack([jnp.real(m4), -jnp.imag(m4)])
    w4 = jnp.einsum("rlgcp,gh->rgplhc", m4, eye).reshape(2 * N_STATE, CHUNK_COLS)

    cm = jnp.stack([jnp.real(cc), -jnp.imag(cc)])
    cmat = jnp.einsum("rgcp,gh->rgphc", cm, eye).reshape(2 * N_STATE, SSM_CH)
    bmat = w2[(l_n - 1) * SSM_CH:]

    flat = lambda z: jnp.concatenate([jnp.real(z).reshape(1, N_STATE), jnp.imag(z).reshape(1, N_STATE)], -1)
    return (w1.astype(BF16), w2.astype(BF16), w4.astype(BF16), bmat.astype(BF16), cmat.astype(BF16),
            flat(pows[l_n]), flat(abar))


def _decode_head_order():
    s = jnp.arange(N_HEADS)
    return (s % 2) * KV_REP + s // 2


def kernel(x_prompt, x_sample, cache_swa_k, cache_swa_v, cache_conv, state_ssm_re, state_ssm_im,
           norm_mix_g, w_in, attn_sinks, conv_dw_w, conv_dw_b, conv_ln_g, conv_ln_b,
           ssm_a_re, ssm_a_im, ssm_log_dt, ssm_b_re, ssm_b_im, ssm_c_re, ssm_c_im,
           ssm_d, ssm_glu_w, ssm_glu_b, w_out, norm_ffn_g, w_ff_gate, w_ff_up, w_ff_down,
           norm_final_g):
    n_seq, seq_len, _ = x_prompt.shape
    n_dec = x_sample.shape[0]
    win = cache_swa_k.shape[2]
    assert x_sample.shape[1] == 1 and win == WINDOW
    assert seq_len % TS_ROWS == 0 and n_dec % DEC_BLOCK == 0

    row3 = lambda v: v.reshape(DEPTH, 1, -1)
    g_mix, g_ffn = row3(norm_mix_g), row3(norm_ffn_g)
    sinks3 = row3(attn_sinks)
    order = _decode_head_order()
    sinks_dec = attn_sinks[:, order][:, :, None]
    cb3, lg3, lb3 = row3(conv_dw_b), row3(conv_ln_g), row3(conv_ln_b)
    d3, gb3 = row3(ssm_d), row3(ssm_glu_b)
    w_in_b = w_in.astype(BF16)
    w_out_b = w_out.astype(BF16)
    n_ff = D_FF // FF_CHUNK
    col_chunks = lambda w: jnp.transpose(w.astype(BF16).reshape(DEPTH, D_MODEL, n_ff, FF_CHUNK), (0, 2, 1, 3))
    wg_b, wu_b = col_chunks(w_ff_gate), col_chunks(w_ff_up)
    wd_b = w_ff_down.astype(BF16).reshape(DEPTH, n_ff, FF_CHUNK, D_MODEL)
    gw_b = ssm_glu_w.astype(BF16)
    w1, w2, w4, bmat, cmat, a_chunk, a_step = jax.vmap(_ssm_operands)(
        ssm_a_re, ssm_a_im, ssm_log_dt, ssm_b_re, ssm_b_im, ssm_c_re, ssm_c_im)

    wo_heads = w_out_b[:, :ATTN_WIDTH].reshape(DEPTH, N_HEADS, HEAD_DIM, D_MODEL)[:, order]
    own = (jnp.arange(N_HEADS)[:, None] % 2) == jnp.arange(N_KV_HEADS)[None, :]
    wo_dec = jnp.where(own[None, :, :, None, None], wo_heads[:, :, None], 0).reshape(
        DEPTH, N_HEADS * LANES, D_MODEL)
    wo_attn_conv = w_out_b[:, :V_END]
    wo_conv = w_out_b[:, ATTN_WIDTH:V_END]
    wo_ssm0 = w_out_b[:, V_END:V_END + LANES]
    wo_ssm1 = w_out_b[:, V_END + LANES:]
    wo_ssm = w_out_b[:, V_END:]

    ck = cache_swa_k.reshape(DEPTH, n_dec, win, LANES)
    cv = cache_swa_v.reshape(DEPTH, n_dec, win, LANES)
    cct = jnp.transpose(cache_conv, (0, 2, 1, 3))
    hr = state_ssm_re.reshape(DEPTH, n_dec, N_STATE)
    hi = state_ssm_im.reshape(DEPTH, n_dec, N_STATE)

    xp = x_prompt.reshape(n_seq * seq_len, D_MODEL)
    xs = x_sample.reshape(n_dec, D_MODEL)
    tm = TM_ROWS
    rowspec = lambda width: pl.BlockSpec((tm, width), lambda i: (i, 0))
    slabspec = lambda s: pl.BlockSpec((None, tm, LANES), lambda i: (s, i, 0))
    decspec = lambda width: pl.BlockSpec((n_dec, width), lambda i: (i, 0))
    scale = 1.0 / math.sqrt(HEAD_DIM)
    own_dec = own[None, :, :, None]

    kp, vp, cp, hrp, hip = [], [], [], [], []
    ks, vs, cs, hrs, his = [], [], [], [], []
    for l in range(DEPTH):
        final = norm_final_g.reshape(1, D_MODEL) if l == DEPTH - 1 else None

        z, u = _inproj(xp, g_mix, w_in_b, l, tm)
        ssm, h_last = _ssm_prompt(u, w1, w2, w4, a_chunk, d3, gw_b, gb3, l, n_seq, seq_len)
        mix, ctx = _mix_prompt(z, sinks3, conv_dw_w, cb3, lg3, lb3, l, n_seq, seq_len)
        xp = _tail(xp, g_ffn,
                   [(mix, rowspec(V_END), wo_attn_conv),
                    (ssm, slabspec(0), wo_ssm0),
                    (ssm, slabspec(1), wo_ssm1)],
                   wg_b, wu_b, wd_b, l, tm, final)
        z3 = z.reshape(n_seq, seq_len, C_END)[:, seq_len - WINDOW:]
        kp.append(z3[..., Q_END:K_END].reshape(n_seq, WINDOW, N_KV_HEADS, HEAD_DIM))
        vp.append(z3[..., K_END:V_END].reshape(n_seq, WINDOW, N_KV_HEADS, HEAD_DIM))
        cp.append(ctx)
        hrp.append(h_last[:, 0, :N_STATE].reshape(n_seq, SSM_GROUPS, SSM_STATE))
        hip.append(h_last[:, 0, N_STATE:].reshape(n_seq, SSM_GROUPS, SSM_STATE))

        zs, us = _inproj(xs, g_mix, w_in_b, l, n_dec)
        qh = (zs[:, :Q_END] * scale).reshape(n_dec, N_HEADS, 1, HEAD_DIM)[:, order]
        q3 = jnp.where(own_dec, qh, 0.0).astype(BF16).reshape(n_dec * N_HEADS, LANES)
        kn3 = zs[:, Q_END:K_END].reshape(n_dec, 1, LANES)
        vn3 = zs[:, K_END:V_END].reshape(n_dec, 1, LANES)
        o3, nk, nv, conv_s, nct, ssm_s, nhr, nhi = _dec_mix(
            q3, kn3, vn3, ck, cv, zs, cct, us, hr, hi, sinks_dec, conv_dw_w, cb3, lg3, lb3,
            bmat, cmat, a_step, d3, gw_b, gb3, l)
        xs = _tail(xs, g_ffn,
                   [(o3.reshape(n_dec, N_HEADS * LANES), decspec(N_HEADS * LANES), wo_dec),
                    (conv_s, decspec(CONV_CH), wo_conv),
                    (ssm_s, decspec(SSM_CH), wo_ssm)],
                   wg_b, wu_b, wd_b, l, n_dec, final)
        ks.append(nk.reshape(n_dec, win, N_KV_HEADS, HEAD_DIM))
        vs.append(nv.reshape(n_dec, win, N_KV_HEADS, HEAD_DIM))
        cs.append(jnp.transpose(nct, (1, 0, 2)))
        hrs.append(nhr.reshape(n_dec, SSM_GROUPS, SSM_STATE))
        his.append(nhi.reshape(n_dec, SSM_GROUPS, SSM_STATE))

    return (xp.reshape(n_seq, seq_len, D_MODEL), xs.reshape(n_dec, 1, D_MODEL),
            jnp.stack(kp), jnp.stack(vp), jnp.stack(cp), jnp.stack(hrp), jnp.stack(hip),
            jnp.stack(ks), jnp.stack(vs), jnp.stack(cs), jnp.stack(hrs), jnp.stack(his))
```

```python
import functools
import math

import jax
import jax.numpy as jnp
from jax import lax
from jax.experimental import pallas as pl
from jax.experimental.pallas import tpu as pltpu

D_MODEL = 1024
DEPTH = 4
HEAD_DIM = 64
ATTN_WIDTH = 512
N_HEADS = 8
N_KV_HEADS = 2
KV_REP = 4
WINDOW = 128
CONV_CH = 256
CONV_WIDTH = 31
SSM_CH = 256
SSM_GROUP = 16
SSM_GROUPS = 16
SSM_STATE = 64
D_FF = 2816
EPS = 1e-6

Q_END = ATTN_WIDTH
K_END = Q_END + N_KV_HEADS * HEAD_DIM
V_END = K_END + N_KV_HEADS * HEAD_DIM
C_END = V_END + 2 * CONV_CH
IN_COLS = C_END + SSM_CH

N_STATE = SSM_GROUPS * SSM_STATE
LANES = 128
SSM_CHUNK = 8
CHUNK_COLS = SSM_CHUNK * SSM_CH
NEG = -1e30

TM_ROWS = 512
TS_ROWS = 2048
FF_CHUNK = 256
DEC_BLOCK = 16
CONV_ROWS = 64
VMEM_LIMIT = 56 * 1024 * 1024

F32 = jnp.float32
BF16 = jnp.bfloat16


def _params(n_axes):
    return pltpu.CompilerParams(dimension_semantics=("arbitrary",) * n_axes,
                                vmem_limit_bytes=VMEM_LIMIT)


def _resident(shape, index_map):
    return pl.BlockSpec(shape, index_map, pipeline_mode=pl.Buffered(1))


def _rms(x, g):
    return x * lax.rsqrt(jnp.mean(x * x, -1, keepdims=True) + EPS) * g


def _sigmoid(x):
    return 1.0 / (1.0 + jnp.exp(-x))


def _gelu_tanh(x):
    c = math.sqrt(2.0 / math.pi)
    return 0.5 * x * (1.0 + jnp.tanh(c * (x + 0.044715 * (x * x * x))))


def _dot(a, b):
    return jnp.dot(a, b, preferred_element_type=F32)


def _inproj_body(x_ref, g_ref, w_ref, z_ref, u_ref):
    h = _rms(x_ref[...], g_ref[...]).astype(BF16)
    z = _dot(h, w_ref[...])
    z_ref[...] = z[:, :C_END]
    u_ref[0] = z[:, C_END:C_END + LANES]
    u_ref[1] = z[:, C_END + LANES:]


def _inproj(x, g3, w_in, layer, tm):
    rows = x.shape[0]
    return pl.pallas_call(
        _inproj_body,
        grid=(rows // tm,),
        in_specs=[
            pl.BlockSpec((tm, D_MODEL), lambda i: (i, 0)),
            pl.BlockSpec((None, 1, D_MODEL), lambda i: (layer, 0, 0)),
            _resident((None, D_MODEL, IN_COLS), lambda i: (layer, 0, 0)),
        ],
        out_specs=[
            pl.BlockSpec((tm, C_END), lambda i: (i, 0)),
            pl.BlockSpec((2, tm, LANES), lambda i: (0, i, 0)),
        ],
        out_shape=[jax.ShapeDtypeStruct((rows, C_END), F32),
                   jax.ShapeDtypeStruct((2, rows, LANES), F32)],
        compiler_params=_params(1),
        name="inproj",
    )(x, g3, w_in)


def _tail_body(n_parts, final, *refs):
    x_ref, g_ref = refs[0], refs[1]
    parts = refs[2:2 + 2 * n_parts]
    wg_ref, wu_ref, wd_ref = refs[2 + 2 * n_parts:5 + 2 * n_parts]
    gf_ref = refs[5 + 2 * n_parts] if final else None
    o_ref, acc_s, hf_s = refs[-3:]

    x1 = x_ref[...]
    for p in range(n_parts):
        x1 = x1 + _dot(parts[2 * p][...].astype(BF16), parts[2 * p + 1][...])
    hf_s[...] = _rms(x1, g_ref[...]).astype(BF16)
    acc_s[...] = x1

    def ff_step(c, carry):
        hf = hf_s[...]
        gate = _dot(hf, wg_ref[c])
        up = _dot(hf, wu_ref[c])
        act = (gate * _sigmoid(gate) * up).astype(BF16)
        acc_s[...] += _dot(act, wd_ref[c])
        return carry

    lax.fori_loop(0, D_FF // FF_CHUNK, ff_step, 0)
    acc = acc_s[...]
    if final:
        acc = _rms(acc, gf_ref[...])
    o_ref[...] = acc


def _tail(x, g3, parts, wg, wu, wd, layer, tm, final_g=None):
    rows = x.shape[0]
    final = final_g is not None
    n_ff = D_FF // FF_CHUNK
    in_specs = [pl.BlockSpec((tm, D_MODEL), lambda i: (i, 0)),
                pl.BlockSpec((None, 1, D_MODEL), lambda i: (layer, 0, 0))]
    args = [x, g3]
    for act, spec, w in parts:
        in_specs.append(spec)
        in_specs.append(pl.BlockSpec((None,) + w.shape[1:], lambda i: (layer, 0, 0)))
        args += [act, w]
    in_specs += [
        _resident((None, n_ff, D_MODEL, FF_CHUNK), lambda i: (layer, 0, 0, 0)),
        _resident((None, n_ff, D_MODEL, FF_CHUNK), lambda i: (layer, 0, 0, 0)),
        _resident((None, n_ff, FF_CHUNK, D_MODEL), lambda i: (layer, 0, 0, 0)),
    ]
    args += [wg, wu, wd]
    if final:
        in_specs.append(pl.BlockSpec((1, D_MODEL), lambda i: (0, 0)))
        args.append(final_g)
    return pl.pallas_call(
        functools.partial(_tail_body, len(parts), final),
        grid=(rows // tm,),
        in_specs=in_specs,
        out_specs=pl.BlockSpec((tm, D_MODEL), lambda i: (i, 0)),
        out_shape=jax.ShapeDtypeStruct((rows, D_MODEL), F32),
        scratch_shapes=[pltpu.VMEM((tm, D_MODEL), F32), pltpu.VMEM((tm, D_MODEL), BF16)],
        compiler_params=_params(1),
        name="tail",
    )(*args)


def _expand_ssm_operands(k1_ref, m2_ref, n4_ref, w1_ref, w2_ref, w4_ref):
    col = lax.broadcasted_iota(jnp.int32, (SSM_GROUP, CHUNK_COLS), 1)
    col_chan_group = (col % SSM_CH) // SSM_GROUP
    col_state_group = (col % N_STATE) // SSM_STATE
    for l in range(SSM_CHUNK):
        k1 = k1_ref[l * SSM_GROUP:(l + 1) * SSM_GROUP, :]
        m2 = m2_ref[l * SSM_GROUP:(l + 1) * SSM_GROUP, :]
        for g in range(SSM_GROUPS):
            r0 = l * SSM_CH + g * SSM_GROUP
            w1_ref[r0:r0 + SSM_GROUP, :] = jnp.where(col_chan_group == g, k1, 0.0).astype(BF16)
            w2_ref[r0:r0 + SSM_GROUP, :] = jnp.where(col_state_group == g, m2, 0.0).astype(BF16)
    e_row = lax.broadcasted_iota(jnp.int32, (SSM_CHUNK * SSM_GROUP, CHUNK_COLS), 0)
    e_col = lax.broadcasted_iota(jnp.int32, (SSM_CHUNK * SSM_GROUP, CHUNK_COLS), 1)
    spread = ((e_row // SSM_GROUP == e_col // SSM_CH) & (e_row % SSM_GROUP == e_col % SSM_GROUP))
    spread = jnp.where(spread, 1.0, 0.0).astype(BF16)
    rows = lax.broadcasted_iota(jnp.int32, (LANES, CHUNK_COLS), 0)
    cols = lax.broadcasted_iota(jnp.int32, (LANES, CHUNK_COLS), 1)
    for b in range(2 * N_STATE // LANES):
        full = _dot(n4_ref[b * LANES:(b + 1) * LANES, :].astype(BF16), spread)
        row_group = ((rows + b * LANES) % N_STATE) // SSM_STATE
        keep = row_group == (cols % SSM_CH) // SSM_GROUP
        w4_ref[b * LANES:(b + 1) * LANES, :] = jnp.where(keep, full, 0.0).astype(BF16)


def _ssm_body(n_chunks, u_ref, k1_ref, m2_ref, n4_ref, al_ref, d_ref, gw_ref, gb_ref,
              o_ref, hl_ref, x_s, g_s, hp_s, hc_s, w1_ref, w2_ref, w4_ref):
    @pl.when((pl.program_id(0) == 0) & (pl.program_id(1) == 0))
    def _():
        _expand_ssm_operands(k1_ref, m2_ref, n4_ref, w1_ref, w2_ref, w4_ref)

    @pl.when(pl.program_id(1) == 0)
    def _():
        hc_s[...] = jnp.zeros_like(hc_s)

    for l in range(SSM_CHUNK):
        for s in range(2):
            c0 = l * SSM_CH + s * LANES
            x_s[:, c0:c0 + LANES] = u_ref[s, pl.ds(l, n_chunks, stride=SSM_CHUNK), :]
    xb = x_s[...].astype(BF16)

    g_s[...] = _dot(xb, w2_ref[...])

    a_re = al_ref[:, :N_STATE]
    a_im = al_ref[:, N_STATE:]

    def step(k, carry):
        h_re, h_im = carry
        hp_s[pl.ds(k, 1), :N_STATE] = h_re
        hp_s[pl.ds(k, 1), N_STATE:] = h_im
        g_re = g_s[pl.ds(k, 1), :N_STATE]
        g_im = g_s[pl.ds(k, 1), N_STATE:]
        return (a_re * h_re - a_im * h_im + g_re, a_re * h_im + a_im * h_re + g_im)

    h_re, h_im = lax.fori_loop(0, n_chunks, step, (hc_s[:, :N_STATE], hc_s[:, N_STATE:]))
    hc_s[:, :N_STATE] = h_re
    hc_s[:, N_STATE:] = h_im
    hl_ref[:, :N_STATE] = h_re
    hl_ref[:, N_STATE:] = h_im

    hb = hp_s[...].astype(BF16)
    for l in range(SSM_CHUNK):
        c0, c1 = l * SSM_CH, (l + 1) * SSM_CH
        y = _dot(xb[:, :c1], w1_ref[:c1, c0:c1]) + _dot(hb, w4_ref[:, c0:c1])
        y = _gelu_tanh(y + d_ref[...] * x_s[:, c0:c1])
        gate = _dot(y.astype(BF16), gw_ref[...]) + gb_ref[...]
        out = y * _sigmoid(gate)
        for s in range(2):
            o_ref[s, pl.ds(l, n_chunks, stride=SSM_CHUNK), :] = out[:, s * LANES:(s + 1) * LANES]


def _ssm_prompt(u, k1, m2, n4, al, d3, gw, gb3, layer, n_seq, seq_len):
    ts = TS_ROWS
    n_tiles = seq_len // ts
    n_chunks = ts // SSM_CHUNK
    rows = u.shape[1]
    compact = SSM_CHUNK * SSM_GROUP
    return pl.pallas_call(
        functools.partial(_ssm_body, n_chunks),
        grid=(n_seq, n_tiles),
        in_specs=[
            pl.BlockSpec((2, ts, LANES), lambda s, i: (0, s * n_tiles + i, 0)),
            pl.BlockSpec((None, compact, CHUNK_COLS), lambda s, i: (layer, 0, 0)),
            pl.BlockSpec((None, compact, 2 * N_STATE), lambda s, i: (layer, 0, 0)),
            pl.BlockSpec((None, 2 * N_STATE, compact), lambda s, i: (layer, 0, 0)),
            pl.BlockSpec((None, 1, 2 * N_STATE), lambda s, i: (layer, 0, 0)),
            pl.BlockSpec((None, 1, SSM_CH), lambda s, i: (layer, 0, 0)),
            pl.BlockSpec((None, SSM_CH, SSM_CH), lambda s, i: (layer, 0, 0)),
            pl.BlockSpec((None, 1, SSM_CH), lambda s, i: (layer, 0, 0)),
        ],
        out_specs=[
            pl.BlockSpec((2, ts, LANES), lambda s, i: (0, s * n_tiles + i, 0)),
            pl.BlockSpec((None, 1, 2 * N_STATE), lambda s, i: (s, 0, 0)),
        ],
        out_shape=[jax.ShapeDtypeStruct((2, rows, LANES), F32),
                   jax.ShapeDtypeStruct((n_seq, 1, 2 * N_STATE), F32)],
        scratch_shapes=[
            pltpu.VMEM((n_chunks, CHUNK_COLS), F32),
            pltpu.VMEM((n_chunks, 2 * N_STATE), F32),
            pltpu.VMEM((n_chunks, 2 * N_STATE), F32),
            pltpu.VMEM((1, 2 * N_STATE), F32),
            pltpu.VMEM((CHUNK_COLS, CHUNK_COLS), BF16),
            pltpu.VMEM((CHUNK_COLS, 2 * N_STATE), BF16),
            pltpu.VMEM((2 * N_STATE, CHUNK_COLS), BF16),
        ],
        compiler_params=_params(2),
        name="ssm_prompt",
    )(u, k1, m2, n4, al, d3, gw, gb3)


def _mix_body(tm, q_ref, kv_ref, a_ref, gg_ref, sink_ref, cw_ref, cb_ref, lg_ref, lb_ref,
              o_ref, ctx_ref, kv_s, u_s, ush_s, bias_s):
    first_tile = pl.program_id(1) == 0
    pad = 32
    off = pad - (CONV_WIDTH - 1)
    sub = 8

    @pl.when(first_tile)
    def _():
        kv_s[0:WINDOW, :] = jnp.zeros((WINDOW, 2 * LANES), F32)
        u_s[0:pad, :] = jnp.zeros((pad, CONV_CH), F32)
        qi = lax.broadcasted_iota(jnp.int32, (WINDOW, 2 * WINDOW), 0)
        ki = lax.broadcasted_iota(jnp.int32, (WINDOW, 2 * WINDOW), 1)
        dist = qi - ki + WINDOW
        valid = (dist >= 0) & (dist < WINDOW)
        distf = dist.astype(F32)
        for g in range(N_KV_HEADS):
            for r in range(KV_REP):
                slope = 2.0 ** (-8.0 * (g * KV_REP + r + 1) / N_HEADS)
                bias_s[g, r * WINDOW:(r + 1) * WINDOW, :] = jnp.where(valid, -slope * distf, NEG)

    kv_s[WINDOW:, :] = kv_ref[...]
    u_s[pad:, :] = a_ref[...] * _sigmoid(gg_ref[...])
    for b in range(1, sub):
        ush_s[b - 1] = u_s[b:b + tm + pad - sub, :]

    for r0 in range(0, tm, CONV_ROWS):
        acc = jnp.zeros((CONV_ROWS, CONV_CH), F32) + cb_ref[...]
        for j in range(CONV_WIDTH):
            a0, b = divmod(off + j, sub)
            lo = r0 + a0 * sub
            rows = u_s[lo:lo + CONV_ROWS, :] if b == 0 else ush_s[b - 1, lo:lo + CONV_ROWS, :]
            acc = acc + cw_ref[j:j + 1, :] * rows
        mu = jnp.mean(acc, -1, keepdims=True)
        cen = acc - mu
        var = jnp.mean(cen * cen, -1, keepdims=True)
        yn = cen * lax.rsqrt(var + EPS) * lg_ref[...] + lb_ref[...]
        o_ref[r0:r0 + CONV_ROWS, ATTN_WIDTH:] = yn * _sigmoid(yn)
    ctx_ref[...] = u_s[tm + off:tm + pad, :]

    scale = 1.0 / math.sqrt(HEAD_DIM)
    k_lane_group = lax.broadcasted_iota(jnp.int32, (2 * WINDOW, LANES), 1) // HEAD_DIM
    o_lane_group = lax.broadcasted_iota(jnp.int32, (KV_REP * WINDOW, LANES), 1) // HEAD_DIM
    key_in_prev = lax.broadcasted_iota(jnp.int32, (1, 2 * WINDOW), 1) < WINDOW
    no_prev_block = jnp.logical_and(key_in_prev, first_tile)
    sinks = [jnp.concatenate([jnp.broadcast_to(sink_ref[:, g * KV_REP + r:g * KV_REP + r + 1], (WINDOW, 1))
                              for r in range(KV_REP)], 0) for g in range(N_KV_HEADS)]

    for blk in range(tm // WINDOW):
        r0 = blk * WINDOW
        kblk = kv_s[r0:r0 + 2 * WINDOW, 0:LANES]
        vblk = kv_s[r0:r0 + 2 * WINDOW, LANES:].astype(BF16)
        qs = jnp.concatenate([q_ref[r0:r0 + WINDOW, r * LANES:(r + 1) * LANES] for r in range(KV_REP)], 0)
        qs = (qs * scale).astype(BF16)
        o = None
        for g in range(N_KV_HEADS):
            kg = jnp.where(k_lane_group == g, kblk, 0.0).astype(BF16)
            s = lax.dot_general(qs, kg, (((1,), (1,)), ((), ())), preferred_element_type=F32) + bias_s[g]
            if blk == 0:
                s = jnp.where(no_prev_block, NEG, s)
            m = jnp.maximum(jnp.max(s, -1, keepdims=True), sinks[g])
            p = jnp.exp(s - m)
            denom = jnp.sum(p, -1, keepdims=True) + jnp.exp(sinks[g] - m)
            og = _dot(p.astype(BF16), vblk) / denom
            o = og if g == 0 else jnp.where(o_lane_group == 0, o, og)
        for r in range(KV_REP):
            o_ref[r0:r0 + WINDOW, r * LANES:(r + 1) * LANES] = o[r * WINDOW:(r + 1) * WINDOW, :]

    kv_s[0:WINDOW, :] = kv_s[tm:tm + WINDOW, :]
    u_s[0:pad, :] = u_s[tm:tm + pad, :]


def _mix_prompt(z, sinks3, cw, cb3, lg3, lb3, layer, n_seq, seq_len):
    tm = TM_ROWS
    n_tiles = seq_len // tm
    rows = n_seq * seq_len
    row = lambda s, i: s * n_tiles + i
    vec = lambda width: pl.BlockSpec((None, 1, width), lambda s, i: (layer, 0, 0))
    return pl.pallas_call(
        functools.partial(_mix_body, tm),
        grid=(n_seq, n_tiles),
        in_specs=[
            pl.BlockSpec((tm, ATTN_WIDTH), lambda s, i: (row(s, i), 0)),
            pl.BlockSpec((tm, 2 * LANES), lambda s, i: (row(s, i), Q_END // (2 * LANES))),
            pl.BlockSpec((tm, CONV_CH), lambda s, i: (row(s, i), V_END // CONV_CH)),
            pl.BlockSpec((tm, CONV_CH), lambda s, i: (row(s, i), V_END // CONV_CH + 1)),
            vec(N_HEADS),
            pl.BlockSpec((None, CONV_WIDTH, CONV_CH), lambda s, i: (layer, 0, 0)),
            vec(CONV_CH), vec(CONV_CH), vec(CONV_CH),
        ],
        out_specs=[
            pl.BlockSpec((tm, V_END), lambda s, i: (row(s, i), 0)),
            pl.BlockSpec((None, CONV_WIDTH - 1, CONV_CH), lambda s, i: (s, 0, 0)),
        ],
        out_shape=[jax.ShapeDtypeStruct((rows, V_END), F32),
                   jax.ShapeDtypeStruct((n_seq, CONV_WIDTH - 1, CONV_CH), F32)],
        scratch_shapes=[pltpu.VMEM((tm + WINDOW, 2 * LANES), F32),
                        pltpu.VMEM((tm + 32, CONV_CH), F32),
                        pltpu.VMEM((7, tm + 24, CONV_CH), F32),
                        pltpu.VMEM((N_KV_HEADS, KV_REP * WINDOW, 2 * WINDOW), F32)],
        compiler_params=_params(2),
        name="mix_prompt",
    )(z, z, z, z, sinks3, cw, cb3, lg3, lb3)


def _dec_body(q_ref, kn_ref, vn_ref, ck_ref, cv_ref, a_ref, gg_ref, cc_ref, u_ref, hr_ref, hi_ref,
              sink_ref, cw_ref, cb_ref, lg_ref, lb_ref, bm_ref, cm_ref, ab_ref, d_ref, gw_ref, gb_ref,
              o_ref, ok_ref, ov_ref, co_ref, oc_ref, so_ref, or_ref, oi_ref):
    nb = DEC_BLOCK
    win = ck_ref.shape[1]

    q3 = q_ref[...].reshape(nb, N_HEADS, LANES)
    kn = kn_ref[...]
    vn = vn_ref[...]
    kb = ck_ref[...].astype(BF16)
    vb = cv_ref[...].astype(BF16)
    s = jnp.einsum("nsc,njc->nsj", q3, kb, preferred_element_type=F32)
    si = lax.broadcasted_iota(jnp.int32, (N_HEADS, win), 0)
    ji = lax.broadcasted_iota(jnp.int32, (N_HEADS, win), 1)
    head = (si % 2) * KV_REP + si // 2
    slope = jnp.zeros((N_HEADS, win), F32)
    for h in range(N_HEADS):
        slope = jnp.where(head == h, 2.0 ** (-8.0 * (h + 1) / N_HEADS), slope)
    dist = win - ji
    bias = jnp.where(dist < WINDOW, -slope * dist.astype(F32), NEG)
    s = s + bias[None]
    s_new = jnp.sum(q3.astype(F32) * kn.astype(BF16).astype(F32), -1, keepdims=True)
    sink = sink_ref[...][None]
    m = jnp.maximum(jnp.maximum(jnp.max(s, -1, keepdims=True), s_new), sink)
    p = jnp.exp(s - m)
    p_new = jnp.exp(s_new - m)
    denom = jnp.sum(p, -1, keepdims=True) + p_new + jnp.exp(sink - m)
    o = jnp.einsum("nsj,njc->nsc", p.astype(BF16), vb, preferred_element_type=F32)
    o = o + p_new.astype(BF16).astype(F32) * vn.astype(BF16).astype(F32)
    o_ref[...] = (o / denom).reshape(nb * N_HEADS, LANES)
    ok_ref[:, 0:win - 1, :] = ck_ref[:, 1:win, :]
    ok_ref[:, win - 1:win, :] = kn
    ov_ref[:, 0:win - 1, :] = cv_ref[:, 1:win, :]
    ov_ref[:, win - 1:win, :] = vn

    u = a_ref[...] * _sigmoid(gg_ref[...])
    acc = cb_ref[...] + cw_ref[CONV_WIDTH - 1:CONV_WIDTH, :] * u
    for j in range(CONV_WIDTH - 1):
        acc = acc + cw_ref[j:j + 1, :] * cc_ref[j]
    mu = jnp.mean(acc, -1, keepdims=True)
    cen = acc - mu
    var = jnp.mean(cen * cen, -1, keepdims=True)
    yn = cen * lax.rsqrt(var + EPS) * lg_ref[...] + lb_ref[...]
    co_ref[...] = yn * _sigmoid(yn)
    for j in range(CONV_WIDTH - 2):
        oc_ref[j] = cc_ref[j + 1]
    oc_ref[CONV_WIDTH - 2] = u

    us = jnp.concatenate([u_ref[0], u_ref[1]], -1)
    bu = _dot(us.astype(BF16), bm_ref[...])
    a_re = ab_ref[:, :N_STATE]
    a_im = ab_ref[:, N_STATE:]
    h_re = hr_ref[...]
    h_im = hi_ref[...]
    n_re = a_re * h_re - a_im * h_im + bu[:, :N_STATE]
    n_im = a_re * h_im + a_im * h_re + bu[:, N_STATE:]
    or_ref[...] = n_re
    oi_ref[...] = n_im
    hcat = jnp.concatenate([n_re, n_im], -1).astype(BF16)
    y = _gelu_tanh(_dot(hcat, cm_ref[...]) + d_ref[...] * us)
    gate = _dot(y.astype(BF16), gw_ref[...]) + gb_ref[...]
    so_ref[...] = y * _sigmoid(gate)


def _dec_mix(q3, kn3, vn3, ck, cv, z, cct, u, hr, hi, sinks3, cw, cb3, lg3, lb3,
             bmat, cmat, ab3, d3, gw, gb3, layer):
    n = kn3.shape[0]
    win = ck.shape[2]
    nb = DEC_BLOCK
    vec = lambda width: pl.BlockSpec((None, 1, width), lambda i: (layer, 0, 0))
    mat = lambda r, c: pl.BlockSpec((None, r, c), lambda i: (layer, 0, 0))
    cache = pl.BlockSpec((None, nb, win, LANES), lambda i: (layer, i, 0, 0))
    ctx = pl.BlockSpec((None, CONV_WIDTH - 1, nb, CONV_CH), lambda i: (layer, 0, i, 0))
    state = pl.BlockSpec((None, nb, N_STATE), lambda i: (layer, i, 0))
    rowblk = lambda width: pl.BlockSpec((nb, width), lambda i: (i, 0))
    new3 = pl.BlockSpec((nb, 1, LANES), lambda i: (i, 0, 0))
    return pl.pallas_call(
        _dec_body,
        grid=(n // nb,),
        in_specs=[
            pl.BlockSpec((nb * N_HEADS, LANES), lambda i: (i, 0)), new3, new3, cache, cache,
            pl.BlockSpec((nb, CONV_CH), lambda i: (i, V_END // CONV_CH)),
            pl.BlockSpec((nb, CONV_CH), lambda i: (i, V_END // CONV_CH + 1)),
            ctx,
            pl.BlockSpec((2, nb, LANES), lambda i: (0, i, 0)),
            state, state,
            mat(N_HEADS, 1), mat(CONV_WIDTH, CONV_CH), vec(CONV_CH), vec(CONV_CH), vec(CONV_CH),
            mat(SSM_CH, 2 * N_STATE), mat(2 * N_STATE, SSM_CH), vec(2 * N_STATE), vec(SSM_CH),
            mat(SSM_CH, SSM_CH), vec(SSM_CH),
        ],
        out_specs=[
            pl.BlockSpec((nb * N_HEADS, LANES), lambda i: (i, 0)),
            pl.BlockSpec((nb, win, LANES), lambda i: (i, 0, 0)),
            pl.BlockSpec((nb, win, LANES), lambda i: (i, 0, 0)),
            rowblk(CONV_CH),
            pl.BlockSpec((CONV_WIDTH - 1, nb, CONV_CH), lambda i: (0, i, 0)),
            rowblk(SSM_CH), rowblk(N_STATE), rowblk(N_STATE),
        ],
        out_shape=[
            jax.ShapeDtypeStruct((n * N_HEADS, LANES), F32),
            jax.ShapeDtypeStruct((n, win, LANES), F32),
            jax.ShapeDtypeStruct((n, win, LANES), F32),
            jax.ShapeDtypeStruct((n, CONV_CH), F32),
            jax.ShapeDtypeStruct((CONV_WIDTH - 1, n, CONV_CH), F32),
            jax.ShapeDtypeStruct((n, SSM_CH), F32),
            jax.ShapeDtypeStruct((n, N_STATE), F32),
            jax.ShapeDtypeStruct((n, N_STATE), F32),
        ],
        compiler_params=_params(1),
        name="dec_mix",
    )(q3, kn3, vn3, ck, cv, z, z, cct, u, hr, hi, sinks3, cw, cb3, lg3, lb3,
      bmat, cmat, ab3, d3, gw, gb3)


def _ssm_operands(a_re, a_im, log_dt, b_re, b_im, c_re, c_im):
    hi = lax.Precision.HIGHEST
    l_n = SSM_CHUNK
    dt = jnp.exp(log_dt)[:, None]
    mag = jnp.exp(a_re * dt)
    ab_re, ab_im = mag * jnp.cos(a_im * dt), mag * jnp.sin(a_im * dt)
    den = a_re * a_re + a_im * a_im
    q_re = ((ab_re - 1.0) * a_re + ab_im * a_im) / den
    q_im = (ab_im * a_re - (ab_re - 1.0) * a_im) / den
    bb_re = q_re[..., None] * b_re - q_im[..., None] * b_im
    bb_im = q_re[..., None] * b_im + q_im[..., None] * b_re
    pw_re, pw_im = [jnp.ones_like(ab_re)], [jnp.zeros_like(ab_re)]
    for _ in range(l_n):
        r, i = pw_re[-1], pw_im[-1]
        pw_re.append(r * ab_re - i * ab_im)
        pw_im.append(r * ab_im + i * ab_re)
    pb_re = jnp.stack([pw_re[t][..., None] * bb_re - pw_im[t][..., None] * bb_im for t in range(l_n)])
    pb_im = jnp.stack([pw_re[t][..., None] * bb_im + pw_im[t][..., None] * bb_re for t in range(l_n)])

    ktau = (jnp.einsum("gcp,tgpd->tgcd", c_re, pb_re, precision=hi)
            - jnp.einsum("gcp,tgpd->tgcd", c_im, pb_im, precision=hi))
    zero = jnp.zeros_like(ktau[0])
    k1 = jnp.stack([jnp.stack([ktau[lo - li] if lo >= li else zero for lo in range(l_n)])
                    for li in range(l_n)])
    k1 = jnp.transpose(k1, (0, 4, 1, 2, 3)).reshape(l_n * SSM_GROUP, CHUNK_COLS)
    m2 = jnp.stack([jnp.stack([pb_re[l_n - 1 - l], pb_im[l_n - 1 - l]]) for l in range(l_n)])
    m2 = jnp.transpose(m2, (0, 4, 1, 2, 3)).reshape(l_n * SSM_GROUP, 2 * N_STATE)
    n4_re = jnp.stack([c_re * pw_re[l + 1][:, None, :] - c_im * pw_im[l + 1][:, None, :] for l in range(l_n)])
    n4_im = jnp.stack([c_re * pw_im[l + 1][:, None, :] + c_im * pw_re[l + 1][:, None, :] for l in range(l_n)])
    n4 = jnp.transpose(jnp.stack([n4_re, -n4_im]), (0, 2, 4, 1, 3)).reshape(2 * N_STATE, l_n * SSM_GROUP)

    chan_group = jnp.arange(SSM_CH) // SSM_GROUP
    state_group = (jnp.arange(2 * N_STATE) % N_STATE) // SSM_STATE
    bmat = jnp.where(chan_group[:, None] == state_group[None, :],
                     jnp.tile(m2[(l_n - 1) * SSM_GROUP:], (SSM_GROUPS, 1)), 0.0)
    cc = jnp.transpose(jnp.stack([c_re, -c_im]), (0, 1, 3, 2)).reshape(2 * N_STATE, SSM_GROUP)
    cmat = jnp.where(state_group[:, None] == chan_group[None, :], jnp.tile(cc, (1, SSM_GROUPS)), 0.0)

    flat = lambda re, im: jnp.concatenate([re.reshape(1, N_STATE), im.reshape(1, N_STATE)], -1)
    return (k1, m2, n4, bmat.astype(BF16), cmat.astype(BF16),
            flat(pw_re[l_n], pw_im[l_n]), flat(ab_re, ab_im))


def _decode_head_order():
    s = jnp.arange(N_HEADS)
    return (s % 2) * KV_REP + s // 2


def kernel(x_prompt, x_sample, cache_swa_k, cache_swa_v, cache_conv, state_ssm_re, state_ssm_im,
           norm_mix_g, w_in, attn_sinks, conv_dw_w, conv_dw_b, conv_ln_g, conv_ln_b,
           ssm_a_re, ssm_a_im, ssm_log_dt, ssm_b_re, ssm_b_im, ssm_c_re, ssm_c_im,
           ssm_d, ssm_glu_w, ssm_glu_b, w_out, norm_ffn_g, w_ff_gate, w_ff_up, w_ff_down,
           norm_final_g):
    n_seq, seq_len, _ = x_prompt.shape
    n_dec = x_sample.shape[0]
    win = cache_swa_k.shape[2]
    assert x_sample.shape[1] == 1 and win == WINDOW
    assert seq_len % TS_ROWS == 0 and n_dec % DEC_BLOCK == 0

    row3 = lambda v: v.reshape(DEPTH, 1, -1)
    g_mix, g_ffn = row3(norm_mix_g), row3(norm_ffn_g)
    sinks3 = row3(attn_sinks)
    order = _decode_head_order()
    sinks_dec = attn_sinks[:, order][:, :, None]
    cb3, lg3, lb3 = row3(conv_dw_b), row3(conv_ln_g), row3(conv_ln_b)
    d3, gb3 = row3(ssm_d), row3(ssm_glu_b)
    w_in_q = jnp.transpose(w_in[:, :, :Q_END].reshape(DEPTH, D_MODEL, N_KV_HEADS, KV_REP, HEAD_DIM),
                           (0, 1, 3, 2, 4)).reshape(DEPTH, D_MODEL, Q_END)
    w_in_b = jnp.concatenate([w_in_q, w_in[:, :, Q_END:]], -1).astype(BF16)
    w_out_b = w_out.astype(BF16)
    n_ff = D_FF // FF_CHUNK
    col_chunks = lambda w: jnp.transpose(w.astype(BF16).reshape(DEPTH, D_MODEL, n_ff, FF_CHUNK), (0, 2, 1, 3))
    wg_b, wu_b = col_chunks(w_ff_gate), col_chunks(w_ff_up)
    wd_b = w_ff_down.astype(BF16).reshape(DEPTH, n_ff, FF_CHUNK, D_MODEL)
    gw_b = ssm_glu_w.astype(BF16)
    k1, m2, n4, bmat, cmat, a_chunk, a_step = jax.vmap(_ssm_operands)(
        ssm_a_re, ssm_a_im, ssm_log_dt, ssm_b_re, ssm_b_im, ssm_c_re, ssm_c_im)

    wo_heads = w_out_b[:, :ATTN_WIDTH].reshape(DEPTH, N_HEADS, HEAD_DIM, D_MODEL)[:, order]
    own = (jnp.arange(N_HEADS)[:, None] % 2) == jnp.arange(N_KV_HEADS)[None, :]
    wo_dec = jnp.where(own[None, :, :, None, None], wo_heads[:, :, None], 0).reshape(
        DEPTH, N_HEADS * LANES, D_MODEL)
    wo_attn = jnp.transpose(w_out_b[:, :ATTN_WIDTH].reshape(DEPTH, N_KV_HEADS, KV_REP, HEAD_DIM, D_MODEL),
                            (0, 2, 1, 3, 4)).reshape(DEPTH, ATTN_WIDTH, D_MODEL)
    wo_attn_conv = jnp.concatenate([wo_attn, w_out_b[:, ATTN_WIDTH:V_END]], 1)
    wo_conv = w_out_b[:, ATTN_WIDTH:V_END]
    wo_ssm0 = w_out_b[:, V_END:V_END + LANES]
    wo_ssm1 = w_out_b[:, V_END + LANES:]
    wo_ssm = w_out_b[:, V_END:]

    ck = cache_swa_k.reshape(DEPTH, n_dec, win, LANES)
    cv = cache_swa_v.reshape(DEPTH, n_dec, win, LANES)
    cct = jnp.transpose(cache_conv, (0, 2, 1, 3))
    hr = state_ssm_re.reshape(DEPTH, n_dec, N_STATE)
    hi = state_ssm_im.reshape(DEPTH, n_dec, N_STATE)

    xp = x_prompt.reshape(n_seq * seq_len, D_MODEL)
    xs = x_sample.reshape(n_dec, D_MODEL)
    tm = TM_ROWS
    rowspec = lambda width: pl.BlockSpec((tm, width), lambda i: (i, 0))
    slabspec = lambda s: pl.BlockSpec((None, tm, LANES), lambda i: (s, i, 0))
    decspec = lambda width: pl.BlockSpec((n_dec, width), lambda i: (i, 0))
    scale = 1.0 / math.sqrt(HEAD_DIM)
    own_lane = ((jnp.arange(LANES) // HEAD_DIM)[None, None, None, :]
                == jnp.arange(N_KV_HEADS)[None, None, :, None])

    kp, vp, cp, hrp, hip = [], [], [], [], []
    ks, vs, cs, hrs, his = [], [], [], [], []
    for l in range(DEPTH):
        final = norm_final_g.reshape(1, D_MODEL) if l == DEPTH - 1 else None

        z, u = _inproj(xp, g_mix, w_in_b, l, tm)
        ssm, h_last = _ssm_prompt(u, k1, m2, n4, a_chunk, d3, gw_b, gb3, l, n_seq, seq_len)
        mix, ctx = _mix_prompt(z, sinks3, conv_dw_w, cb3, lg3, lb3, l, n_seq, seq_len)
        xp = _tail(xp, g_ffn,
                   [(mix, rowspec(V_END), wo_attn_conv),
                    (ssm, slabspec(0), wo_ssm0),
                    (ssm, slabspec(1), wo_ssm1)],
                   wg_b, wu_b, wd_b, l, tm, final)
        z3 = z.reshape(n_seq, seq_len, C_END)[:, seq_len - WINDOW:]
        kp.append(z3[..., Q_END:K_END].reshape(n_seq, WINDOW, N_KV_HEADS, HEAD_DIM))
        vp.append(z3[..., K_END:V_END].reshape(n_seq, WINDOW, N_KV_HEADS, HEAD_DIM))
        cp.append(ctx)
        hrp.append(h_last[:, 0, :N_STATE].reshape(n_seq, SSM_GROUPS, SSM_STATE))
        hip.append(h_last[:, 0, N_STATE:].reshape(n_seq, SSM_GROUPS, SSM_STATE))

        zs, us = _inproj(xs, g_mix, w_in_b, l, n_dec)
        zq = (zs[:, :Q_END] * scale).reshape(n_dec, KV_REP, 1, LANES)
        q3 = jnp.where(own_lane, zq, 0.0).astype(BF16).reshape(n_dec * N_HEADS, LANES)
        kn3 = zs[:, Q_END:K_END].reshape(n_dec, 1, LANES)
        vn3 = zs[:, K_END:V_END].reshape(n_dec, 1, LANES)
        o3, nk, nv, conv_s, nct, ssm_s, nhr, nhi = _dec_mix(
            q3, kn3, vn3, ck, cv, zs, cct, us, hr, hi, sinks_dec, conv_dw_w, cb3, lg3, lb3,
            bmat, cmat, a_step, d3, gw_b, gb3, l)
        xs = _tail(xs, g_ffn,
                   [(o3.reshape(n_dec, N_HEADS * LANES), decspec(N_HEADS * LANES), wo_dec),
                    (conv_s, decspec(CONV_CH), wo_conv),
                    (ssm_s, decspec(SSM_CH), wo_ssm)],
                   wg_b, wu_b, wd_b, l, n_dec, final)
        ks.append(nk.reshape(n_dec, win, N_KV_HEADS, HEAD_DIM))
        vs.append(nv.reshape(n_dec, win, N_KV_HEADS, HEAD_DIM))
        cs.append(jnp.transpose(nct, (1, 0, 2)))
        hrs.append(nhr.reshape(n_dec, SSM_GROUPS, SSM_STATE))
        his.append(nhi.reshape(n_dec, SSM_GROUPS, SSM_STATE))

    return (xp.reshape(n_seq, seq_len, D_MODEL), xs.reshape(n_dec, 1, D_MODEL),
            jnp.stack(kp), jnp.stack(vp), jnp.stack(cp), jnp.stack(hrp), jnp.stack(hip),
            jnp.stack(ks), jnp.stack(vs), jnp.stack(cs), jnp.stack(hrs), jnp.stack(his))
```

```python
import functools
import math

import jax
import jax.numpy as jnp
from jax import lax
from jax.experimental import pallas as pl
from jax.experimental.pallas import tpu as pltpu

D_MODEL = 1024
DEPTH = 4
HEAD_DIM = 64
ATTN_WIDTH = 512
N_HEADS = 8
N_KV_HEADS = 2
KV_REP = 4
WINDOW = 128
CONV_CH = 256
CONV_WIDTH = 31
SSM_CH = 256
SSM_GROUP = 16
SSM_GROUPS = 16
SSM_STATE = 64
D_FF = 2816
EPS = 1e-6

Q_END = ATTN_WIDTH
K_END = Q_END + N_KV_HEADS * HEAD_DIM
V_END = K_END + N_KV_HEADS * HEAD_DIM
C_END = V_END + 2 * CONV_CH
IN_COLS = C_END + SSM_CH

N_STATE = SSM_GROUPS * SSM_STATE
LANES = 128
SSM_CHUNK = 8
CHUNK_COLS = SSM_CHUNK * SSM_CH
NEG = -1e30

TM_ROWS = 512
TS_ROWS = 2048
FF_CHUNK = 256
DEC_BLOCK = 16
CONV_ROWS = 64
VMEM_LIMIT = 56 * 1024 * 1024

F32 = jnp.float32
BF16 = jnp.bfloat16


def _params(n_axes):
    return pltpu.CompilerParams(dimension_semantics=("arbitrary",) * n_axes,
                                vmem_limit_bytes=VMEM_LIMIT)


def _resident(shape, index_map):
    return pl.BlockSpec(shape, index_map, pipeline_mode=pl.Buffered(1))


def _rms(x, g):
    return x * lax.rsqrt(jnp.mean(x * x, -1, keepdims=True) + EPS) * g


def _sigmoid(x):
    return 1.0 / (1.0 + jnp.exp(-x))


def _gelu_tanh(x):
    c = math.sqrt(2.0 / math.pi)
    return 0.5 * x * (1.0 + jnp.tanh(c * (x + 0.044715 * (x * x * x))))


def _dot(a, b):
    return jnp.dot(a, b, preferred_element_type=F32)


def _inproj_body(x_ref, g_ref, wq_ref, wr_ref, z_ref, u_ref):
    h = _rms(x_ref[...], g_ref[...]).astype(BF16)
    z_ref[:, :Q_END] = _dot(h, wq_ref[...])
    zr = _dot(h, wr_ref[...])
    z_ref[:, Q_END:] = zr[:, :C_END - Q_END]
    u_ref[0] = zr[:, C_END - Q_END:C_END - Q_END + LANES]
    u_ref[1] = zr[:, C_END - Q_END + LANES:]


def _inproj(x, g3, w_q, w_rest, layer, tm):
    rows = x.shape[0]
    return pl.pallas_call(
        _inproj_body,
        grid=(rows // tm,),
        in_specs=[
            pl.BlockSpec((tm, D_MODEL), lambda i: (i, 0)),
            pl.BlockSpec((None, 1, D_MODEL), lambda i: (layer, 0, 0)),
            _resident((None, D_MODEL, Q_END), lambda i: (layer, 0, 0)),
            _resident((None, D_MODEL, IN_COLS - Q_END), lambda i: (layer, 0, 0)),
        ],
        out_specs=[
            pl.BlockSpec((tm, C_END), lambda i: (i, 0)),
            pl.BlockSpec((2, tm, LANES), lambda i: (0, i, 0)),
        ],
        out_shape=[jax.ShapeDtypeStruct((rows, C_END), F32),
                   jax.ShapeDtypeStruct((2, rows, LANES), F32)],
        compiler_params=_params(1),
        name="inproj",
    )(x, g3, w_q, w_rest)


def _tail_body(n_parts, final, *refs):
    x_ref, g_ref = refs[0], refs[1]
    parts = refs[2:2 + 2 * n_parts]
    wg_ref, wu_ref, wd_ref = refs[2 + 2 * n_parts:5 + 2 * n_parts]
    gf_ref = refs[5 + 2 * n_parts] if final else None
    o_ref, x1_s, hf_s, act_s = refs[-4:]

    x1 = x_ref[...]
    for p in range(n_parts):
        act_ref = parts[2 * p]
        if len(act_ref.shape) == 3:
            act = jnp.concatenate([act_ref[i] for i in range(act_ref.shape[0])], -1)
        else:
            act = act_ref[...]
        x1 = x1 + _dot(act.astype(BF16), parts[2 * p + 1][...])
    hf_s[...] = _rms(x1, g_ref[...]).astype(BF16)
    x1_s[...] = x1

    for c in range(0, D_FF, FF_CHUNK):
        gate = _dot(hf_s[...], wg_ref[:, c:c + FF_CHUNK])
        up = _dot(hf_s[...], wu_ref[:, c:c + FF_CHUNK])
        act_s[:, c:c + FF_CHUNK] = (gate * _sigmoid(gate) * up).astype(BF16)
    acc = x1_s[...] + _dot(act_s[...], wd_ref[...])
    if final:
        acc = _rms(acc, gf_ref[...])
    o_ref[...] = acc


def _tail(x, g3, parts, wg, wu, wd, layer, tm, final_g=None):
    rows = x.shape[0]
    final = final_g is not None
    in_specs = [pl.BlockSpec((tm, D_MODEL), lambda i: (i, 0)),
                pl.BlockSpec((None, 1, D_MODEL), lambda i: (layer, 0, 0))]
    args = [x, g3]
    for act, spec, w in parts:
        in_specs.append(spec)
        in_specs.append(pl.BlockSpec((None,) + w.shape[1:], lambda i: (layer, 0, 0)))
        args += [act, w]
    in_specs += [
        _resident((None, D_MODEL, D_FF), lambda i: (layer, 0, 0)),
        _resident((None, D_MODEL, D_FF), lambda i: (layer, 0, 0)),
        _resident((None, D_FF, D_MODEL), lambda i: (layer, 0, 0)),
    ]
    args += [wg, wu, wd]
    if final:
        in_specs.append(pl.BlockSpec((1, D_MODEL), lambda i: (0, 0)))
        args.append(final_g)
    return pl.pallas_call(
        functools.partial(_tail_body, len(parts), final),
        grid=(rows // tm,),
        in_specs=in_specs,
        out_specs=pl.BlockSpec((tm, D_MODEL), lambda i: (i, 0)),
        out_shape=jax.ShapeDtypeStruct((rows, D_MODEL), F32),
        scratch_shapes=[pltpu.VMEM((tm, D_MODEL), F32), pltpu.VMEM((tm, D_MODEL), BF16),
                        pltpu.VMEM((tm, D_FF), BF16)],
        compiler_params=_params(1),
        name="tail",
    )(*args)


def _expand_ssm_operands(k1_ref, m2_ref, n4_ref, w1_ref, w2_ref, w4_ref):
    col = lax.broadcasted_iota(jnp.int32, (SSM_GROUP, CHUNK_COLS), 1)
    col_chan_group = (col % SSM_CH) // SSM_GROUP
    col_state_group = (col % N_STATE) // SSM_STATE
    for l in range(SSM_CHUNK):
        k1 = k1_ref[l * SSM_GROUP:(l + 1) * SSM_GROUP, :]
        m2 = m2_ref[l * SSM_GROUP:(l + 1) * SSM_GROUP, :]
        for g in range(SSM_GROUPS):
            r0 = l * SSM_CH + g * SSM_GROUP
            w1_ref[r0:r0 + SSM_GROUP, :] = jnp.where(col_chan_group == g, k1, 0.0).astype(BF16)
            w2_ref[r0:r0 + SSM_GROUP, :] = jnp.where(col_state_group == g, m2, 0.0).astype(BF16)
    e_row = lax.broadcasted_iota(jnp.int32, (SSM_CHUNK * SSM_GROUP, CHUNK_COLS), 0)
    e_col = lax.broadcasted_iota(jnp.int32, (SSM_CHUNK * SSM_GROUP, CHUNK_COLS), 1)
    spread = ((e_row // SSM_GROUP == e_col // SSM_CH) & (e_row % SSM_GROUP == e_col % SSM_GROUP))
    spread = jnp.where(spread, 1.0, 0.0).astype(BF16)
    rows = lax.broadcasted_iota(jnp.int32, (LANES, CHUNK_COLS), 0)
    cols = lax.broadcasted_iota(jnp.int32, (LANES, CHUNK_COLS), 1)
    for b in range(2 * N_STATE // LANES):
        full = _dot(n4_ref[b * LANES:(b + 1) * LANES, :].astype(BF16), spread)
        row_group = ((rows + b * LANES) % N_STATE) // SSM_STATE
        keep = row_group == (cols % SSM_CH) // SSM_GROUP
        w4_ref[b * LANES:(b + 1) * LANES, :] = jnp.where(keep, full, 0.0).astype(BF16)


def _ssm_body(n_chunks, u_ref, k1_ref, m2_ref, n4_ref, al_ref, d_ref, gw_ref, gb_ref,
              o_ref, hl_ref, x_s, g_s, hp_s, hc_s, w1_ref, w2_ref, w4_ref):
    @pl.when((pl.program_id(0) == 0) & (pl.program_id(1) == 0))
    def _():
        _expand_ssm_operands(k1_ref, m2_ref, n4_ref, w1_ref, w2_ref, w4_ref)

    @pl.when(pl.program_id(1) == 0)
    def _():
        hc_s[...] = jnp.zeros_like(hc_s)

    for l in range(SSM_CHUNK):
        for s in range(2):
            c0 = l * SSM_CH + s * LANES
            x_s[:, c0:c0 + LANES] = u_ref[s, pl.ds(l, n_chunks, stride=SSM_CHUNK), :]
    xb = x_s[...].astype(BF16)

    g_s[...] = _dot(xb, w2_ref[...])

    a_re = al_ref[:, :N_STATE]
    a_im = al_ref[:, N_STATE:]

    def step(k, carry):
        h_re, h_im = carry
        hp_s[pl.ds(k, 1), :N_STATE] = h_re
        hp_s[pl.ds(k, 1), N_STATE:] = h_im
        g_re = g_s[pl.ds(k, 1), :N_STATE]
        g_im = g_s[pl.ds(k, 1), N_STATE:]
        return (a_re * h_re - a_im * h_im + g_re, a_re * h_im + a_im * h_re + g_im)

    h_re, h_im = lax.fori_loop(0, n_chunks, step, (hc_s[:, :N_STATE], hc_s[:, N_STATE:]))
    hc_s[:, :N_STATE] = h_re
    hc_s[:, N_STATE:] = h_im
    hl_ref[:, :N_STATE] = h_re
    hl_ref[:, N_STATE:] = h_im

    hb = hp_s[...].astype(BF16)
    for l in range(SSM_CHUNK):
        c0, c1 = l * SSM_CH, (l + 1) * SSM_CH
        y = _dot(xb[:, :c1], w1_ref[:c1, c0:c1]) + _dot(hb, w4_ref[:, c0:c1])
        y = _gelu_tanh(y + d_ref[...] * x_s[:, c0:c1])
        gate = _dot(y.astype(BF16), gw_ref[...]) + gb_ref[...]
        out = y * _sigmoid(gate)
        for s in range(2):
            o_ref[s, pl.ds(l, n_chunks, stride=SSM_CHUNK), :] = out[:, s * LANES:(s + 1) * LANES]


def _ssm_prompt(u, k1, m2, n4, al, d3, gw, gb3, layer, n_seq, seq_len):
    ts = TS_ROWS
    n_tiles = seq_len // ts
    n_chunks = ts // SSM_CHUNK
    rows = u.shape[1]
    compact = SSM_CHUNK * SSM_GROUP
    return pl.pallas_call(
        functools.partial(_ssm_body, n_chunks),
        grid=(n_seq, n_tiles),
        in_specs=[
            pl.BlockSpec((2, ts, LANES), lambda s, i: (0, s * n_tiles + i, 0)),
            pl.BlockSpec((None, compact, CHUNK_COLS), lambda s, i: (layer, 0, 0)),
            pl.BlockSpec((None, compact, 2 * N_STATE), lambda s, i: (layer, 0, 0)),
            pl.BlockSpec((None, 2 * N_STATE, compact), lambda s, i: (layer, 0, 0)),
            pl.BlockSpec((None, 1, 2 * N_STATE), lambda s, i: (layer, 0, 0)),
            pl.BlockSpec((None, 1, SSM_CH), lambda s, i: (layer, 0, 0)),
            pl.BlockSpec((None, SSM_CH, SSM_CH), lambda s, i: (layer, 0, 0)),
            pl.BlockSpec((None, 1, SSM_CH), lambda s, i: (layer, 0, 0)),
        ],
        out_specs=[
            pl.BlockSpec((2, ts, LANES), lambda s, i: (0, s * n_tiles + i, 0)),
            pl.BlockSpec((None, 1, 2 * N_STATE), lambda s, i: (s, 0, 0)),
        ],
        out_shape=[jax.ShapeDtypeStruct((2, rows, LANES), F32),
                   jax.ShapeDtypeStruct((n_seq, 1, 2 * N_STATE), F32)],
        scratch_shapes=[
            pltpu.VMEM((n_chunks, CHUNK_COLS), F32),
            pltpu.VMEM((n_chunks, 2 * N_STATE), F32),
            pltpu.VMEM((n_chunks, 2 * N_STATE), F32),
            pltpu.VMEM((1, 2 * N_STATE), F32),
            pltpu.VMEM((CHUNK_COLS, CHUNK_COLS), BF16),
            pltpu.VMEM((CHUNK_COLS, 2 * N_STATE), BF16),
            pltpu.VMEM((2 * N_STATE, CHUNK_COLS), BF16),
        ],
        compiler_params=_params(2),
        name="ssm_prompt",
    )(u, k1, m2, n4, al, d3, gw, gb3)


def _mix_body(tm, q_ref, kv_ref, a_ref, gg_ref, sink_ref, cw_ref, cb_ref, lg_ref, lb_ref,
              o_ref, ctx_ref, kv_s, u_s, ush_s, bias_s):
    first_tile = pl.program_id(1) == 0
    pad = 32
    off = pad - (CONV_WIDTH - 1)
    sub = 8

    @pl.when(first_tile)
    def _():
        kv_s[0:WINDOW, :] = jnp.zeros((WINDOW, 2 * LANES), F32)
        u_s[0:pad, :] = jnp.zeros((pad, CONV_CH), F32)
        qi = lax.broadcasted_iota(jnp.int32, (WINDOW, 2 * WINDOW), 0)
        ki = lax.broadcasted_iota(jnp.int32, (WINDOW, 2 * WINDOW), 1)
        dist = qi - ki + WINDOW
        valid = (dist >= 0) & (dist < WINDOW)
        distf = dist.astype(F32)
        for g in range(N_KV_HEADS):
            for r in range(KV_REP):
                slope = 2.0 ** (-8.0 * (g * KV_REP + r + 1) / N_HEADS)
                bias_s[g, r * WINDOW:(r + 1) * WINDOW, :] = jnp.where(valid, -slope * distf, NEG)

    kv_s[WINDOW:, :] = kv_ref[...]
    u_s[pad:, :] = a_ref[...] * _sigmoid(gg_ref[...])
    for b in range(1, sub):
        ush_s[b - 1] = u_s[b:b + tm + pad - sub, :]

    for r0 in range(0, tm, CONV_ROWS):
        acc = jnp.zeros((CONV_ROWS, CONV_CH), F32) + cb_ref[...]
        for j in range(CONV_WIDTH):
            a0, b = divmod(off + j, sub)
            lo = r0 + a0 * sub
            rows = u_s[lo:lo + CONV_ROWS, :] if b == 0 else ush_s[b - 1, lo:lo + CONV_ROWS, :]
            acc = acc + cw_ref[j:j + 1, :] * rows
        mu = jnp.mean(acc, -1, keepdims=True)
        cen = acc - mu
        var = jnp.mean(cen * cen, -1, keepdims=True)
        yn = cen * lax.rsqrt(var + EPS) * lg_ref[...] + lb_ref[...]
        o_ref[r0:r0 + CONV_ROWS, ATTN_WIDTH:] = yn * _sigmoid(yn)
    ctx_ref[...] = u_s[tm + off:tm + pad, :]

    scale = 1.0 / math.sqrt(HEAD_DIM)
    k_lane_group = lax.broadcasted_iota(jnp.int32, (2 * WINDOW, LANES), 1) // HEAD_DIM
    o_lane_group = lax.broadcasted_iota(jnp.int32, (KV_REP * WINDOW, LANES), 1) // HEAD_DIM
    key_in_prev = lax.broadcasted_iota(jnp.int32, (1, 2 * WINDOW), 1) < WINDOW
    no_prev_block = jnp.logical_and(key_in_prev, first_tile)
    sinks = [jnp.concatenate([jnp.broadcast_to(sink_ref[:, g * KV_REP + r:g * KV_REP + r + 1], (WINDOW, 1))
                              for r in range(KV_REP)], 0) for g in range(N_KV_HEADS)]

    for blk in range(tm // WINDOW):
        r0 = blk * WINDOW
        kblk = kv_s[r0:r0 + 2 * WINDOW, 0:LANES]
        vblk = kv_s[r0:r0 + 2 * WINDOW, LANES:].astype(BF16)
        qs = jnp.concatenate([q_ref[r0:r0 + WINDOW, r * LANES:(r + 1) * LANES] for r in range(KV_REP)], 0)
        qs = (qs * scale).astype(BF16)
        o = None
        for g in range(N_KV_HEADS):
            kg = jnp.where(k_lane_group == g, kblk, 0.0).astype(BF16)
            s = lax.dot_general(qs, kg, (((1,), (1,)), ((), ())), preferred_element_type=F32) + bias_s[g]
            if blk == 0:
                s = jnp.where(no_prev_block, NEG, s)
            m = jnp.maximum(jnp.max(s, -1, keepdims=True), sinks[g])
            p = jnp.exp(s - m)
            denom = jnp.sum(p, -1, keepdims=True) + jnp.exp(sinks[g] - m)
            og = _dot(p.astype(BF16), vblk) / denom
            o = og if g == 0 else jnp.where(o_lane_group == 0, o, og)
        for r in range(KV_REP):
            o_ref[r0:r0 + WINDOW, r * LANES:(r + 1) * LANES] = o[r * WINDOW:(r + 1) * WINDOW, :]

    kv_s[0:WINDOW, :] = kv_s[tm:tm + WINDOW, :]
    u_s[0:pad, :] = u_s[tm:tm + pad, :]


def _mix_prompt(z, sinks3, cw, cb3, lg3, lb3, layer, n_seq, seq_len):
    tm = TM_ROWS
    n_tiles = seq_len // tm
    rows = n_seq * seq_len
    row = lambda s, i: s * n_tiles + i
    vec = lambda width: pl.BlockSpec((None, 1, width), lambda s, i: (layer, 0, 0))
    return pl.pallas_call(
        functools.partial(_mix_body, tm),
        grid=(n_seq, n_tiles),
        in_specs=[
            pl.BlockSpec((tm, ATTN_WIDTH), lambda s, i: (row(s, i), 0)),
            pl.BlockSpec((tm, 2 * LANES), lambda s, i: (row(s, i), Q_END // (2 * LANES))),
            pl.BlockSpec((tm, CONV_CH), lambda s, i: (row(s, i), V_END // CONV_CH)),
            pl.BlockSpec((tm, CONV_CH), lambda s, i: (row(s, i), V_END // CONV_CH + 1)),
            vec(N_HEADS),
            pl.BlockSpec((None, CONV_WIDTH, CONV_CH), lambda s, i: (layer, 0, 0)),
            vec(CONV_CH), vec(CONV_CH), vec(CONV_CH),
        ],
        out_specs=[
            pl.BlockSpec((tm, V_END), lambda s, i: (row(s, i), 0)),
            pl.BlockSpec((None, CONV_WIDTH - 1, CONV_CH), lambda s, i: (s, 0, 0)),
        ],
        out_shape=[jax.ShapeDtypeStruct((rows, V_END), F32),
                   jax.ShapeDtypeStruct((n_seq, CONV_WIDTH - 1, CONV_CH), F32)],
        scratch_shapes=[pltpu.VMEM((tm + WINDOW, 2 * LANES), F32),
                        pltpu.VMEM((tm + 32, CONV_CH), F32),
                        pltpu.VMEM((7, tm + 24, CONV_CH), F32),
                        pltpu.VMEM((N_KV_HEADS, KV_REP * WINDOW, 2 * WINDOW), F32)],
        compiler_params=_params(2),
        name="mix_prompt",
    )(z, z, z, z, sinks3, cw, cb3, lg3, lb3)


def _dec_body(q_ref, kn_ref, vn_ref, k2_ref, v2_ref, ck_ref, cv_ref, a_ref, gg_ref, cc_ref, u_ref, hr_ref, hi_ref,
              sink_ref, cw_ref, cb_ref, lg_ref, lb_ref, bm_ref, cm_ref, ab_ref, d_ref, gw_ref, gb_ref,
              o_ref, ok_ref, ov_ref, co_ref, oc_ref, so_ref, or_ref, oi_ref):
    nb = DEC_BLOCK
    win = ck_ref.shape[2]

    q3 = q_ref[...].reshape(nb, N_HEADS, LANES)
    kn = kn_ref[...]
    vn = vn_ref[...]
    ck = ck_ref[...]
    cv = cv_ref[...]
    s = jnp.einsum("nsc,ncj->nsj", q3, ck.astype(BF16), preferred_element_type=F32)
    si = lax.broadcasted_iota(jnp.int32, (N_HEADS, win), 0)
    ji = lax.broadcasted_iota(jnp.int32, (N_HEADS, win), 1)
    head = (si % 2) * KV_REP + si // 2
    slope = jnp.zeros((N_HEADS, win), F32)
    for h in range(N_HEADS):
        slope = jnp.where(head == h, 2.0 ** (-8.0 * (h + 1) / N_HEADS), slope)
    dist = win - ji
    bias = jnp.where(dist < WINDOW, -slope * dist.astype(F32), NEG)
    s = s + bias[None]
    s_new = jnp.sum(q3.astype(F32) * kn.astype(BF16).astype(F32), -1, keepdims=True)
    sink = sink_ref[...][None]
    m = jnp.maximum(jnp.maximum(jnp.max(s, -1, keepdims=True), s_new), sink)
    p = jnp.exp(s - m)
    p_new = jnp.exp(s_new - m)
    denom = jnp.sum(p, -1, keepdims=True) + p_new + jnp.exp(sink - m)
    o = jnp.einsum("nsj,ncj->nsc", p.astype(BF16), cv.astype(BF16), preferred_element_type=F32)
    o = o + p_new.astype(BF16).astype(F32) * vn.astype(BF16).astype(F32)
    o_ref[...] = (o / denom).reshape(nb * N_HEADS, LANES)

    last = lax.broadcasted_iota(jnp.int32, (LANES, win), 1) == win - 1
    fill = jnp.zeros((LANES - nb, LANES), F32)
    knt = jnp.concatenate([k2_ref[...], fill], 0).T
    vnt = jnp.concatenate([v2_ref[...], fill], 0).T
    for i in range(nb):
        ok_ref[i] = jnp.where(last, pltpu.roll(knt, win - 1 - i, 1), pltpu.roll(ck[i], win - 1, 1))
        ov_ref[i] = jnp.where(last, pltpu.roll(vnt, win - 1 - i, 1), pltpu.roll(cv[i], win - 1, 1))

    u = a_ref[...] * _sigmoid(gg_ref[...])
    acc = cb_ref[...] + cw_ref[CONV_WIDTH - 1:CONV_WIDTH, :] * u
    for j in range(CONV_WIDTH - 1):
        acc = acc + cw_ref[j:j + 1, :] * cc_ref[j]
    mu = jnp.mean(acc, -1, keepdims=True)
    cen = acc - mu
    var = jnp.mean(cen * cen, -1, keepdims=True)
    yn = cen * lax.rsqrt(var + EPS) * lg_ref[...] + lb_ref[...]
    co_ref[...] = yn * _sigmoid(yn)
    for j in range(CONV_WIDTH - 2):
        oc_ref[j] = cc_ref[j + 1]
    oc_ref[CONV_WIDTH - 2] = u

    us = jnp.concatenate([u_ref[0], u_ref[1]], -1)
    bu = _dot(us.astype(BF16), bm_ref[...])
    a_re = ab_ref[:, :N_STATE]
    a_im = ab_ref[:, N_STATE:]
    h_re = hr_ref[...]
    h_im = hi_ref[...]
    n_re = a_re * h_re - a_im * h_im + bu[:, :N_STATE]
    n_im = a_re * h_im + a_im * h_re + bu[:, N_STATE:]
    or_ref[...] = n_re
    oi_ref[...] = n_im
    hcat = jnp.concatenate([n_re, n_im], -1).astype(BF16)
    y = _gelu_tanh(_dot(hcat, cm_ref[...]) + d_ref[...] * us)
    gate = _dot(y.astype(BF16), gw_ref[...]) + gb_ref[...]
    so_ref[...] = y * _sigmoid(gate)


def _dec_mix(q3, kn3, vn3, ck, cv, z, cct, u, hr, hi, sinks3, cw, cb3, lg3, lb3,
             bmat, cmat, ab3, d3, gw, gb3, layer):
    n = kn3.shape[0]
    win = ck.shape[3]
    nb = DEC_BLOCK
    vec = lambda width: pl.BlockSpec((None, 1, width), lambda i: (layer, 0, 0))
    mat = lambda r, c: pl.BlockSpec((None, r, c), lambda i: (layer, 0, 0))
    cache = pl.BlockSpec((None, nb, LANES, win), lambda i: (layer, i, 0, 0))
    ctx = pl.BlockSpec((None, CONV_WIDTH - 1, nb, CONV_CH), lambda i: (layer, 0, i, 0))
    state = pl.BlockSpec((None, nb, N_STATE), lambda i: (layer, i, 0))
    rowblk = lambda width: pl.BlockSpec((nb, width), lambda i: (i, 0))
    new3 = pl.BlockSpec((nb, 1, LANES), lambda i: (i, 0, 0))
    return pl.pallas_call(
        _dec_body,
        grid=(n // nb,),
        in_specs=[
            pl.BlockSpec((nb * N_HEADS, LANES), lambda i: (i, 0)), new3, new3,
            pl.BlockSpec((nb, LANES), lambda i: (i, Q_END // LANES)),
            pl.BlockSpec((nb, LANES), lambda i: (i, K_END // LANES)),
            cache, cache,
            pl.BlockSpec((nb, CONV_CH), lambda i: (i, V_END // CONV_CH)),
            pl.BlockSpec((nb, CONV_CH), lambda i: (i, V_END // CONV_CH + 1)),
            ctx,
            pl.BlockSpec((2, nb, LANES), lambda i: (0, i, 0)),
            state, state,
            mat(N_HEADS, 1), mat(CONV_WIDTH, CONV_CH), vec(CONV_CH), vec(CONV_CH), vec(CONV_CH),
            mat(SSM_CH, 2 * N_STATE), mat(2 * N_STATE, SSM_CH), vec(2 * N_STATE), vec(SSM_CH),
            mat(SSM_CH, SSM_CH), vec(SSM_CH),
        ],
        out_specs=[
            pl.BlockSpec((nb * N_HEADS, LANES), lambda i: (i, 0)),
            pl.BlockSpec((nb, LANES, win), lambda i: (i, 0, 0)),
            pl.BlockSpec((nb, LANES, win), lambda i: (i, 0, 0)),
            rowblk(CONV_CH),
            pl.BlockSpec((CONV_WIDTH - 1, nb, CONV_CH), lambda i: (0, i, 0)),
            rowblk(SSM_CH), rowblk(N_STATE), rowblk(N_STATE),
        ],
        out_shape=[
            jax.ShapeDtypeStruct((n * N_HEADS, LANES), F32),
            jax.ShapeDtypeStruct((n, LANES, win), F32),
            jax.ShapeDtypeStruct((n, LANES, win), F32),
            jax.ShapeDtypeStruct((n, CONV_CH), F32),
            jax.ShapeDtypeStruct((CONV_WIDTH - 1, n, CONV_CH), F32),
            jax.ShapeDtypeStruct((n, SSM_CH), F32),
            jax.ShapeDtypeStruct((n, N_STATE), F32),
            jax.ShapeDtypeStruct((n, N_STATE), F32),
        ],
        compiler_params=_params(1),
        name="dec_mix",
    )(q3, kn3, vn3, z, z, ck, cv, z, z, cct, u, hr, hi, sinks3, cw, cb3, lg3, lb3,
      bmat, cmat, ab3, d3, gw, gb3)


def _ssm_operands(a_re, a_im, log_dt, b_re, b_im, c_re, c_im):
    hi = lax.Precision.HIGHEST
    l_n, g_n, p_n, c_n = SSM_CHUNK, SSM_GROUPS, SSM_STATE, SSM_GROUP
    dt = jnp.exp(log_dt)[:, None]
    mag = jnp.exp(a_re * dt)
    ab_re, ab_im = mag * jnp.cos(a_im * dt), mag * jnp.sin(a_im * dt)
    den = a_re * a_re + a_im * a_im
    q_re = ((ab_re - 1.0) * a_re + ab_im * a_im) / den
    q_im = (ab_im * a_re - (ab_re - 1.0) * a_im) / den
    bt_re, bt_im = jnp.transpose(b_re, (2, 0, 1)), jnp.transpose(b_im, (2, 0, 1))
    bb_re = q_re * bt_re - q_im * bt_im
    bb_im = q_re * bt_im + q_im * bt_re
    pw_re, pw_im = [jnp.ones_like(ab_re)], [jnp.zeros_like(ab_re)]
    for _ in range(l_n):
        r, i = pw_re[-1], pw_im[-1]
        pw_re.append(r * ab_re - i * ab_im)
        pw_im.append(r * ab_im + i * ab_re)
    pb_re = [pw_re[t] * bb_re - pw_im[t] * bb_im for t in range(l_n)]
    pb_im = [pw_re[t] * bb_im + pw_im[t] * bb_re for t in range(l_n)]

    ktau = (jnp.einsum("gop,tcgp->tcgo", c_re, jnp.stack(pb_re), precision=hi)
            - jnp.einsum("gop,tcgp->tcgo", c_im, jnp.stack(pb_im), precision=hi)).reshape(l_n, c_n, SSM_CH)
    zero = jnp.zeros((c_n, SSM_CH), F32)
    k1 = jnp.concatenate([jnp.concatenate([ktau[lo - li] if lo >= li else zero for lo in range(l_n)], 1)
                          for li in range(l_n)], 0)
    m2 = jnp.concatenate([jnp.concatenate([pb_re[l_n - 1 - l].reshape(c_n, N_STATE),
                                           pb_im[l_n - 1 - l].reshape(c_n, N_STATE)], 1)
                          for l in range(l_n)], 0)
    ct_re, ct_im = jnp.transpose(c_re, (0, 2, 1)), jnp.transpose(c_im, (0, 2, 1))
    n4_re = jnp.stack([ct_re * pw_re[l + 1][:, :, None] - ct_im * pw_im[l + 1][:, :, None]
                       for l in range(l_n)], 2).reshape(N_STATE, l_n * c_n)
    n4_im = jnp.stack([ct_re * pw_im[l + 1][:, :, None] + ct_im * pw_re[l + 1][:, :, None]
                       for l in range(l_n)], 2).reshape(N_STATE, l_n * c_n)
    n4 = jnp.concatenate([n4_re, -n4_im], 0)

    chan_group = jnp.arange(SSM_CH) // c_n
    state_group = (jnp.arange(2 * N_STATE) % N_STATE) // p_n
    bmat = jnp.where(chan_group[:, None] == state_group[None, :],
                     jnp.tile(m2[(l_n - 1) * c_n:], (g_n, 1)), 0.0)
    cc = jnp.concatenate([ct_re.reshape(N_STATE, c_n), -ct_im.reshape(N_STATE, c_n)], 0)
    cmat = jnp.where(state_group[:, None] == chan_group[None, :], jnp.tile(cc, (1, g_n)), 0.0)

    flat = lambda re, im: jnp.concatenate([re.reshape(1, N_STATE), im.reshape(1, N_STATE)], -1)
    return (k1, m2, n4, bmat.astype(BF16), cmat.astype(BF16),
            flat(pw_re[l_n], pw_im[l_n]), flat(ab_re, ab_im))


def _decode_head_order():
    s = jnp.arange(N_HEADS)
    return (s % 2) * KV_REP + s // 2


def kernel(x_prompt, x_sample, cache_swa_k, cache_swa_v, cache_conv, state_ssm_re, state_ssm_im,
           norm_mix_g, w_in, attn_sinks, conv_dw_w, conv_dw_b, conv_ln_g, conv_ln_b,
           ssm_a_re, ssm_a_im, ssm_log_dt, ssm_b_re, ssm_b_im, ssm_c_re, ssm_c_im,
           ssm_d, ssm_glu_w, ssm_glu_b, w_out, norm_ffn_g, w_ff_gate, w_ff_up, w_ff_down,
           norm_final_g):
    n_seq, seq_len, _ = x_prompt.shape
    n_dec = x_sample.shape[0]
    win = cache_swa_k.shape[2]
    assert x_sample.shape[1] == 1 and win == WINDOW
    assert seq_len % TS_ROWS == 0 and n_dec % DEC_BLOCK == 0

    row3 = lambda v: v.reshape(DEPTH, 1, -1)
    g_mix, g_ffn = row3(norm_mix_g), row3(norm_ffn_g)
    sinks3 = row3(attn_sinks)
    order = _decode_head_order()
    sinks_dec = attn_sinks[:, order][:, :, None]
    cb3, lg3, lb3 = row3(conv_dw_b), row3(conv_ln_g), row3(conv_ln_b)
    d3, gb3 = row3(ssm_d), row3(ssm_glu_b)
    w_in_q = jnp.transpose(w_in[:, :, :Q_END].astype(BF16).reshape(DEPTH, D_MODEL, N_KV_HEADS, KV_REP, HEAD_DIM),
                           (0, 1, 3, 2, 4)).reshape(DEPTH, D_MODEL, Q_END)
    w_in_rest = w_in[:, :, Q_END:].astype(BF16)
    w_out_b = w_out.astype(BF16)
    wg_b, wu_b, wd_b = w_ff_gate.astype(BF16), w_ff_up.astype(BF16), w_ff_down.astype(BF16)
    gw_b = ssm_glu_w.astype(BF16)
    k1, m2, n4, bmat, cmat, a_chunk, a_step = jax.vmap(_ssm_operands)(
        ssm_a_re, ssm_a_im, ssm_log_dt, ssm_b_re, ssm_b_im, ssm_c_re, ssm_c_im)

    wo_heads = w_out_b[:, :ATTN_WIDTH].reshape(DEPTH, N_HEADS, HEAD_DIM, D_MODEL)[:, order]
    own = (jnp.arange(N_HEADS)[:, None] % 2) == jnp.arange(N_KV_HEADS)[None, :]
    wo_dec = jnp.where(own[None, :, :, None, None], wo_heads[:, :, None], 0).reshape(
        DEPTH, N_HEADS * LANES, D_MODEL)
    wo_attn = jnp.transpose(w_out_b[:, :ATTN_WIDTH].reshape(DEPTH, N_KV_HEADS, KV_REP, HEAD_DIM, D_MODEL),
                            (0, 2, 1, 3, 4)).reshape(DEPTH, ATTN_WIDTH, D_MODEL)
    wo_attn_conv = jnp.concatenate([wo_attn, w_out_b[:, ATTN_WIDTH:V_END]], 1)
    wo_conv = w_out_b[:, ATTN_WIDTH:V_END]
    wo_ssm = w_out_b[:, V_END:]

    ck = jnp.transpose(cache_swa_k, (0, 1, 3, 4, 2)).reshape(DEPTH, n_dec, LANES, win)
    cv = jnp.transpose(cache_swa_v, (0, 1, 3, 4, 2)).reshape(DEPTH, n_dec, LANES, win)
    cct = jnp.transpose(cache_conv, (0, 2, 1, 3))
    hr = state_ssm_re.reshape(DEPTH, n_dec, N_STATE)
    hi = state_ssm_im.reshape(DEPTH, n_dec, N_STATE)

    xp = x_prompt.reshape(n_seq * seq_len, D_MODEL)
    xs = x_sample.reshape(n_dec, D_MODEL)
    tm = TM_ROWS
    rowspec = lambda width: pl.BlockSpec((tm, width), lambda i: (i, 0))
    decspec = lambda width: pl.BlockSpec((n_dec, width), lambda i: (i, 0))
    scale = 1.0 / math.sqrt(HEAD_DIM)
    own_lane = ((jnp.arange(LANES) // HEAD_DIM)[None, None, None, :]
                == jnp.arange(N_KV_HEADS)[None, None, :, None])

    cache_out = lambda layers: jnp.transpose(
        jnp.stack(layers).reshape(DEPTH, n_dec, N_KV_HEADS, HEAD_DIM, win), (0, 1, 4, 2, 3))
    kp, vp, cp, hrp, hip = [], [], [], [], []
    ks, vs, cs, hrs, his = [], [], [], [], []
    for l in range(DEPTH):
        final = norm_final_g.reshape(1, D_MODEL) if l == DEPTH - 1 else None

        z, u = _inproj(xp, g_mix, w_in_q, w_in_rest, l, tm)
        ssm, h_last = _ssm_prompt(u, k1, m2, n4, a_chunk, d3, gw_b, gb3, l, n_seq, seq_len)
        mix, ctx = _mix_prompt(z, sinks3, conv_dw_w, cb3, lg3, lb3, l, n_seq, seq_len)
        xp = _tail(xp, g_ffn,
                   [(mix, rowspec(V_END), wo_attn_conv),
                    (ssm, pl.BlockSpec((2, tm, LANES), lambda i: (0, i, 0)), wo_ssm)],
                   wg_b, wu_b, wd_b, l, tm, final)
        z3 = z.reshape(n_seq, seq_len, C_END)[:, seq_len - WINDOW:]
        kp.append(z3[..., Q_END:K_END].reshape(n_seq, WINDOW, N_KV_HEADS, HEAD_DIM))
        vp.append(z3[..., K_END:V_END].reshape(n_seq, WINDOW, N_KV_HEADS, HEAD_DIM))
        cp.append(ctx)
        hrp.append(h_last[:, 0, :N_STATE].reshape(n_seq, SSM_GROUPS, SSM_STATE))
        hip.append(h_last[:, 0, N_STATE:].reshape(n_seq, SSM_GROUPS, SSM_STATE))

        zs, us = _inproj(xs, g_mix, w_in_q, w_in_rest, l, n_dec)
        zq = (zs[:, :Q_END] * scale).reshape(n_dec, KV_REP, 1, LANES)
        q3 = jnp.where(own_lane, zq, 0.0).astype(BF16).reshape(n_dec * N_HEADS, LANES)
        kn3 = zs[:, Q_END:K_END].reshape(n_dec, 1, LANES)
        vn3 = zs[:, K_END:V_END].reshape(n_dec, 1, LANES)
        o3, nk, nv, conv_s, nct, ssm_s, nhr, nhi = _dec_mix(
            q3, kn3, vn3, ck, cv, zs, cct, us, hr, hi, sinks_dec, conv_dw_w, cb3, lg3, lb3,
            bmat, cmat, a_step, d3, gw_b, gb3, l)
        xs = _tail(xs, g_ffn,
                   [(o3.reshape(n_dec, N_HEADS * LANES), decspec(N_HEADS * LANES), wo_dec),
                    (conv_s, decspec(CONV_CH), wo_conv),
                    (ssm_s, decspec(SSM_CH), wo_ssm)],
                   wg_b, wu_b, wd_b, l, n_dec, final)
        ks.append(nk)
        vs.append(nv)
        cs.append(jnp.transpose(nct, (1, 0, 2)))
        hrs.append(nhr.reshape(n_dec, SSM_GROUPS, SSM_STATE))
        his.append(nhi.reshape(n_dec, SSM_GROUPS, SSM_STATE))

    return (xp.reshape(n_seq, seq_len, D_MODEL), xs.reshape(n_dec, 1, D_MODEL),
            jnp.stack(kp), jnp.stack(vp), jnp.stack(cp), jnp.stack(hrp), jnp.stack(hip),
            cache_out(ks), cache_out(vs), jnp.stack(cs), jnp.stack(hrs), jnp.stack(his))
```

```python
import functools
import math

import jax
import jax.numpy as jnp
from jax import lax
from jax.experimental import pallas as pl
from jax.experimental.pallas import tpu as pltpu

D_MODEL = 1024
DEPTH = 4
HEAD_DIM = 64
ATTN_WIDTH = 512
N_HEADS = 8
N_KV_HEADS = 2
KV_REP = 4
WINDOW = 128
CONV_CH = 256
CONV_WIDTH = 31
SSM_CH = 256
SSM_GROUP = 16
SSM_GROUPS = 16
SSM_STATE = 64
D_FF = 2816
EPS = 1e-6

Q_END = ATTN_WIDTH
K_END = Q_END + N_KV_HEADS * HEAD_DIM
V_END = K_END + N_KV_HEADS * HEAD_DIM
C_END = V_END + 2 * CONV_CH
IN_COLS = C_END + SSM_CH

N_STATE = SSM_GROUPS * SSM_STATE
LANES = 128
SSM_CHUNK = 8
CHUNK_COLS = SSM_CHUNK * SSM_CH
NEG = -1e30

TM_ROWS = 512
TS_ROWS = 4096
FF_CHUNK = 256
OUT_CHUNK = 1024
DEC_BLOCK = 16
CONV_ROWS = 64
ATTN_BLOCKS = 4
VMEM_LIMIT = 56 * 1024 * 1024

F32 = jnp.float32
BF16 = jnp.bfloat16


def _params(n_axes, flags=None):
    return pltpu.CompilerParams(dimension_semantics=("arbitrary",) * n_axes,
                                vmem_limit_bytes=VMEM_LIMIT, flags=flags)


def _resident(shape, index_map):
    return pl.BlockSpec(shape, index_map, pipeline_mode=pl.Buffered(1))


def _rms(x, g):
    return x * lax.rsqrt(jnp.mean(x * x, -1, keepdims=True) + EPS) * g


def _sigmoid(x):
    return 1.0 / (1.0 + jnp.exp(-x))


def _gelu_tanh(x):
    c = math.sqrt(2.0 / math.pi)
    return 0.5 * x * (1.0 + jnp.tanh(c * (x + 0.044715 * (x * x * x))))


def _dot(a, b):
    return jnp.dot(a, b, preferred_element_type=F32)


def _inproj_body(x_ref, g_ref, wq_ref, wr_ref, z_ref, u_ref):
    h = _rms(x_ref[...], g_ref[...]).astype(BF16)
    z_ref[:, :Q_END] = _dot(h, wq_ref[...])
    zr = _dot(h, wr_ref[...])
    z_ref[:, Q_END:] = zr[:, :C_END - Q_END]
    u_ref[0] = zr[:, C_END - Q_END:C_END - Q_END + LANES]
    u_ref[1] = zr[:, C_END - Q_END + LANES:]


def _inproj(x, g3, w_q, w_rest, layer, tm):
    rows = x.shape[0]
    return pl.pallas_call(
        _inproj_body,
        grid=(rows // tm,),
        in_specs=[
            pl.BlockSpec((tm, D_MODEL), lambda i: (i, 0)),
            pl.BlockSpec((None, 1, D_MODEL), lambda i: (layer, 0, 0)),
            _resident((None, D_MODEL, Q_END), lambda i: (layer, 0, 0)),
            _resident((None, D_MODEL, IN_COLS - Q_END), lambda i: (layer, 0, 0)),
        ],
        out_specs=[
            pl.BlockSpec((tm, C_END), lambda i: (i, 0)),
            pl.BlockSpec((2, tm, LANES), lambda i: (0, i, 0)),
        ],
        out_shape=[jax.ShapeDtypeStruct((rows, C_END), F32),
                   jax.ShapeDtypeStruct((2, rows, LANES), F32)],
        compiler_params=_params(1),
        name="inproj",
    )(x, g3, w_q, w_rest)


def _tail_steps(x, acts, g_ref, wg_ref, wu_ref, wd_ref, gf_ref, o_ref, x1_s, hf_s, act_s):
    x1 = x
    for act, w_ref in acts:
        x1 = x1 + _dot(act, w_ref[...])
    hf_s[...] = _rms(x1, g_ref[...]).astype(BF16)
    x1_s[...] = x1
    yield
    for c in range(0, D_FF, FF_CHUNK):
        gate = _dot(hf_s[...], wg_ref[:, c:c + FF_CHUNK])
        up = _dot(hf_s[...], wu_ref[:, c:c + FF_CHUNK])
        act_s[:, c:c + FF_CHUNK] = (gate * _sigmoid(gate) * up).astype(BF16)
        yield
    for n in range(0, D_MODEL, OUT_CHUNK):
        o_ref[:, n:n + OUT_CHUNK] = x1_s[:, n:n + OUT_CHUNK] + _dot(act_s[...], wd_ref[:, n:n + OUT_CHUNK])
        yield
    if gf_ref is not None:
        o_ref[...] = _rms(o_ref[...], gf_ref[...])


def _tail_body(n_parts, final, *refs):
    x_ref, g_ref = refs[0], refs[1]
    parts = refs[2:2 + 2 * n_parts]
    wg_ref, wu_ref, wd_ref = refs[2 + 2 * n_parts:5 + 2 * n_parts]
    gf_ref = refs[5 + 2 * n_parts] if final else None
    o_ref, x1_s, hf_s, act_s = refs[-4:]
    acts = []
    for p in range(n_parts):
        act_ref = parts[2 * p]
        if len(act_ref.shape) == 3:
            act = jnp.concatenate([act_ref[i] for i in range(act_ref.shape[0])], -1)
        else:
            act = act_ref[...]
        acts.append((act.astype(BF16), parts[2 * p + 1]))
    for _ in _tail_steps(x_ref[...], acts, g_ref, wg_ref, wu_ref, wd_ref, gf_ref, o_ref, x1_s, hf_s, act_s):
        pass


def _tail(x, g3, parts, wg, wu, wd, layer, tm, final_g=None):
    rows = x.shape[0]
    final = final_g is not None
    in_specs = [pl.BlockSpec((tm, D_MODEL), lambda i: (i, 0)),
                pl.BlockSpec((None, 1, D_MODEL), lambda i: (layer, 0, 0))]
    args = [x, g3]
    for act, spec, w in parts:
        in_specs.append(spec)
        in_specs.append(pl.BlockSpec((None,) + w.shape[1:], lambda i: (layer, 0, 0)))
        args += [act, w]
    in_specs += [
        _resident((None, D_MODEL, D_FF), lambda i: (layer, 0, 0)),
        _resident((None, D_MODEL, D_FF), lambda i: (layer, 0, 0)),
        _resident((None, D_FF, D_MODEL), lambda i: (layer, 0, 0)),
    ]
    args += [wg, wu, wd]
    if final:
        in_specs.append(pl.BlockSpec((1, D_MODEL), lambda i: (0, 0)))
        args.append(final_g)
    return pl.pallas_call(
        functools.partial(_tail_body, len(parts), final),
        grid=(rows // tm,),
        in_specs=in_specs,
        out_specs=pl.BlockSpec((tm, D_MODEL), lambda i: (i, 0)),
        out_shape=jax.ShapeDtypeStruct((rows, D_MODEL), F32),
        scratch_shapes=[pltpu.VMEM((tm, D_MODEL), F32), pltpu.VMEM((tm, D_MODEL), BF16),
                        pltpu.VMEM((tm, D_FF), BF16)],
        compiler_params=_params(1),
        name="tail",
    )(*args)


def _expand_ssm_operands(k1_ref, m2_ref, n4_ref, w1_ref, w2_ref, w4_ref):
    col = lax.broadcasted_iota(jnp.int32, (SSM_GROUP, CHUNK_COLS), 1)
    col_chan_group = (col % SSM_CH) // SSM_GROUP
    col_state_group = (col % N_STATE) // SSM_STATE
    for l in range(SSM_CHUNK):
        k1 = k1_ref[l * SSM_GROUP:(l + 1) * SSM_GROUP, :]
        m2 = m2_ref[l * SSM_GROUP:(l + 1) * SSM_GROUP, :]
        for g in range(SSM_GROUPS):
            r0 = l * SSM_CH + g * SSM_GROUP
            w1_ref[r0:r0 + SSM_GROUP, :] = jnp.where(col_chan_group == g, k1, 0.0).astype(BF16)
            w2_ref[r0:r0 + SSM_GROUP, :] = jnp.where(col_state_group == g, m2, 0.0).astype(BF16)
    e_row = lax.broadcasted_iota(jnp.int32, (SSM_CHUNK * SSM_GROUP, CHUNK_COLS), 0)
    e_col = lax.broadcasted_iota(jnp.int32, (SSM_CHUNK * SSM_GROUP, CHUNK_COLS), 1)
    spread = ((e_row // SSM_GROUP == e_col // SSM_CH) & (e_row % SSM_GROUP == e_col % SSM_GROUP))
    spread = jnp.where(spread, 1.0, 0.0).astype(BF16)
    rows = lax.broadcasted_iota(jnp.int32, (LANES, CHUNK_COLS), 0)
    cols = lax.broadcasted_iota(jnp.int32, (LANES, CHUNK_COLS), 1)
    for b in range(2 * N_STATE // LANES):
        full = _dot(n4_ref[b * LANES:(b + 1) * LANES, :].astype(BF16), spread)
        row_group = ((rows + b * LANES) % N_STATE) // SSM_STATE
        keep = row_group == (cols % SSM_CH) // SSM_GROUP
        w4_ref[b * LANES:(b + 1) * LANES, :] = jnp.where(keep, full, 0.0).astype(BF16)


def _ssm_body(n_chunks, u_ref, k1_ref, m2_ref, n4_ref, al_ref, d_ref, gw_ref, gb_ref,
              o_ref, hl_ref, x_s, gh_s, hc_s, w1_ref, w2_ref, w4_ref):
    @pl.when((pl.program_id(0) == 0) & (pl.program_id(1) == 0))
    def _():
        _expand_ssm_operands(k1_ref, m2_ref, n4_ref, w1_ref, w2_ref, w4_ref)

    @pl.when(pl.program_id(1) == 0)
    def _():
        hc_s[...] = jnp.zeros_like(hc_s)

    def steps(l, s):
        return u_ref[s, pl.ds(l, n_chunks, stride=SSM_CHUNK), :]

    for l in range(SSM_CHUNK):
        for s in range(2):
            c0 = l * SSM_CH + s * LANES
            x_s[:, c0:c0 + LANES] = steps(l, s).astype(BF16)

    gh_s[...] = _dot(x_s[...], w2_ref[...])

    a_re = al_ref[:, :N_STATE]
    a_im = al_ref[:, N_STATE:]

    def step(k, carry):
        h_re, h_im = carry
        g_re = gh_s[pl.ds(k, 1), :N_STATE]
        g_im = gh_s[pl.ds(k, 1), N_STATE:]
        gh_s[pl.ds(k, 1), :N_STATE] = h_re
        gh_s[pl.ds(k, 1), N_STATE:] = h_im
        return (a_re * h_re - a_im * h_im + g_re, a_re * h_im + a_im * h_re + g_im)

    h_re, h_im = lax.fori_loop(0, n_chunks, step, (hc_s[:, :N_STATE], hc_s[:, N_STATE:]))
    hc_s[:, :N_STATE] = h_re
    hc_s[:, N_STATE:] = h_im
    hl_ref[:, :N_STATE] = h_re
    hl_ref[:, N_STATE:] = h_im

    hb = gh_s[...].astype(BF16)
    for l in range(SSM_CHUNK):
        c0, c1 = l * SSM_CH, (l + 1) * SSM_CH
        y = _dot(x_s[:, :c1], w1_ref[:c1, c0:c1]) + _dot(hb, w4_ref[:, c0:c1])
        y = _gelu_tanh(y + d_ref[...] * jnp.concatenate([steps(l, 0), steps(l, 1)], -1))
        gate = _dot(y.astype(BF16), gw_ref[...]) + gb_ref[...]
        out = y * _sigmoid(gate)
        for s in range(2):
            o_ref[s, pl.ds(l, n_chunks, stride=SSM_CHUNK), :] = out[:, s * LANES:(s + 1) * LANES]


def _ssm_prompt(u, k1, m2, n4, al, d3, gw, gb3, layer, n_seq, seq_len):
    ts = TS_ROWS
    n_tiles = seq_len // ts
    n_chunks = ts // SSM_CHUNK
    rows = u.shape[1]
    compact = SSM_CHUNK * SSM_GROUP
    return pl.pallas_call(
        functools.partial(_ssm_body, n_chunks),
        grid=(n_seq, n_tiles),
        in_specs=[
            _resident((2, ts, LANES), lambda s, i: (0, s * n_tiles + i, 0)),
            pl.BlockSpec((None, compact, CHUNK_COLS), lambda s, i: (layer, 0, 0)),
            pl.BlockSpec((None, compact, 2 * N_STATE), lambda s, i: (layer, 0, 0)),
            pl.BlockSpec((None, 2 * N_STATE, compact), lambda s, i: (layer, 0, 0)),
            pl.BlockSpec((None, 1, 2 * N_STATE), lambda s, i: (layer, 0, 0)),
            pl.BlockSpec((None, 1, SSM_CH), lambda s, i: (layer, 0, 0)),
            pl.BlockSpec((None, SSM_CH, SSM_CH), lambda s, i: (layer, 0, 0)),
            pl.BlockSpec((None, 1, SSM_CH), lambda s, i: (layer, 0, 0)),
        ],
        out_specs=[
            _resident((2, ts, LANES), lambda s, i: (0, s * n_tiles + i, 0)),
            pl.BlockSpec((None, 1, 2 * N_STATE), lambda s, i: (s, 0, 0)),
        ],
        out_shape=[jax.ShapeDtypeStruct((2, rows, LANES), F32),
                   jax.ShapeDtypeStruct((n_seq, 1, 2 * N_STATE), F32)],
        scratch_shapes=[
            pltpu.VMEM((n_chunks, CHUNK_COLS), BF16),
            pltpu.VMEM((n_chunks, 2 * N_STATE), F32),
            pltpu.VMEM((1, 2 * N_STATE), F32),
            pltpu.VMEM((CHUNK_COLS, CHUNK_COLS), BF16),
            pltpu.VMEM((CHUNK_COLS, 2 * N_STATE), BF16),
            pltpu.VMEM((2 * N_STATE, CHUNK_COLS), BF16),
        ],
        compiler_params=_params(2),
        name="ssm_prompt",
    )(u, k1, m2, n4, al, d3, gw, gb3)


def _mix_steps(tm, first_tile, q_ref, kv_ref, a_ref, gg_ref, sink_ref, cw_ref, cb_ref, lg_ref, lb_ref,
                 o_ref, ctx_ref, kv_s, u_s, ush_s, bias_s):
    pad = 32
    off = pad - (CONV_WIDTH - 1)
    sub = 8

    @pl.when(first_tile)
    def _():
        kv_s[0:WINDOW, :] = jnp.zeros((WINDOW, 2 * LANES), F32)
        u_s[0:pad, :] = jnp.zeros((pad, CONV_CH), F32)
        qi = lax.broadcasted_iota(jnp.int32, (WINDOW, 2 * WINDOW), 0)
        ki = lax.broadcasted_iota(jnp.int32, (WINDOW, 2 * WINDOW), 1)
        dist = qi - ki + WINDOW
        valid = (dist >= 0) & (dist < WINDOW)
        distf = dist.astype(F32)
        for g in range(N_KV_HEADS):
            for r in range(KV_REP):
                slope = 2.0 ** (-8.0 * (g * KV_REP + r + 1) / N_HEADS)
                bias_s[g, r * WINDOW:(r + 1) * WINDOW, :] = jnp.where(valid, -slope * distf, NEG)

    kv_s[WINDOW:, :] = kv_ref[...]
    u_s[pad:, :] = a_ref[...] * _sigmoid(gg_ref[...])
    for b in range(1, sub):
        ush_s[b - 1] = u_s[b:b + tm + pad - sub, :]
    ctx_ref[...] = u_s[tm + off:tm + pad, :]
    yield

    def conv_units():
        for r0 in range(0, tm, CONV_ROWS):
            acc = jnp.zeros((CONV_ROWS, CONV_CH), F32) + cb_ref[...]
            for j in range(CONV_WIDTH):
                a0, b = divmod(off + j, sub)
                lo = r0 + a0 * sub
                rows = u_s[lo:lo + CONV_ROWS, :] if b == 0 else ush_s[b - 1, lo:lo + CONV_ROWS, :]
                acc = acc + cw_ref[j:j + 1, :] * rows
            mu = jnp.mean(acc, -1, keepdims=True)
            cen = acc - mu
            var = jnp.mean(cen * cen, -1, keepdims=True)
            yn = cen * lax.rsqrt(var + EPS) * lg_ref[...] + lb_ref[...]
            o_ref[r0:r0 + CONV_ROWS, ATTN_WIDTH:] = (yn * _sigmoid(yn)).astype(BF16)
            yield

    scale = 1.0 / math.sqrt(HEAD_DIM)
    k_lane_group = lax.broadcasted_iota(jnp.int32, (2 * WINDOW, LANES), 1) // HEAD_DIM
    o_lane_group = lax.broadcasted_iota(jnp.int32, (KV_REP * WINDOW, LANES), 1) // HEAD_DIM
    key_in_prev = lax.broadcasted_iota(jnp.int32, (1, 2 * WINDOW), 1) < WINDOW
    no_prev_block = jnp.logical_and(key_in_prev, first_tile)
    sinks = [jnp.concatenate([jnp.broadcast_to(sink_ref[:, g * KV_REP + r:g * KV_REP + r + 1], (WINDOW, 1))
                              for r in range(KV_REP)], 0) for g in range(N_KV_HEADS)]

    def attention_units():
        n_blocks = tm // WINDOW
        for b0 in range(0, n_blocks, ATTN_BLOCKS):
            blocks = range(b0, min(b0 + ATTN_BLOCKS, n_blocks))
            pairs = [(blk, g) for blk in blocks for g in range(N_KV_HEADS)]
            kblk = {blk: kv_s[blk * WINDOW:(blk + 2) * WINDOW, 0:LANES] for blk in blocks}
            vblk = {blk: kv_s[blk * WINDOW:(blk + 2) * WINDOW, LANES:].astype(BF16) for blk in blocks}
            qs = {blk: (jnp.concatenate([q_ref[blk * WINDOW:(blk + 1) * WINDOW, r * LANES:(r + 1) * LANES]
                                         for r in range(KV_REP)], 0) * scale).astype(BF16) for blk in blocks}
            kg = {(blk, g): jnp.where(k_lane_group == g, kblk[blk], 0.0).astype(BF16) for blk, g in pairs}
            s = {(blk, g): lax.dot_general(qs[blk], kg[blk, g], (((1,), (1,)), ((), ())),
                                           preferred_element_type=F32) + bias_s[g] for blk, g in pairs}
            for g in range(N_KV_HEADS):
                if b0 == 0:
                    s[0, g] = jnp.where(no_prev_block, NEG, s[0, g])
            m = {k: jnp.maximum(jnp.max(s[k], -1, keepdims=True), sinks[k[1]]) for k in pairs}
            p = {k: jnp.exp(s[k] - m[k]) for k in pairs}
            denom = {k: jnp.sum(p[k], -1, keepdims=True) + jnp.exp(sinks[k[1]] - m[k]) for k in pairs}
            og = {k: _dot(p[k].astype(BF16), vblk[k[0]]) / denom[k] for k in pairs}
            for blk in blocks:
                o = jnp.where(o_lane_group == 0, og[blk, 0], og[blk, 1])
                for r in range(KV_REP):
                    o_ref[blk * WINDOW:(blk + 1) * WINDOW, r * LANES:(r + 1) * LANES] = (
                        o[r * WINDOW:(r + 1) * WINDOW, :].astype(BF16))
            yield

    yield from conv_units()
    yield from attention_units()

    kv_s[0:WINDOW, :] = kv_s[tm:tm + WINDOW, :]
    u_s[0:pad, :] = u_s[tm:tm + pad, :]


def _mix_body(tm, *refs):
    for _ in _mix_steps(tm, pl.program_id(1) == 0, *refs):
        pass


def _mix_prompt(z, sinks3, cw, cb3, lg3, lb3, layer, n_seq, seq_len):
    tm = TM_ROWS
    n_tiles = seq_len // tm
    rows = n_seq * seq_len
    row = lambda s, i: s * n_tiles + i
    vec = lambda width: pl.BlockSpec((None, 1, width), lambda s, i: (layer, 0, 0))
    return pl.pallas_call(
        functools.partial(_mix_body, tm),
        grid=(n_seq, n_tiles),
        in_specs=[
            pl.BlockSpec((tm, ATTN_WIDTH), lambda s, i: (row(s, i), 0)),
            pl.BlockSpec((tm, 2 * LANES), lambda s, i: (row(s, i), Q_END // (2 * LANES))),
            pl.BlockSpec((tm, CONV_CH), lambda s, i: (row(s, i), V_END // CONV_CH)),
            pl.BlockSpec((tm, CONV_CH), lambda s, i: (row(s, i), V_END // CONV_CH + 1)),
            vec(N_HEADS),
            pl.BlockSpec((None, CONV_WIDTH, CONV_CH), lambda s, i: (layer, 0, 0)),
            vec(CONV_CH), vec(CONV_CH), vec(CONV_CH),
        ],
        out_specs=[
            pl.BlockSpec((tm, V_END), lambda s, i: (row(s, i), 0)),
            pl.BlockSpec((None, CONV_WIDTH - 1, CONV_CH), lambda s, i: (s, 0, 0)),
        ],
        out_shape=[jax.ShapeDtypeStruct((rows, V_END), BF16),
                   jax.ShapeDtypeStruct((n_seq, CONV_WIDTH - 1, CONV_CH), F32)],
        scratch_shapes=[pltpu.VMEM((tm + WINDOW, 2 * LANES), F32),
                        pltpu.VMEM((tm + 32, CONV_CH), F32),
                        pltpu.VMEM((7, tm + 24, CONV_CH), F32),
                        pltpu.VMEM((N_KV_HEADS, KV_REP * WINDOW, 2 * WINDOW), F32)],
        compiler_params=_params(2),
        name="mix_prompt",
    )(z, z, z, z, sinks3, cw, cb3, lg3, lb3)


def _dec_body(q_ref, kn_ref, vn_ref, k2_ref, v2_ref, ck_ref, cv_ref, a_ref, gg_ref, cc_ref, u_ref, hr_ref, hi_ref,
              sink_ref, cw_ref, cb_ref, lg_ref, lb_ref, bm_ref, cm_ref, ab_ref, d_ref, gw_ref, gb_ref,
              o_ref, ok_ref, ov_ref, co_ref, oc_ref, so_ref, or_ref, oi_ref):
    nb = DEC_BLOCK
    win = ck_ref.shape[2]

    q3 = q_ref[...].reshape(nb, N_HEADS, LANES)
    kn = kn_ref[...]
    vn = vn_ref[...]
    ck = ck_ref[...]
    cv = cv_ref[...]
    s = jnp.einsum("nsc,ncj->nsj", q3, ck.astype(BF16), preferred_element_type=F32)
    si = lax.broadcasted_iota(jnp.int32, (N_HEADS, win), 0)
    ji = lax.broadcasted_iota(jnp.int32, (N_HEADS, win), 1)
    head = (si % 2) * KV_REP + si // 2
    slope = jnp.zeros((N_HEADS, win), F32)
    for h in range(N_HEADS):
        slope = jnp.where(head == h, 2.0 ** (-8.0 * (h + 1) / N_HEADS), slope)
    dist = win - ji
    bias = jnp.where(dist < WINDOW, -slope * dist.astype(F32), NEG)
    s = s + bias[None]
    s_new = jnp.sum(q3.astype(F32) * kn.astype(BF16).astype(F32), -1, keepdims=True)
    sink = sink_ref[...][None]
    m = jnp.maximum(jnp.maximum(jnp.max(s, -1, keepdims=True), s_new), sink)
    p = jnp.exp(s - m)
    p_new = jnp.exp(s_new - m)
    denom = jnp.sum(p, -1, keepdims=True) + p_new + jnp.exp(sink - m)
    o = jnp.einsum("nsj,ncj->nsc", p.astype(BF16), cv.astype(BF16), preferred_element_type=F32)
    o = o + p_new.astype(BF16).astype(F32) * vn.astype(BF16).astype(F32)
    o_ref[...] = (o / denom).reshape(nb * N_HEADS, LANES)

    last = lax.broadcasted_iota(jnp.int32, (LANES, win), 1) == win - 1
    fill = jnp.zeros((LANES - nb, LANES), F32)
    knt = jnp.concatenate([k2_ref[...], fill], 0).T
    vnt = jnp.concatenate([v2_ref[...], fill], 0).T
    for i in range(nb):
        ok_ref[i] = jnp.where(last, pltpu.roll(knt, win - 1 - i, 1), pltpu.roll(ck[i], win - 1, 1))
        ov_ref[i] = jnp.where(last, pltpu.roll(vnt, win - 1 - i, 1), pltpu.roll(cv[i], win - 1, 1))

    u = a_ref[...] * _sigmoid(gg_ref[...])
    acc = cb_ref[...] + cw_ref[CONV_WIDTH - 1:CONV_WIDTH, :] * u
    for j in range(CONV_WIDTH - 1):
        acc = acc + cw_ref[j:j + 1, :] * cc_ref[j]
    mu = jnp.mean(acc, -1, keepdims=True)
    cen = acc - mu
    var = jnp.mean(cen * cen, -1, keepdims=True)
    yn = cen * lax.rsqrt(var + EPS) * lg_ref[...] + lb_ref[...]
    co_ref[...] = yn * _sigmoid(yn)
    for j in range(CONV_WIDTH - 2):
        oc_ref[j] = cc_ref[j + 1]
    oc_ref[CONV_WIDTH - 2] = u

    us = jnp.concatenate([u_ref[0], u_ref[1]], -1)
    bu = _dot(us.astype(BF16), bm_ref[...])
    a_re = ab_ref[:, :N_STATE]
    a_im = ab_ref[:, N_STATE:]
    h_re = hr_ref[...]
    h_im = hi_ref[...]
    n_re = a_re * h_re - a_im * h_im + bu[:, :N_STATE]
    n_im = a_re * h_im + a_im * h_re + bu[:, N_STATE:]
    or_ref[...] = n_re
    oi_ref[...] = n_im
    hcat = jnp.concatenate([n_re, n_im], -1).astype(BF16)
    y = _gelu_tanh(_dot(hcat, cm_ref[...]) + d_ref[...] * us)
    gate = _dot(y.astype(BF16), gw_ref[...]) + gb_ref[...]
    so_ref[...] = y * _sigmoid(gate)


def _dec_mix(q3, kn3, vn3, ck, cv, z, cct, u, hr, hi, sinks3, cw, cb3, lg3, lb3,
             bmat, cmat, ab3, d3, gw, gb3, layer):
    n = kn3.shape[0]
    win = ck.shape[3]
    nb = DEC_BLOCK
    vec = lambda width: pl.BlockSpec((None, 1, width), lambda i: (layer, 0, 0))
    mat = lambda r, c: pl.BlockSpec((None, r, c), lambda i: (layer, 0, 0))
    cache = pl.BlockSpec((None, nb, LANES, win), lambda i: (layer, i, 0, 0))
    ctx = pl.BlockSpec((None, CONV_WIDTH - 1, nb, CONV_CH), lambda i: (layer, 0, i, 0))
    state = pl.BlockSpec((None, nb, N_STATE), lambda i: (layer, i, 0))
    rowblk = lambda width: pl.BlockSpec((nb, width), lambda i: (i, 0))
    new3 = pl.BlockSpec((nb, 1, LANES), lambda i: (i, 0, 0))
    return pl.pallas_call(
        _dec_body,
        grid=(n // nb,),
        in_specs=[
            pl.BlockSpec((nb * N_HEADS, LANES), lambda i: (i, 0)), new3, new3,
            pl.BlockSpec((nb, LANES), lambda i: (i, Q_END // LANES)),
            pl.BlockSpec((nb, LANES), lambda i: (i, K_END // LANES)),
            cache, cache,
            pl.BlockSpec((nb, CONV_CH), lambda i: (i, V_END // CONV_CH)),
            pl.BlockSpec((nb, CONV_CH), lambda i: (i, V_END // CONV_CH + 1)),
            ctx,
            pl.BlockSpec((2, nb, LANES), lambda i: (0, i, 0)),
            state, state,
            mat(N_HEADS, 1), mat(CONV_WIDTH, CONV_CH), vec(CONV_CH), vec(CONV_CH), vec(CONV_CH),
            mat(SSM_CH, 2 * N_STATE), mat(2 * N_STATE, SSM_CH), vec(2 * N_STATE), vec(SSM_CH),
            mat(SSM_CH, SSM_CH), vec(SSM_CH),
        ],
        out_specs=[
            pl.BlockSpec((nb * N_HEADS, LANES), lambda i: (i, 0)),
            pl.BlockSpec((nb, LANES, win), lambda i: (i, 0, 0)),
            pl.BlockSpec((nb, LANES, win), lambda i: (i, 0, 0)),
            rowblk(CONV_CH),
            pl.BlockSpec((CONV_WIDTH - 1, nb, CONV_CH), lambda i: (0, i, 0)),
            rowblk(SSM_CH), rowblk(N_STATE), rowblk(N_STATE),
        ],
        out_shape=[
            jax.ShapeDtypeStruct((n * N_HEADS, LANES), F32),
            jax.ShapeDtypeStruct((n, LANES, win), F32),
            jax.ShapeDtypeStruct((n, LANES, win), F32),
            jax.ShapeDtypeStruct((n, CONV_CH), F32),
            jax.ShapeDtypeStruct((CONV_WIDTH - 1, n, CONV_CH), F32),
            jax.ShapeDtypeStruct((n, SSM_CH), F32),
            jax.ShapeDtypeStruct((n, N_STATE), F32),
            jax.ShapeDtypeStruct((n, N_STATE), F32),
        ],
        compiler_params=_params(1),
        name="dec_mix",
    )(q3, kn3, vn3, z, z, ck, cv, z, z, cct, u, hr, hi, sinks3, cw, cb3, lg3, lb3,
      bmat, cmat, ab3, d3, gw, gb3)


def _ssm_operands(a_re, a_im, log_dt, b_re, b_im, c_re, c_im):
    hi = lax.Precision.HIGHEST
    l_n, g_n, p_n, c_n = SSM_CHUNK, SSM_GROUPS, SSM_STATE, SSM_GROUP
    dt = jnp.exp(log_dt)[:, None]
    mag = jnp.exp(a_re * dt)
    ab_re, ab_im = mag * jnp.cos(a_im * dt), mag * jnp.sin(a_im * dt)
    den = a_re * a_re + a_im * a_im
    q_re = ((ab_re - 1.0) * a_re + ab_im * a_im) / den
    q_im = (ab_im * a_re - (ab_re - 1.0) * a_im) / den
    bt_re, bt_im = jnp.transpose(b_re, (2, 0, 1)), jnp.transpose(b_im, (2, 0, 1))
    bb_re = q_re * bt_re - q_im * bt_im
    bb_im = q_re * bt_im + q_im * bt_re
    pw_re, pw_im = [jnp.ones_like(ab_re)], [jnp.zeros_like(ab_re)]
    for _ in range(l_n):
        r, i = pw_re[-1], pw_im[-1]
        pw_re.append(r * ab_re - i * ab_im)
        pw_im.append(r * ab_im + i * ab_re)
    pb_re = [pw_re[t] * bb_re - pw_im[t] * bb_im for t in range(l_n)]
    pb_im = [pw_re[t] * bb_im + pw_im[t] * bb_re for t in range(l_n)]

    ktau = (jnp.einsum("gop,tcgp->tcgo", c_re, jnp.stack(pb_re), precision=hi)
            - jnp.einsum("gop,tcgp->tcgo", c_im, jnp.stack(pb_im), precision=hi)).reshape(l_n, c_n, SSM_CH)
    zero = jnp.zeros((c_n, SSM_CH), F32)
    k1 = jnp.concatenate([jnp.concatenate([ktau[lo - li] if lo >= li else zero for lo in range(l_n)], 1)
                          for li in range(l_n)], 0)
    m2 = jnp.concatenate([jnp.concatenate([pb_re[l_n - 1 - l].reshape(c_n, N_STATE),
                                           pb_im[l_n - 1 - l].reshape(c_n, N_STATE)], 1)
                          for l in range(l_n)], 0)
    ct_re, ct_im = jnp.transpose(c_re, (0, 2, 1)), jnp.transpose(c_im, (0, 2, 1))
    n4_re = jnp.stack([ct_re * pw_re[l + 1][:, :, None] - ct_im * pw_im[l + 1][:, :, None]
                       for l in range(l_n)], 2).reshape(N_STATE, l_n * c_n)
    n4_im = jnp.stack([ct_re * pw_im[l + 1][:, :, None] + ct_im * pw_re[l + 1][:, :, None]
                       for l in range(l_n)], 2).reshape(N_STATE, l_n * c_n)
    n4 = jnp.concatenate([n4_re, -n4_im], 0)

    chan_group = jnp.arange(SSM_CH) // c_n
    state_group = (jnp.arange(2 * N_STATE) % N_STATE) // p_n
    bmat = jnp.where(chan_group[:, None] == state_group[None, :],
                     jnp.tile(m2[(l_n - 1) * c_n:], (g_n, 1)), 0.0)
    cc = jnp.concatenate([ct_re.reshape(N_STATE, c_n), -ct_im.reshape(N_STATE, c_n)], 0)
    cmat = jnp.where(state_group[:, None] == chan_group[None, :], jnp.tile(cc, (1, g_n)), 0.0)

    flat = lambda re, im: jnp.concatenate([re.reshape(1, N_STATE), im.reshape(1, N_STATE)], -1)
    return (k1, m2, n4, bmat.astype(BF16), cmat.astype(BF16),
            flat(pw_re[l_n], pw_im[l_n]), flat(ab_re, ab_im))


def _decode_head_order():
    s = jnp.arange(N_HEADS)
    return (s % 2) * KV_REP + s // 2


def kernel(x_prompt, x_sample, cache_swa_k, cache_swa_v, cache_conv, state_ssm_re, state_ssm_im,
           norm_mix_g, w_in, attn_sinks, conv_dw_w, conv_dw_b, conv_ln_g, conv_ln_b,
           ssm_a_re, ssm_a_im, ssm_log_dt, ssm_b_re, ssm_b_im, ssm_c_re, ssm_c_im,
           ssm_d, ssm_glu_w, ssm_glu_b, w_out, norm_ffn_g, w_ff_gate, w_ff_up, w_ff_down,
           norm_final_g):
    n_seq, seq_len, _ = x_prompt.shape
    n_dec = x_sample.shape[0]
    win = cache_swa_k.shape[2]
    assert x_sample.shape[1] == 1 and win == WINDOW
    assert seq_len % TS_ROWS == 0 and n_dec % DEC_BLOCK == 0

    row3 = lambda v: v.reshape(DEPTH, 1, -1)
    g_mix, g_ffn = row3(norm_mix_g), row3(norm_ffn_g)
    sinks3 = row3(attn_sinks)
    order = _decode_head_order()
    sinks_dec = attn_sinks[:, order][:, :, None]
    cb3, lg3, lb3 = row3(conv_dw_b), row3(conv_ln_g), row3(conv_ln_b)
    d3, gb3 = row3(ssm_d), row3(ssm_glu_b)
    w_in_q = jnp.transpose(w_in[:, :, :Q_END].astype(BF16).reshape(DEPTH, D_MODEL, N_KV_HEADS, KV_REP, HEAD_DIM),
                           (0, 1, 3, 2, 4)).reshape(DEPTH, D_MODEL, Q_END)
    w_in_rest = w_in[:, :, Q_END:].astype(BF16)
    w_out_b = w_out.astype(BF16)
    wg_b, wu_b, wd_b = w_ff_gate.astype(BF16), w_ff_up.astype(BF16), w_ff_down.astype(BF16)
    gw_b = ssm_glu_w.astype(BF16)
    k1, m2, n4, bmat, cmat, a_chunk, a_step = jax.vmap(_ssm_operands)(
        ssm_a_re, ssm_a_im, ssm_log_dt, ssm_b_re, ssm_b_im, ssm_c_re, ssm_c_im)

    wo_heads = w_out_b[:, :ATTN_WIDTH].reshape(DEPTH, N_HEADS, HEAD_DIM, D_MODEL)[:, order]
    own = (jnp.arange(N_HEADS)[:, None] % 2) == jnp.arange(N_KV_HEADS)[None, :]
    wo_dec = jnp.where(own[None, :, :, None, None], wo_heads[:, :, None], 0).reshape(
        DEPTH, N_HEADS * LANES, D_MODEL)
    wo_attn = jnp.transpose(w_out_b[:, :ATTN_WIDTH].reshape(DEPTH, N_KV_HEADS, KV_REP, HEAD_DIM, D_MODEL),
                            (0, 2, 1, 3, 4)).reshape(DEPTH, ATTN_WIDTH, D_MODEL)
    wo_attn_conv = jnp.concatenate([wo_attn, w_out_b[:, ATTN_WIDTH:V_END]], 1)
    wo_conv = w_out_b[:, ATTN_WIDTH:V_END]
    wo_ssm = w_out_b[:, V_END:]

    ck = jnp.transpose(cache_swa_k, (0, 1, 3, 4, 2)).reshape(DEPTH, n_dec, LANES, win)
    cv = jnp.transpose(cache_swa_v, (0, 1, 3, 4, 2)).reshape(DEPTH, n_dec, LANES, win)
    cct = jnp.transpose(cache_conv, (0, 2, 1, 3))
    hr = state_ssm_re.reshape(DEPTH, n_dec, N_STATE)
    hi = state_ssm_im.reshape(DEPTH, n_dec, N_STATE)

    xp = x_prompt.reshape(n_seq * seq_len, D_MODEL)
    xs = x_sample.reshape(n_dec, D_MODEL)
    tm = TM_ROWS
    decspec = lambda width: pl.BlockSpec((n_dec, width), lambda i: (i, 0))
    scale = 1.0 / math.sqrt(HEAD_DIM)
    own_lane = ((jnp.arange(LANES) // HEAD_DIM)[None, None, None, :]
                == jnp.arange(N_KV_HEADS)[None, None, :, None])

    cache_out = lambda layers: jnp.transpose(
        jnp.stack(layers).reshape(DEPTH, n_dec, N_KV_HEADS, HEAD_DIM, win), (0, 1, 4, 2, 3))
    kp, vp, cp, hrp, hip = [], [], [], [], []
    ks, vs, cs, hrs, his = [], [], [], [], []
    for l in range(DEPTH):
        final = norm_final_g.reshape(1, D_MODEL) if l == DEPTH - 1 else None

        z, u = _inproj(xp, g_mix, w_in_q, w_in_rest, l, tm)
        ssm, h_last = _ssm_prompt(u, k1, m2, n4, a_chunk, d3, gw_b, gb3, l, n_seq, seq_len)
        mix, ctx = _mix_prompt(z, sinks3, conv_dw_w, cb3, lg3, lb3, l, n_seq, seq_len)
        xp = _tail(xp, g_ffn,
                   [(mix, pl.BlockSpec((tm, V_END), lambda i: (i, 0)), wo_attn_conv),
                    (ssm, pl.BlockSpec((2, tm, LANES), lambda i: (0, i, 0)), wo_ssm)],
                   wg_b, wu_b, wd_b, l, tm, final)
        z3 = z.reshape(n_seq, seq_len, C_END)[:, seq_len - WINDOW:]
        kp.append(z3[..., Q_END:K_END].reshape(n_seq, WINDOW, N_KV_HEADS, HEAD_DIM))
        vp.append(z3[..., K_END:V_END].reshape(n_seq, WINDOW, N_KV_HEADS, HEAD_DIM))
        cp.append(ctx)
        hrp.append(h_last[:, 0, :N_STATE].reshape(n_seq, SSM_GROUPS, SSM_STATE))
        hip.append(h_last[:, 0, N_STATE:].reshape(n_seq, SSM_GROUPS, SSM_STATE))

        zs, us = _inproj(xs, g_mix, w_in_q, w_in_rest, l, n_dec)
        zq = (zs[:, :Q_END] * scale).reshape(n_dec, KV_REP, 1, LANES)
        q3 = jnp.where(own_lane, zq, 0.0).astype(BF16).reshape(n_dec * N_HEADS, LANES)
        kn3 = zs[:, Q_END:K_END].reshape(n_dec, 1, LANES)
        vn3 = zs[:, K_END:V_END].reshape(n_dec, 1, LANES)
        o3, nk, nv, conv_s, nct, ssm_s, nhr, nhi = _dec_mix(
            q3, kn3, vn3, ck, cv, zs, cct, us, hr, hi, sinks_dec, conv_dw_w, cb3, lg3, lb3,
            bmat, cmat, a_step, d3, gw_b, gb3, l)
        xs = _tail(xs, g_ffn,
                   [(o3.reshape(n_dec, N_HEADS * LANES), decspec(N_HEADS * LANES), wo_dec),
                    (conv_s, decspec(CONV_CH), wo_conv),
                    (ssm_s, decspec(SSM_CH), wo_ssm)],
                   wg_b, wu_b, wd_b, l, n_dec, final)
        ks.append(nk)
        vs.append(nv)
        cs.append(jnp.transpose(nct, (1, 0, 2)))
        hrs.append(nhr.reshape(n_dec, SSM_GROUPS, SSM_STATE))
        his.append(nhi.reshape(n_dec, SSM_GROUPS, SSM_STATE))

    return (xp.reshape(n_seq, seq_len, D_MODEL), xs.reshape(n_dec, 1, D_MODEL),
            jnp.stack(kp), jnp.stack(vp), jnp.stack(cp), jnp.stack(hrp), jnp.stack(hip),
            cache_out(ks), cache_out(vs), jnp.stack(cs), jnp.stack(hrs), jnp.stack(his))
```

```python
import functools
import math

import jax
import jax.numpy as jnp
from jax import lax
from jax.experimental import pallas as pl
from jax.experimental.pallas import tpu as pltpu

D_MODEL = 1024
DEPTH = 4
HEAD_DIM = 64
ATTN_WIDTH = 512
N_HEADS = 8
N_KV_HEADS = 2
KV_REP = 4
WINDOW = 128
CONV_CH = 256
CONV_WIDTH = 31
SSM_CH = 256
SSM_GROUP = 16
SSM_GROUPS = 16
SSM_STATE = 64
D_FF = 2816
EPS = 1e-6

Q_END = ATTN_WIDTH
K_END = Q_END + N_KV_HEADS * HEAD_DIM
V_END = K_END + N_KV_HEADS * HEAD_DIM
C_END = V_END + 2 * CONV_CH
IN_COLS = C_END + SSM_CH

N_STATE = SSM_GROUPS * SSM_STATE
LANES = 128
SSM_CHUNK = 8
CHUNK_COLS = SSM_CHUNK * SSM_CH
NEG = -1e30

TM_ROWS = 512
TS_ROWS = 4096
TI_ROWS = 1024
FF_CHUNK = 256
OUT_CHUNK = 1024
DEC_BLOCK = 16
CONV_ROWS = 64
ATTN_BLOCKS = 4
VMEM_LIMIT = 56 * 1024 * 1024

F32 = jnp.float32
BF16 = jnp.bfloat16


def _params(n_axes, flags=None):
    return pltpu.CompilerParams(dimension_semantics=("arbitrary",) * n_axes,
                                vmem_limit_bytes=VMEM_LIMIT, flags=flags)


def _resident(shape, index_map):
    return pl.BlockSpec(shape, index_map, pipeline_mode=pl.Buffered(1))


def _rms(x, g):
    return x * lax.rsqrt(jnp.mean(x * x, -1, keepdims=True) + EPS) * g


def _sigmoid(x):
    return 1.0 / (1.0 + jnp.exp(-x))


def _gelu_tanh(x):
    c = math.sqrt(2.0 / math.pi)
    return 0.5 * x * (1.0 + jnp.tanh(c * (x + 0.044715 * (x * x * x))))


def _dot(a, b):
    return jnp.dot(a, b, preferred_element_type=F32)


def _inproj_body(x_ref, g_ref, wq_ref, wr_ref, z_ref, u_ref):
    h = _rms(x_ref[...], g_ref[...]).astype(BF16)
    z_ref[:, :Q_END] = _dot(h, wq_ref[...])
    zr = _dot(h, wr_ref[...])
    z_ref[:, Q_END:] = zr[:, :C_END - Q_END]
    u_ref[0] = zr[:, C_END - Q_END:C_END - Q_END + LANES]
    u_ref[1] = zr[:, C_END - Q_END + LANES:]


def _inproj(x, g3, w_q, w_rest, layer, tm):
    rows = x.shape[0]
    return pl.pallas_call(
        _inproj_body,
        grid=(rows // tm,),
        in_specs=[
            pl.BlockSpec((tm, D_MODEL), lambda i: (i, 0)),
            pl.BlockSpec((None, 1, D_MODEL), lambda i: (layer, 0, 0)),
            _resident((None, D_MODEL, Q_END), lambda i: (layer, 0, 0)),
            _resident((None, D_MODEL, IN_COLS - Q_END), lambda i: (layer, 0, 0)),
        ],
        out_specs=[
            pl.BlockSpec((tm, C_END), lambda i: (i, 0)),
            pl.BlockSpec((2, tm, LANES), lambda i: (0, i, 0)),
        ],
        out_shape=[jax.ShapeDtypeStruct((rows, C_END), F32),
                   jax.ShapeDtypeStruct((2, rows, LANES), F32)],
        compiler_params=_params(1),
        name="inproj",
    )(x, g3, w_q, w_rest)


def _tail_steps(x, acts, g_ref, wg_ref, wu_ref, wd_ref, gf_ref, o_ref, x1_s, hf_s, act_s):
    x1 = x
    for act, w_ref in acts:
        x1 = x1 + _dot(act, w_ref[...])
    hf_s[...] = _rms(x1, g_ref[...]).astype(BF16)
    x1_s[...] = x1
    yield
    for c in range(0, D_FF, FF_CHUNK):
        gate = _dot(hf_s[...], wg_ref[:, c:c + FF_CHUNK])
        up = _dot(hf_s[...], wu_ref[:, c:c + FF_CHUNK])
        act_s[:, c:c + FF_CHUNK] = (gate * _sigmoid(gate) * up).astype(BF16)
        yield
    for n in range(0, D_MODEL, OUT_CHUNK):
        o_ref[:, n:n + OUT_CHUNK] = x1_s[:, n:n + OUT_CHUNK] + _dot(act_s[...], wd_ref[:, n:n + OUT_CHUNK])
        yield
    if gf_ref is not None:
        o_ref[...] = _rms(o_ref[...], gf_ref[...])


def _tail_body(n_parts, final, *refs):
    x_ref, g_ref = refs[0], refs[1]
    parts = refs[2:2 + 2 * n_parts]
    wg_ref, wu_ref, wd_ref = refs[2 + 2 * n_parts:5 + 2 * n_parts]
    gf_ref = refs[5 + 2 * n_parts] if final else None
    o_ref, x1_s, hf_s, act_s = refs[-4:]
    acts = []
    for p in range(n_parts):
        act_ref = parts[2 * p]
        if len(act_ref.shape) == 3:
            act = jnp.concatenate([act_ref[i] for i in range(act_ref.shape[0])], -1)
        else:
            act = act_ref[...]
        acts.append((act.astype(BF16), parts[2 * p + 1]))
    for _ in _tail_steps(x_ref[...], acts, g_ref, wg_ref, wu_ref, wd_ref, gf_ref, o_ref, x1_s, hf_s, act_s):
        pass


def _tail(x, g3, parts, wg, wu, wd, layer, tm, final_g=None):
    rows = x.shape[0]
    final = final_g is not None
    in_specs = [pl.BlockSpec((tm, D_MODEL), lambda i: (i, 0)),
                pl.BlockSpec((None, 1, D_MODEL), lambda i: (layer, 0, 0))]
    args = [x, g3]
    for act, spec, w in parts:
        in_specs.append(spec)
        in_specs.append(pl.BlockSpec((None,) + w.shape[1:], lambda i: (layer, 0, 0)))
        args += [act, w]
    in_specs += [
        _resident((None, D_MODEL, D_FF), lambda i: (layer, 0, 0)),
        _resident((None, D_MODEL, D_FF), lambda i: (layer, 0, 0)),
        _resident((None, D_FF, D_MODEL), lambda i: (layer, 0, 0)),
    ]
    args += [wg, wu, wd]
    if final:
        in_specs.append(pl.BlockSpec((1, D_MODEL), lambda i: (0, 0)))
        args.append(final_g)
    return pl.pallas_call(
        functools.partial(_tail_body, len(parts), final),
        grid=(rows // tm,),
        in_specs=in_specs,
        out_specs=pl.BlockSpec((tm, D_MODEL), lambda i: (i, 0)),
        out_shape=jax.ShapeDtypeStruct((rows, D_MODEL), F32),
        scratch_shapes=[pltpu.VMEM((tm, D_MODEL), F32), pltpu.VMEM((tm, D_MODEL), BF16),
                        pltpu.VMEM((tm, D_FF), BF16)],
        compiler_params=_params(1),
        name="tail",
    )(*args)


def _expand_ssm_operands(k1_ref, m2_ref, n4_ref, w1_ref, w2_ref, w4_ref):
    col = lax.broadcasted_iota(jnp.int32, (SSM_GROUP, CHUNK_COLS), 1)
    col_chan_group = (col % SSM_CH) // SSM_GROUP
    col_state_group = (col % N_STATE) // SSM_STATE
    for l in range(SSM_CHUNK):
        k1 = k1_ref[l * SSM_GROUP:(l + 1) * SSM_GROUP, :]
        m2 = m2_ref[l * SSM_GROUP:(l + 1) * SSM_GROUP, :]
        for g in range(SSM_GROUPS):
            r0 = l * SSM_CH + g * SSM_GROUP
            w1_ref[r0:r0 + SSM_GROUP, :] = jnp.where(col_chan_group == g, k1, 0.0).astype(BF16)
            w2_ref[r0:r0 + SSM_GROUP, :] = jnp.where(col_state_group == g, m2, 0.0).astype(BF16)
    e_row = lax.broadcasted_iota(jnp.int32, (SSM_CHUNK * SSM_GROUP, CHUNK_COLS), 0)
    e_col = lax.broadcasted_iota(jnp.int32, (SSM_CHUNK * SSM_GROUP, CHUNK_COLS), 1)
    spread = ((e_row // SSM_GROUP == e_col // SSM_CH) & (e_row % SSM_GROUP == e_col % SSM_GROUP))
    spread = jnp.where(spread, 1.0, 0.0).astype(BF16)
    rows = lax.broadcasted_iota(jnp.int32, (LANES, CHUNK_COLS), 0)
    cols = lax.broadcasted_iota(jnp.int32, (LANES, CHUNK_COLS), 1)
    for b in range(2 * N_STATE // LANES):
        full = _dot(n4_ref[b * LANES:(b + 1) * LANES, :].astype(BF16), spread)
        row_group = ((rows + b * LANES) % N_STATE) // SSM_STATE
        keep = row_group == (cols % SSM_CH) // SSM_GROUP
        w4_ref[b * LANES:(b + 1) * LANES, :] = jnp.where(keep, full, 0.0).astype(BF16)


def _ssm_body(n_chunks, u_ref, k1_ref, m2_ref, n4_ref, al_ref, d_ref, gw_ref, gb_ref,
              o_ref, hl_ref, x_s, gh_s, hc_s, w1_ref, w2_ref, w4_ref):
    @pl.when((pl.program_id(0) == 0) & (pl.program_id(1) == 0))
    def _():
        _expand_ssm_operands(k1_ref, m2_ref, n4_ref, w1_ref, w2_ref, w4_ref)

    @pl.when(pl.program_id(1) == 0)
    def _():
        hc_s[...] = jnp.zeros_like(hc_s)

    def steps(l, s):
        return u_ref[s, pl.ds(l, n_chunks, stride=SSM_CHUNK), :]

    for l in range(SSM_CHUNK):
        for s in range(2):
            c0 = l * SSM_CH + s * LANES
            x_s[:, c0:c0 + LANES] = steps(l, s).astype(BF16)

    gh_s[...] = _dot(x_s[...], w2_ref[...])

    a_re = al_ref[:, :N_STATE]
    a_im = al_ref[:, N_STATE:]

    def step(k, carry):
        h_re, h_im = carry
        g_re = gh_s[pl.ds(k, 1), :N_STATE]
        g_im = gh_s[pl.ds(k, 1), N_STATE:]
        gh_s[pl.ds(k, 1), :N_STATE] = h_re
        gh_s[pl.ds(k, 1), N_STATE:] = h_im
        return (a_re * h_re - a_im * h_im + g_re, a_re * h_im + a_im * h_re + g_im)

    h_re, h_im = lax.fori_loop(0, n_chunks, step, (hc_s[:, :N_STATE], hc_s[:, N_STATE:]))
    hc_s[:, :N_STATE] = h_re
    hc_s[:, N_STATE:] = h_im
    hl_ref[:, :N_STATE] = h_re
    hl_ref[:, N_STATE:] = h_im

    hb = gh_s[...].astype(BF16)
    for l in range(SSM_CHUNK):
        c0, c1 = l * SSM_CH, (l + 1) * SSM_CH
        y = _dot(x_s[:, :c1], w1_ref[:c1, c0:c1]) + _dot(hb, w4_ref[:, c0:c1])
        y = _gelu_tanh(y + d_ref[...] * jnp.concatenate([steps(l, 0), steps(l, 1)], -1))
        gate = _dot(y.astype(BF16), gw_ref[...]) + gb_ref[...]
        out = y * _sigmoid(gate)
        for s in range(2):
            o_ref[s, pl.ds(l, n_chunks, stride=SSM_CHUNK), :] = out[:, s * LANES:(s + 1) * LANES]


def _ssm_prompt(u, k1, m2, n4, al, d3, gw, gb3, layer, n_seq, seq_len):
    ts = TS_ROWS
    n_tiles = seq_len // ts
    n_chunks = ts // SSM_CHUNK
    rows = u.shape[1]
    compact = SSM_CHUNK * SSM_GROUP
    return pl.pallas_call(
        functools.partial(_ssm_body, n_chunks),
        grid=(n_seq, n_tiles),
        in_specs=[
            pl.BlockSpec((2, ts, LANES), lambda s, i: (0, s * n_tiles + i, 0)),
            pl.BlockSpec((None, compact, CHUNK_COLS), lambda s, i: (layer, 0, 0)),
            pl.BlockSpec((None, compact, 2 * N_STATE), lambda s, i: (layer, 0, 0)),
            pl.BlockSpec((None, 2 * N_STATE, compact), lambda s, i: (layer, 0, 0)),
            pl.BlockSpec((None, 1, 2 * N_STATE), lambda s, i: (layer, 0, 0)),
            pl.BlockSpec((None, 1, SSM_CH), lambda s, i: (layer, 0, 0)),
            pl.BlockSpec((None, SSM_CH, SSM_CH), lambda s, i: (layer, 0, 0)),
            pl.BlockSpec((None, 1, SSM_CH), lambda s, i: (layer, 0, 0)),
        ],
        out_specs=[
            pl.BlockSpec((2, ts, LANES), lambda s, i: (0, s * n_tiles + i, 0)),
            pl.BlockSpec((None, 1, 2 * N_STATE), lambda s, i: (s, 0, 0)),
        ],
        out_shape=[jax.ShapeDtypeStruct((2, rows, LANES), F32),
                   jax.ShapeDtypeStruct((n_seq, 1, 2 * N_STATE), F32)],
        scratch_shapes=[
            pltpu.VMEM((n_chunks, CHUNK_COLS), BF16),
            pltpu.VMEM((n_chunks, 2 * N_STATE), F32),
            pltpu.VMEM((1, 2 * N_STATE), F32),
            pltpu.VMEM((CHUNK_COLS, CHUNK_COLS), BF16),
            pltpu.VMEM((CHUNK_COLS, 2 * N_STATE), BF16),
            pltpu.VMEM((2 * N_STATE, CHUNK_COLS), BF16),
        ],
        compiler_params=_params(2),
        name="ssm_prompt",
    )(u, k1, m2, n4, al, d3, gw, gb3)


def _mix_steps(tm, first_tile, q_ref, kv_ref, a_ref, gg_ref, sink_ref, cw_ref, cb_ref, lg_ref, lb_ref,
                 o_ref, ctx_ref, kv_s, u_s, ush_s, bias_s):
    pad = 32
    off = pad - (CONV_WIDTH - 1)
    sub = 8

    @pl.when(first_tile)
    def _():
        kv_s[0:WINDOW, :] = jnp.zeros((WINDOW, 2 * LANES), F32)
        u_s[0:pad, :] = jnp.zeros((pad, CONV_CH), F32)
        qi = lax.broadcasted_iota(jnp.int32, (WINDOW, 2 * WINDOW), 0)
        ki = lax.broadcasted_iota(jnp.int32, (WINDOW, 2 * WINDOW), 1)
        dist = qi - ki + WINDOW
        valid = (dist >= 0) & (dist < WINDOW)
        distf = dist.astype(F32)
        for g in range(N_KV_HEADS):
            for r in range(KV_REP):
                slope = 2.0 ** (-8.0 * (g * KV_REP + r + 1) / N_HEADS)
                bias_s[g, r * WINDOW:(r + 1) * WINDOW, :] = jnp.where(valid, -slope * distf, NEG)

    kv_s[WINDOW:, :] = kv_ref[...]
    u_s[pad:, :] = a_ref[...] * _sigmoid(gg_ref[...])
    for b in range(1, sub):
        ush_s[b - 1] = u_s[b:b + tm + pad - sub, :]
    ctx_ref[...] = u_s[tm + off:tm + pad, :]
    yield

    def conv_units():
        for r0 in range(0, tm, CONV_ROWS):
            acc = jnp.zeros((CONV_ROWS, CONV_CH), F32) + cb_ref[...]
            for j in range(CONV_WIDTH):
                a0, b = divmod(off + j, sub)
                lo = r0 + a0 * sub
                rows = u_s[lo:lo + CONV_ROWS, :] if b == 0 else ush_s[b - 1, lo:lo + CONV_ROWS, :]
                acc = acc + cw_ref[j:j + 1, :] * rows
            mu = jnp.mean(acc, -1, keepdims=True)
            cen = acc - mu
            var = jnp.mean(cen * cen, -1, keepdims=True)
            yn = cen * lax.rsqrt(var + EPS) * lg_ref[...] + lb_ref[...]
            o_ref[r0:r0 + CONV_ROWS, ATTN_WIDTH:] = (yn * _sigmoid(yn)).astype(BF16)
            yield

    scale = 1.0 / math.sqrt(HEAD_DIM)
    k_lane_group = lax.broadcasted_iota(jnp.int32, (2 * WINDOW, LANES), 1) // HEAD_DIM
    o_lane_group = lax.broadcasted_iota(jnp.int32, (KV_REP * WINDOW, LANES), 1) // HEAD_DIM
    key_in_prev = lax.broadcasted_iota(jnp.int32, (1, 2 * WINDOW), 1) < WINDOW
    no_prev_block = jnp.logical_and(key_in_prev, first_tile)
    sinks = [jnp.concatenate([jnp.broadcast_to(sink_ref[:, g * KV_REP + r:g * KV_REP + r + 1], (WINDOW, 1))
                              for r in range(KV_REP)], 0) for g in range(N_KV_HEADS)]

    def attention_units():
        n_blocks = tm // WINDOW
        for b0 in range(0, n_blocks, ATTN_BLOCKS):
            blocks = range(b0, min(b0 + ATTN_BLOCKS, n_blocks))
            pairs = [(blk, g) for blk in blocks for g in range(N_KV_HEADS)]
            kblk = {blk: kv_s[blk * WINDOW:(blk + 2) * WINDOW, 0:LANES] for blk in blocks}
            vblk = {blk: kv_s[blk * WINDOW:(blk + 2) * WINDOW, LANES:].astype(BF16) for blk in blocks}
            qs = {blk: (jnp.concatenate([q_ref[blk * WINDOW:(blk + 1) * WINDOW, r * LANES:(r + 1) * LANES]
                                         for r in range(KV_REP)], 0) * scale).astype(BF16) for blk in blocks}
            kg = {(blk, g): jnp.where(k_lane_group == g, kblk[blk], 0.0).astype(BF16) for blk, g in pairs}
            s = {(blk, g): lax.dot_general(qs[blk], kg[blk, g], (((1,), (1,)), ((), ())),
                                           preferred_element_type=F32) + bias_s[g] for blk, g in pairs}
            for g in range(N_KV_HEADS):
                if b0 == 0:
                    s[0, g] = jnp.where(no_prev_block, NEG, s[0, g])
            m = {k: jnp.maximum(jnp.max(s[k], -1, keepdims=True), sinks[k[1]]) for k in pairs}
            p = {k: jnp.exp(s[k] - m[k]) for k in pairs}
            denom = {k: jnp.sum(p[k], -1, keepdims=True) + jnp.exp(sinks[k[1]] - m[k]) for k in pairs}
            og = {k: _dot(p[k].astype(BF16), vblk[k[0]]) / denom[k] for k in pairs}
            for blk in blocks:
                o = jnp.where(o_lane_group == 0, og[blk, 0], og[blk, 1])
                for r in range(KV_REP):
                    o_ref[blk * WINDOW:(blk + 1) * WINDOW, r * LANES:(r + 1) * LANES] = (
                        o[r * WINDOW:(r + 1) * WINDOW, :].astype(BF16))
            yield

    yield from conv_units()
    yield from attention_units()

    kv_s[0:WINDOW, :] = kv_s[tm:tm + WINDOW, :]
    u_s[0:pad, :] = u_s[tm:tm + pad, :]


def _mix_body(tm, *refs):
    for _ in _mix_steps(tm, pl.program_id(1) == 0, *refs):
        pass


def _mix_prompt(z, sinks3, cw, cb3, lg3, lb3, layer, n_seq, seq_len):
    tm = TM_ROWS
    n_tiles = seq_len // tm
    rows = n_seq * seq_len
    row = lambda s, i: s * n_tiles + i
    vec = lambda width: pl.BlockSpec((None, 1, width), lambda s, i: (layer, 0, 0))
    return pl.pallas_call(
        functools.partial(_mix_body, tm),
        grid=(n_seq, n_tiles),
        in_specs=[
            pl.BlockSpec((tm, ATTN_WIDTH), lambda s, i: (row(s, i), 0)),
            pl.BlockSpec((tm, 2 * LANES), lambda s, i: (row(s, i), Q_END // (2 * LANES))),
            pl.BlockSpec((tm, CONV_CH), lambda s, i: (row(s, i), V_END // CONV_CH)),
            pl.BlockSpec((tm, CONV_CH), lambda s, i: (row(s, i), V_END // CONV_CH + 1)),
            vec(N_HEADS),
            pl.BlockSpec((None, CONV_WIDTH, CONV_CH), lambda s, i: (layer, 0, 0)),
            vec(CONV_CH), vec(CONV_CH), vec(CONV_CH),
        ],
        out_specs=[
            pl.BlockSpec((tm, V_END), lambda s, i: (row(s, i), 0)),
            pl.BlockSpec((None, CONV_WIDTH - 1, CONV_CH), lambda s, i: (s, 0, 0)),
        ],
        out_shape=[jax.ShapeDtypeStruct((rows, V_END), BF16),
                   jax.ShapeDtypeStruct((n_seq, CONV_WIDTH - 1, CONV_CH), F32)],
        scratch_shapes=[pltpu.VMEM((tm + WINDOW, 2 * LANES), F32),
                        pltpu.VMEM((tm + 32, CONV_CH), F32),
                        pltpu.VMEM((7, tm + 24, CONV_CH), F32),
                        pltpu.VMEM((N_KV_HEADS, KV_REP * WINDOW, 2 * WINDOW), F32)],
        compiler_params=_params(2),
        name="mix_prompt",
    )(z, z, z, z, sinks3, cw, cb3, lg3, lb3)


def _dec_body(q_ref, kn_ref, vn_ref, k2_ref, v2_ref, ck_ref, cv_ref, a_ref, gg_ref, cc_ref, u_ref, hr_ref, hi_ref,
              sink_ref, cw_ref, cb_ref, lg_ref, lb_ref, bm_ref, cm_ref, ab_ref, d_ref, gw_ref, gb_ref,
              kall_ref, vall_ref, o_ref, ok_ref, ov_ref, co_ref, oc_ref, so_ref, or_ref, oi_ref):
    nb = DEC_BLOCK
    win = ck_ref.shape[2]

    q3 = q_ref[...].reshape(nb, N_HEADS, LANES)
    kn = kn_ref[...]
    vn = vn_ref[...]
    ck = ck_ref[...]
    cv = cv_ref[...]
    s = jnp.einsum("nsc,ncj->nsj", q3, ck.astype(BF16), preferred_element_type=F32)
    si = lax.broadcasted_iota(jnp.int32, (N_HEADS, win), 0)
    ji = lax.broadcasted_iota(jnp.int32, (N_HEADS, win), 1)
    head = (si % 2) * KV_REP + si // 2
    slope = jnp.zeros((N_HEADS, win), F32)
    for h in range(N_HEADS):
        slope = jnp.where(head == h, 2.0 ** (-8.0 * (h + 1) / N_HEADS), slope)
    dist = win - ji
    bias = jnp.where(dist < WINDOW, -slope * dist.astype(F32), NEG)
    s = s + bias[None]
    s_new = jnp.sum(q3.astype(F32) * kn.astype(BF16).astype(F32), -1, keepdims=True)
    sink = sink_ref[...][None]
    m = jnp.maximum(jnp.maximum(jnp.max(s, -1, keepdims=True), s_new), sink)
    p = jnp.exp(s - m)
    p_new = jnp.exp(s_new - m)
    denom = jnp.sum(p, -1, keepdims=True) + p_new + jnp.exp(sink - m)
    o = jnp.einsum("nsj,ncj->nsc", p.astype(BF16), cv.astype(BF16), preferred_element_type=F32)
    o = o + p_new.astype(BF16).astype(F32) * vn.astype(BF16).astype(F32)
    o_ref[...] = (o / denom).reshape(nb * N_HEADS, LANES)

    last = lax.broadcasted_iota(jnp.int32, (LANES, win), 1) == win - 1
    fill = jnp.zeros((LANES - nb, LANES), F32)
    knt = jnp.concatenate([k2_ref[...], fill], 0).T
    vnt = jnp.concatenate([v2_ref[...], fill], 0).T
    for i in range(nb):
        ok_ref[i] = jnp.where(last, pltpu.roll(knt, win - 1 - i, 1), pltpu.roll(ck[i], win - 1, 1))
        ov_ref[i] = jnp.where(last, pltpu.roll(vnt, win - 1 - i, 1), pltpu.roll(cv[i], win - 1, 1))

    u = a_ref[...] * _sigmoid(gg_ref[...])
    acc = cb_ref[...] + cw_ref[CONV_WIDTH - 1:CONV_WIDTH, :] * u
    for j in range(CONV_WIDTH - 1):
        acc = acc + cw_ref[j:j + 1, :] * cc_ref[j]
    mu = jnp.mean(acc, -1, keepdims=True)
    cen = acc - mu
    var = jnp.mean(cen * cen, -1, keepdims=True)
    yn = cen * lax.rsqrt(var + EPS) * lg_ref[...] + lb_ref[...]
    co_ref[...] = yn * _sigmoid(yn)
    for j in range(CONV_WIDTH - 2):
        oc_ref[j] = cc_ref[j + 1]
    oc_ref[CONV_WIDTH - 2] = u

    us = jnp.concatenate([u_ref[0], u_ref[1]], -1)
    bu = _dot(us.astype(BF16), bm_ref[...])
    a_re = ab_ref[:, :N_STATE]
    a_im = ab_ref[:, N_STATE:]
    h_re = hr_ref[...]
    h_im = hi_ref[...]
    n_re = a_re * h_re - a_im * h_im + bu[:, :N_STATE]
    n_im = a_re * h_im + a_im * h_re + bu[:, N_STATE:]
    or_ref[...] = n_re
    oi_ref[...] = n_im
    hcat = jnp.concatenate([n_re, n_im], -1).astype(BF16)
    y = _gelu_tanh(_dot(hcat, cm_ref[...]) + d_ref[...] * us)
    gate = _dot(y.astype(BF16), gw_ref[...]) + gb_ref[...]
    so_ref[...] = y * _sigmoid(gate)


def _dec_mix(q3, kn3, vn3, ck, cv, z, cct, u, hr, hi, sinks3, cw, cb3, lg3, lb3,
             bmat, cmat, ab3, d3, gw, gb3, k_all, v_all, layer):
    n = kn3.shape[0]
    win = ck.shape[3]
    nb = DEC_BLOCK
    vec = lambda width: pl.BlockSpec((None, 1, width), lambda i: (layer, 0, 0))
    mat = lambda r, c: pl.BlockSpec((None, r, c), lambda i: (layer, 0, 0))
    cache = pl.BlockSpec((None, nb, LANES, win), lambda i: (layer, i, 0, 0))
    ctx = pl.BlockSpec((None, CONV_WIDTH - 1, nb, CONV_CH), lambda i: (layer, 0, i, 0))
    state = pl.BlockSpec((None, nb, N_STATE), lambda i: (layer, i, 0))
    rowblk = lambda width: pl.BlockSpec((nb, width), lambda i: (i, 0))
    new3 = pl.BlockSpec((nb, 1, LANES), lambda i: (i, 0, 0))
    return pl.pallas_call(
        _dec_body,
        grid=(n // nb,),
        in_specs=[
            pl.BlockSpec((nb * N_HEADS, LANES), lambda i: (i, 0)), new3, new3,
            pl.BlockSpec((nb, LANES), lambda i: (i, Q_END // LANES)),
            pl.BlockSpec((nb, LANES), lambda i: (i, K_END // LANES)),
            cache, cache,
            pl.BlockSpec((nb, CONV_CH), lambda i: (i, V_END // CONV_CH)),
            pl.BlockSpec((nb, CONV_CH), lambda i: (i, V_END // CONV_CH + 1)),
            ctx,
            pl.BlockSpec((2, nb, LANES), lambda i: (0, i, 0)),
            state, state,
            mat(N_HEADS, 1), mat(CONV_WIDTH, CONV_CH), vec(CONV_CH), vec(CONV_CH), vec(CONV_CH),
            mat(SSM_CH, 2 * N_STATE), mat(2 * N_STATE, SSM_CH), vec(2 * N_STATE), vec(SSM_CH),
            mat(SSM_CH, SSM_CH), vec(SSM_CH),
            pl.BlockSpec(memory_space=pl.ANY), pl.BlockSpec(memory_space=pl.ANY),
        ],
        out_specs=[
            pl.BlockSpec((nb * N_HEADS, LANES), lambda i: (i, 0)),
            cache, cache,
            rowblk(CONV_CH),
            pl.BlockSpec((CONV_WIDTH - 1, nb, CONV_CH), lambda i: (0, i, 0)),
            rowblk(SSM_CH), rowblk(N_STATE), rowblk(N_STATE),
        ],
        out_shape=[
            jax.ShapeDtypeStruct((n * N_HEADS, LANES), F32),
            jax.ShapeDtypeStruct(k_all.shape, F32),
            jax.ShapeDtypeStruct(v_all.shape, F32),
            jax.ShapeDtypeStruct((n, CONV_CH), F32),
            jax.ShapeDtypeStruct((CONV_WIDTH - 1, n, CONV_CH), F32),
            jax.ShapeDtypeStruct((n, SSM_CH), F32),
            jax.ShapeDtypeStruct((n, N_STATE), F32),
            jax.ShapeDtypeStruct((n, N_STATE), F32),
        ],
        input_output_aliases={24: 1, 25: 2},
        compiler_params=_params(1),
        name="dec_mix",
    )(q3, kn3, vn3, z, z, ck, cv, z, z, cct, u, hr, hi, sinks3, cw, cb3, lg3, lb3,
      bmat, cmat, ab3, d3, gw, gb3, k_all, v_all)


def _ssm_operands(a_re, a_im, log_dt, b_re, b_im, c_re, c_im):
    hi = lax.Precision.HIGHEST
    l_n, g_n, p_n, c_n = SSM_CHUNK, SSM_GROUPS, SSM_STATE, SSM_GROUP
    dt = jnp.exp(log_dt)[:, None]
    lam_re, lam_im = a_re * dt, a_im * dt
    steps = jnp.arange(l_n + 1, dtype=F32)[:, None, None]
    mag = jnp.exp(steps * lam_re)
    pw_re, pw_im = mag * jnp.cos(steps * lam_im), mag * jnp.sin(steps * lam_im)
    ab_re, ab_im = pw_re[1], pw_im[1]
    den = a_re * a_re + a_im * a_im
    q_re = ((ab_re - 1.0) * a_re + ab_im * a_im) / den
    q_im = (ab_im * a_re - (ab_re - 1.0) * a_im) / den
    bt_re, bt_im = jnp.transpose(b_re, (2, 0, 1)), jnp.transpose(b_im, (2, 0, 1))
    bb_re = q_re * bt_re - q_im * bt_im
    bb_im = q_re * bt_im + q_im * bt_re
    pl_re, pl_im = pw_re[:l_n, None], pw_im[:l_n, None]
    pb_re = pl_re * bb_re - pl_im * bb_im
    pb_im = pl_re * bb_im + pl_im * bb_re

    ktau = (jnp.einsum("gop,tcgp->tcgo", c_re, pb_re, precision=hi)
            - jnp.einsum("gop,tcgp->tcgo", c_im, pb_im, precision=hi)).reshape(l_n, c_n, SSM_CH)
    lag = jnp.arange(l_n)[None, :] - jnp.arange(l_n)[:, None]
    k1 = jnp.where((lag >= 0)[:, :, None, None], jnp.take(ktau, jnp.maximum(lag, 0), axis=0), 0.0)
    k1 = jnp.transpose(k1, (0, 2, 1, 3)).reshape(l_n * c_n, CHUNK_COLS)
    back = l_n - 1 - jnp.arange(l_n)
    m2 = jnp.concatenate([jnp.take(pb_re, back, axis=0).reshape(l_n, c_n, N_STATE),
                          jnp.take(pb_im, back, axis=0).reshape(l_n, c_n, N_STATE)], -1)
    m2 = m2.reshape(l_n * c_n, 2 * N_STATE)
    ct_re, ct_im = jnp.transpose(c_re, (0, 2, 1)), jnp.transpose(c_im, (0, 2, 1))
    pn_re = jnp.transpose(pw_re[1:], (1, 2, 0))[..., None]
    pn_im = jnp.transpose(pw_im[1:], (1, 2, 0))[..., None]
    n4_re = (ct_re[:, :, None, :] * pn_re - ct_im[:, :, None, :] * pn_im).reshape(N_STATE, l_n * c_n)
    n4_im = (ct_re[:, :, None, :] * pn_im + ct_im[:, :, None, :] * pn_re).reshape(N_STATE, l_n * c_n)
    n4 = jnp.concatenate([n4_re, -n4_im], 0)

    chan_group = jnp.arange(SSM_CH) // c_n
    state_group = (jnp.arange(2 * N_STATE) % N_STATE) // p_n
    bmat = jnp.where(chan_group[:, None] == state_group[None, :],
                     jnp.tile(m2[(l_n - 1) * c_n:], (g_n, 1)), 0.0)
    cc = jnp.concatenate([ct_re.reshape(N_STATE, c_n), -ct_im.reshape(N_STATE, c_n)], 0)
    cmat = jnp.where(state_group[:, None] == chan_group[None, :], jnp.tile(cc, (1, g_n)), 0.0)

    flat = lambda re, im: jnp.concatenate([re.reshape(1, N_STATE), im.reshape(1, N_STATE)], -1)
    return (k1, m2, n4, bmat.astype(BF16), cmat.astype(BF16),
            flat(pw_re[l_n], pw_im[l_n]), flat(ab_re, ab_im))


def _decode_head_order():
    s = jnp.arange(N_HEADS)
    return (s % 2) * KV_REP + s // 2


def kernel(x_prompt, x_sample, cache_swa_k, cache_swa_v, cache_conv, state_ssm_re, state_ssm_im,
           norm_mix_g, w_in, attn_sinks, conv_dw_w, conv_dw_b, conv_ln_g, conv_ln_b,
           ssm_a_re, ssm_a_im, ssm_log_dt, ssm_b_re, ssm_b_im, ssm_c_re, ssm_c_im,
           ssm_d, ssm_glu_w, ssm_glu_b, w_out, norm_ffn_g, w_ff_gate, w_ff_up, w_ff_down,
           norm_final_g):
    n_seq, seq_len, _ = x_prompt.shape
    n_dec = x_sample.shape[0]
    win = cache_swa_k.shape[2]
    assert x_sample.shape[1] == 1 and win == WINDOW
    assert seq_len % TS_ROWS == 0 and n_dec % DEC_BLOCK == 0

    row3 = lambda v: v.reshape(DEPTH, 1, -1)
    g_mix, g_ffn = row3(norm_mix_g), row3(norm_ffn_g)
    sinks3 = row3(attn_sinks)
    order = _decode_head_order()
    sinks_dec = attn_sinks[:, order][:, :, None]
    cb3, lg3, lb3 = row3(conv_dw_b), row3(conv_ln_g), row3(conv_ln_b)
    d3, gb3 = row3(ssm_d), row3(ssm_glu_b)
    w_in_q = jnp.transpose(w_in[:, :, :Q_END].astype(BF16).reshape(DEPTH, D_MODEL, N_KV_HEADS, KV_REP, HEAD_DIM),
                           (0, 1, 3, 2, 4)).reshape(DEPTH, D_MODEL, Q_END)
    w_in_rest = w_in[:, :, Q_END:].astype(BF16)
    w_out_b = w_out.astype(BF16)
    wg_b, wu_b, wd_b = w_ff_gate.astype(BF16), w_ff_up.astype(BF16), w_ff_down.astype(BF16)
    gw_b = ssm_glu_w.astype(BF16)
    k1, m2, n4, bmat, cmat, a_chunk, a_step = jax.vmap(_ssm_operands)(
        ssm_a_re, ssm_a_im, ssm_log_dt, ssm_b_re, ssm_b_im, ssm_c_re, ssm_c_im)

    wo_heads = w_out_b[:, :ATTN_WIDTH].reshape(DEPTH, N_HEADS, HEAD_DIM, D_MODEL)[:, order]
    own = (jnp.arange(N_HEADS)[:, None] % 2) == jnp.arange(N_KV_HEADS)[None, :]
    wo_dec = jnp.where(own[None, :, :, None, None], wo_heads[:, :, None], 0).reshape(
        DEPTH, N_HEADS * LANES, D_MODEL)
    wo_attn = jnp.transpose(w_out_b[:, :ATTN_WIDTH].reshape(DEPTH, N_KV_HEADS, KV_REP, HEAD_DIM, D_MODEL),
                            (0, 2, 1, 3, 4)).reshape(DEPTH, ATTN_WIDTH, D_MODEL)
    wo_attn_conv = jnp.concatenate([wo_attn, w_out_b[:, ATTN_WIDTH:V_END]], 1)
    wo_conv = w_out_b[:, ATTN_WIDTH:V_END]
    wo_ssm = w_out_b[:, V_END:]

    ck = jnp.transpose(cache_swa_k, (0, 1, 3, 4, 2)).reshape(DEPTH, n_dec, LANES, win)
    cv = jnp.transpose(cache_swa_v, (0, 1, 3, 4, 2)).reshape(DEPTH, n_dec, LANES, win)
    cct = jnp.transpose(cache_conv, (0, 2, 1, 3))
    hr = state_ssm_re.reshape(DEPTH, n_dec, N_STATE)
    hi = state_ssm_im.reshape(DEPTH, n_dec, N_STATE)

    xp = x_prompt.reshape(n_seq * seq_len, D_MODEL)
    xs = x_sample.reshape(n_dec, D_MODEL)
    tm = TM_ROWS
    decspec = lambda width: pl.BlockSpec((n_dec, width), lambda i: (i, 0))
    scale = 1.0 / math.sqrt(HEAD_DIM)
    own_lane = ((jnp.arange(LANES) // HEAD_DIM)[None, None, None, :]
                == jnp.arange(N_KV_HEADS)[None, None, :, None])

    cache_out = lambda c: jnp.transpose(c.reshape(DEPTH, n_dec, N_KV_HEADS, HEAD_DIM, win), (0, 1, 4, 2, 3))
    k_all = jnp.zeros((DEPTH, n_dec, LANES, win), F32)
    v_all = jnp.zeros((DEPTH, n_dec, LANES, win), F32)
    kp, vp, cp, hrp, hip = [], [], [], [], []
    cs, hrs, his = [], [], []
    for l in range(DEPTH):
        final = norm_final_g.reshape(1, D_MODEL) if l == DEPTH - 1 else None

        z, u = _inproj(xp, g_mix, w_in_q, w_in_rest, l, TI_ROWS)
        ssm, h_last = _ssm_prompt(u, k1, m2, n4, a_chunk, d3, gw_b, gb3, l, n_seq, seq_len)
        mix, ctx = _mix_prompt(z, sinks3, conv_dw_w, cb3, lg3, lb3, l, n_seq, seq_len)
        xp = _tail(xp, g_ffn,
                   [(mix, pl.BlockSpec((tm, V_END), lambda i: (i, 0)), wo_attn_conv),
                    (ssm, pl.BlockSpec((2, tm, LANES), lambda i: (0, i, 0)), wo_ssm)],
                   wg_b, wu_b, wd_b, l, tm, final)
        z3 = z.reshape(n_seq, seq_len, C_END)[:, seq_len - WINDOW:]
        kp.append(z3[..., Q_END:K_END].reshape(n_seq, WINDOW, N_KV_HEADS, HEAD_DIM))
        vp.append(z3[..., K_END:V_END].reshape(n_seq, WINDOW, N_KV_HEADS, HEAD_DIM))
        cp.append(ctx)
        hrp.append(h_last[:, 0, :N_STATE].reshape(n_seq, SSM_GROUPS, SSM_STATE))
        hip.append(h_last[:, 0, N_STATE:].reshape(n_seq, SSM_GROUPS, SSM_STATE))

        zs, us = _inproj(xs, g_mix, w_in_q, w_in_rest, l, n_dec)
        zq = (zs[:, :Q_END] * scale).reshape(n_dec, KV_REP, 1, LANES)
        q3 = jnp.where(own_lane, zq, 0.0).astype(BF16).reshape(n_dec * N_HEADS, LANES)
        kn3 = zs[:, Q_END:K_END].reshape(n_dec, 1, LANES)
        vn3 = zs[:, K_END:V_END].reshape(n_dec, 1, LANES)
        o3, k_all, v_all, conv_s, nct, ssm_s, nhr, nhi = _dec_mix(
            q3, kn3, vn3, ck, cv, zs, cct, us, hr, hi, sinks_dec, conv_dw_w, cb3, lg3, lb3,
            bmat, cmat, a_step, d3, gw_b, gb3, k_all, v_all, l)
        xs = _tail(xs, g_ffn,
                   [(o3.reshape(n_dec, N_HEADS * LANES), decspec(N_HEADS * LANES), wo_dec),
                    (conv_s, decspec(CONV_CH), wo_conv),
                    (ssm_s, decspec(SSM_CH), wo_ssm)],
                   wg_b, wu_b, wd_b, l, n_dec, final)
        cs.append(jnp.transpose(nct, (1, 0, 2)))
        hrs.append(nhr.reshape(n_dec, SSM_GROUPS, SSM_STATE))
        his.append(nhi.reshape(n_dec, SSM_GROUPS, SSM_STATE))

    return (xp.reshape(n_seq, seq_len, D_MODEL), xs.reshape(n_dec, 1, D_MODEL),
            jnp.stack(kp), jnp.stack(vp), jnp.stack(cp), jnp.stack(hrp), jnp.stack(hip),
            cache_out(k_all), cache_out(v_all), jnp.stack(cs), jnp.stack(hrs), jnp.stack(his))
```

```python
import functools
import math

import jax
import jax.numpy as jnp
from jax import lax
from jax.experimental import pallas as pl
from jax.experimental.pallas import tpu as pltpu

D_MODEL = 1024
DEPTH = 4
HEAD_DIM = 64
ATTN_WIDTH = 512
N_HEADS = 8
N_KV_HEADS = 2
KV_REP = 4
WINDOW = 128
CONV_CH = 256
CONV_WIDTH = 31
SSM_CH = 256
SSM_GROUP = 16
SSM_GROUPS = 16
SSM_STATE = 64
D_FF = 2816
EPS = 1e-6

Q_END = ATTN_WIDTH
K_END = Q_END + N_KV_HEADS * HEAD_DIM
V_END = K_END + N_KV_HEADS * HEAD_DIM
C_END = V_END + 2 * CONV_CH
IN_COLS = C_END + SSM_CH

N_STATE = SSM_GROUPS * SSM_STATE
LANES = 128
SSM_CHUNK = 8
CHUNK_COLS = SSM_CHUNK * SSM_CH
SSM_HALVES = SSM_CH // LANES
HALF_COLS = CHUNK_COLS // SSM_HALVES
HALF_STATE = N_STATE // SSM_HALVES
NEG = -1e30

TM_ROWS = 512
TS_ROWS = 4096
TI_ROWS = 1024
FF_CHUNK = 256
OUT_CHUNK = 1024
DEC_BLOCK = 16
CONV_ROWS = 64
ATTN_BLOCKS = 4
VMEM_LIMIT = 56 * 1024 * 1024

F32 = jnp.float32
BF16 = jnp.bfloat16


def _params(n_axes, flags=None):
    return pltpu.CompilerParams(dimension_semantics=("arbitrary",) * n_axes,
                                vmem_limit_bytes=VMEM_LIMIT, flags=flags)


def _resident(shape, index_map):
    return pl.BlockSpec(shape, index_map, pipeline_mode=pl.Buffered(1))


def _rms(x, g):
    return x * lax.rsqrt(jnp.mean(x * x, -1, keepdims=True) + EPS) * g


def _sigmoid(x):
    return 1.0 / (1.0 + jnp.exp(-x))


def _gelu_tanh(x):
    c = math.sqrt(2.0 / math.pi)
    return 0.5 * x * (1.0 + jnp.tanh(c * (x + 0.044715 * (x * x * x))))


def _dot(a, b):
    return jnp.dot(a, b, preferred_element_type=F32)


def _inproj_body(x_ref, g_ref, wq_ref, wr_ref, z_ref, u_ref):
    h = _rms(x_ref[...], g_ref[...]).astype(BF16)
    z_ref[:, :Q_END] = _dot(h, wq_ref[...])
    zr = _dot(h, wr_ref[...])
    z_ref[:, Q_END:] = zr[:, :C_END - Q_END]
    u_ref[0] = zr[:, C_END - Q_END:C_END - Q_END + LANES]
    u_ref[1] = zr[:, C_END - Q_END + LANES:]


def _inproj(x, g3, w_q, w_rest, layer, tm):
    rows = x.shape[0]
    return pl.pallas_call(
        _inproj_body,
        grid=(rows // tm,),
        in_specs=[
            pl.BlockSpec((tm, D_MODEL), lambda i: (i, 0)),
            pl.BlockSpec((None, 1, D_MODEL), lambda i: (layer, 0, 0)),
            _resident((None, D_MODEL, Q_END), lambda i: (layer, 0, 0)),
            _resident((None, D_MODEL, IN_COLS - Q_END), lambda i: (layer, 0, 0)),
        ],
        out_specs=[
            pl.BlockSpec((tm, C_END), lambda i: (i, 0)),
            pl.BlockSpec((2, tm, LANES), lambda i: (0, i, 0)),
        ],
        out_shape=[jax.ShapeDtypeStruct((rows, C_END), F32),
                   jax.ShapeDtypeStruct((2, rows, LANES), F32)],
        compiler_params=_params(1),
        name="inproj",
    )(x, g3, w_q, w_rest)


def _tail_steps(x, acts, g_ref, wg_ref, wu_ref, wd_ref, gf_ref, o_ref, x1_s, hf_s, act_s):
    x1 = x
    for act, w_ref in acts:
        x1 = x1 + _dot(act, w_ref[...])
    hf_s[...] = _rms(x1, g_ref[...]).astype(BF16)
    x1_s[...] = x1
    yield
    for c in range(0, D_FF, FF_CHUNK):
        gate = _dot(hf_s[...], wg_ref[:, c:c + FF_CHUNK])
        up = _dot(hf_s[...], wu_ref[:, c:c + FF_CHUNK])
        act_s[:, c:c + FF_CHUNK] = (gate * _sigmoid(gate) * up).astype(BF16)
        yield
    for n in range(0, D_MODEL, OUT_CHUNK):
        o_ref[:, n:n + OUT_CHUNK] = x1_s[:, n:n + OUT_CHUNK] + _dot(act_s[...], wd_ref[:, n:n + OUT_CHUNK])
        yield
    if gf_ref is not None:
        o_ref[...] = _rms(o_ref[...], gf_ref[...])


def _tail_body(n_parts, final, *refs):
    x_ref, g_ref = refs[0], refs[1]
    parts = refs[2:2 + 2 * n_parts]
    wg_ref, wu_ref, wd_ref = refs[2 + 2 * n_parts:5 + 2 * n_parts]
    gf_ref = refs[5 + 2 * n_parts] if final else None
    o_ref, x1_s, hf_s, act_s = refs[-4:]
    acts = []
    for p in range(n_parts):
        act_ref = parts[2 * p]
        if len(act_ref.shape) == 3:
            act = jnp.concatenate([act_ref[i] for i in range(act_ref.shape[0])], -1)
        else:
            act = act_ref[...]
        acts.append((act.astype(BF16), parts[2 * p + 1]))
    for _ in _tail_steps(x_ref[...], acts, g_ref, wg_ref, wu_ref, wd_ref, gf_ref, o_ref, x1_s, hf_s, act_s):
        pass


def _tail(x, g3, parts, wg, wu, wd, layer, tm, final_g=None):
    rows = x.shape[0]
    final = final_g is not None
    in_specs = [pl.BlockSpec((tm, D_MODEL), lambda i: (i, 0)),
                pl.BlockSpec((None, 1, D_MODEL), lambda i: (layer, 0, 0))]
    args = [x, g3]
    for act, spec, w in parts:
        in_specs.append(spec)
        in_specs.append(pl.BlockSpec((None,) + w.shape[1:], lambda i: (layer, 0, 0)))
        args += [act, w]
    in_specs += [
        _resident((None, D_MODEL, D_FF), lambda i: (layer, 0, 0)),
        _resident((None, D_MODEL, D_FF), lambda i: (layer, 0, 0)),
        _resident((None, D_FF, D_MODEL), lambda i: (layer, 0, 0)),
    ]
    args += [wg, wu, wd]
    if final:
        in_specs.append(pl.BlockSpec((1, D_MODEL), lambda i: (0, 0)))
        args.append(final_g)
    return pl.pallas_call(
        functools.partial(_tail_body, len(parts), final),
        grid=(rows // tm,),
        in_specs=in_specs,
        out_specs=pl.BlockSpec((tm, D_MODEL), lambda i: (i, 0)),
        out_shape=jax.ShapeDtypeStruct((rows, D_MODEL), F32),
        scratch_shapes=[pltpu.VMEM((tm, D_MODEL), F32), pltpu.VMEM((tm, D_MODEL), BF16),
                        pltpu.VMEM((tm, D_FF), BF16)],
        compiler_params=_params(1),
        name="tail",
    )(*args)


def _expand_ssm_operands(k1_ref, m2_ref, n4_ref, w1_ref, w2_ref, w4_ref):
    compact = SSM_CHUNK * SSM_GROUP
    col = lax.broadcasted_iota(jnp.int32, (SSM_GROUP, HALF_COLS), 1)
    col_chan_group = (col % LANES) // SSM_GROUP
    col_state_group = (col % HALF_STATE) // SSM_STATE
    e_row = lax.broadcasted_iota(jnp.int32, (compact, HALF_COLS), 0)
    e_col = lax.broadcasted_iota(jnp.int32, (compact, HALF_COLS), 1)
    spread = ((e_row // SSM_GROUP == e_col // LANES) & (e_row % SSM_GROUP == e_col % SSM_GROUP))
    spread = jnp.where(spread, 1.0, 0.0).astype(BF16)
    rows = lax.broadcasted_iota(jnp.int32, (LANES, HALF_COLS), 0)
    cols = lax.broadcasted_iota(jnp.int32, (LANES, HALF_COLS), 1)
    for o in range(SSM_HALVES):
        for l in range(SSM_CHUNK):
            k1 = k1_ref[o, l * SSM_GROUP:(l + 1) * SSM_GROUP, :]
            m2 = m2_ref[o, l * SSM_GROUP:(l + 1) * SSM_GROUP, :]
            for g in range(LANES // SSM_GROUP):
                r0 = l * LANES + g * SSM_GROUP
                w1_ref[o, r0:r0 + SSM_GROUP, :] = jnp.where(col_chan_group == g, k1, 0.0).astype(BF16)
                w2_ref[o, r0:r0 + SSM_GROUP, :] = jnp.where(col_state_group == g, m2, 0.0).astype(BF16)
        for b in range(2 * HALF_STATE // LANES):
            full = _dot(n4_ref[o, b * LANES:(b + 1) * LANES, :].astype(BF16), spread)
            row_group = ((rows + b * LANES) % HALF_STATE) // SSM_STATE
            keep = row_group == (cols % LANES) // SSM_GROUP
            w4_ref[o, b * LANES:(b + 1) * LANES, :] = jnp.where(keep, full, 0.0).astype(BF16)


def _ssm_body(n_chunks, u_ref, k1_ref, m2_ref, n4_ref, al_ref, d_ref, gw_ref, gb_ref,
              o_ref, hl_ref, x_s, gh_s, hc_s, w1_ref, w2_ref, w4_ref):
    @pl.when((pl.program_id(0) == 0) & (pl.program_id(1) == 0))
    def _():
        _expand_ssm_operands(k1_ref, m2_ref, n4_ref, w1_ref, w2_ref, w4_ref)

    @pl.when(pl.program_id(1) == 0)
    def _():
        hc_s[...] = jnp.zeros_like(hc_s)

    def steps(l, o):
        return u_ref[o, pl.ds(l, n_chunks, stride=SSM_CHUNK), :]

    for o in range(SSM_HALVES):
        for l in range(SSM_CHUNK):
            c0 = o * HALF_COLS + l * LANES
            x_s[:, c0:c0 + LANES] = steps(l, o).astype(BF16)

    for o in range(SSM_HALVES):
        g = _dot(x_s[:, o * HALF_COLS:(o + 1) * HALF_COLS], w2_ref[o])
        gh_s[:, o * HALF_STATE:(o + 1) * HALF_STATE] = g[:, :HALF_STATE]
        gh_s[:, N_STATE + o * HALF_STATE:N_STATE + (o + 1) * HALF_STATE] = g[:, HALF_STATE:]

    a_re = al_ref[:, :N_STATE]
    a_im = al_ref[:, N_STATE:]

    def step(k, carry):
        h_re, h_im = carry
        g_re = gh_s[pl.ds(k, 1), :N_STATE]
        g_im = gh_s[pl.ds(k, 1), N_STATE:]
        gh_s[pl.ds(k, 1), :N_STATE] = h_re
        gh_s[pl.ds(k, 1), N_STATE:] = h_im
        return (a_re * h_re - a_im * h_im + g_re, a_re * h_im + a_im * h_re + g_im)

    h_re, h_im = lax.fori_loop(0, n_chunks, step, (hc_s[:, :N_STATE], hc_s[:, N_STATE:]))
    hc_s[:, :N_STATE] = h_re
    hc_s[:, N_STATE:] = h_im
    hl_ref[:, :N_STATE] = h_re
    hl_ref[:, N_STATE:] = h_im

    hb = gh_s[...].astype(BF16)
    hcat = [jnp.concatenate([hb[:, o * HALF_STATE:(o + 1) * HALF_STATE],
                             hb[:, N_STATE + o * HALF_STATE:N_STATE + (o + 1) * HALF_STATE]], -1)
            for o in range(SSM_HALVES)]
    pair = 2 * LANES
    for l0 in range(0, SSM_CHUNK, 2):
        c0, k1 = l0 * LANES, (l0 + 2) * LANES
        ys = [_dot(x_s[:, o * HALF_COLS:o * HALF_COLS + k1], w1_ref[o, :k1, c0:c0 + pair])
              + _dot(hcat[o], w4_ref[o, :, c0:c0 + pair]) for o in range(SSM_HALVES)]
        for dl in range(2):
            l = l0 + dl
            y = jnp.concatenate([ys[o][:, dl * LANES:(dl + 1) * LANES] for o in range(SSM_HALVES)], -1)
            y = _gelu_tanh(y + d_ref[...] * jnp.concatenate([steps(l, o) for o in range(SSM_HALVES)], -1))
            gate = _dot(y.astype(BF16), gw_ref[...]) + gb_ref[...]
            out = y * _sigmoid(gate)
            for o in range(SSM_HALVES):
                o_ref[o, pl.ds(l, n_chunks, stride=SSM_CHUNK), :] = out[:, o * LANES:(o + 1) * LANES]


def _ssm_prompt(u, k1, m2, n4, al, d3, gw, gb3, layer, n_seq, seq_len):
    ts = TS_ROWS
    n_tiles = seq_len // ts
    n_chunks = ts // SSM_CHUNK
    rows = u.shape[1]
    compact = SSM_CHUNK * SSM_GROUP
    return pl.pallas_call(
        functools.partial(_ssm_body, n_chunks),
        grid=(n_seq, n_tiles),
        in_specs=[
            pl.BlockSpec((SSM_HALVES, ts, LANES), lambda s, i: (0, s * n_tiles + i, 0)),
            pl.BlockSpec((None, SSM_HALVES, compact, HALF_COLS), lambda s, i: (layer, 0, 0, 0)),
            pl.BlockSpec((None, SSM_HALVES, compact, 2 * HALF_STATE), lambda s, i: (layer, 0, 0, 0)),
            pl.BlockSpec((None, SSM_HALVES, 2 * HALF_STATE, compact), lambda s, i: (layer, 0, 0, 0)),
            pl.BlockSpec((None, 1, 2 * N_STATE), lambda s, i: (layer, 0, 0)),
            pl.BlockSpec((None, 1, SSM_CH), lambda s, i: (layer, 0, 0)),
            pl.BlockSpec((None, SSM_CH, SSM_CH), lambda s, i: (layer, 0, 0)),
            pl.BlockSpec((None, 1, SSM_CH), lambda s, i: (layer, 0, 0)),
        ],
        out_specs=[
            pl.BlockSpec((SSM_HALVES, ts, LANES), lambda s, i: (0, s * n_tiles + i, 0)),
            pl.BlockSpec((None, 1, 2 * N_STATE), lambda s, i: (s, 0, 0)),
        ],
        out_shape=[jax.ShapeDtypeStruct((SSM_HALVES, rows, LANES), F32),
                   jax.ShapeDtypeStruct((n_seq, 1, 2 * N_STATE), F32)],
        scratch_shapes=[
            pltpu.VMEM((n_chunks, CHUNK_COLS), BF16),
            pltpu.VMEM((n_chunks, 2 * N_STATE), F32),
            pltpu.VMEM((1, 2 * N_STATE), F32),
            pltpu.VMEM((SSM_HALVES, HALF_COLS, HALF_COLS), BF16),
            pltpu.VMEM((SSM_HALVES, HALF_COLS, 2 * HALF_STATE), BF16),
            pltpu.VMEM((SSM_HALVES, 2 * HALF_STATE, HALF_COLS), BF16),
        ],
        compiler_params=_params(2),
        name="ssm_prompt",
    )(u, k1, m2, n4, al, d3, gw, gb3)


def _mix_steps(tm, first_tile, q_ref, kv_ref, a_ref, gg_ref, sink_ref, cw_ref, cb_ref, lg_ref, lb_ref,
                 o_ref, ctx_ref, kv_s, u_s, ush_s, bias_s):
    pad = 32
    off = pad - (CONV_WIDTH - 1)
    sub = 8

    @pl.when(first_tile)
    def _():
        kv_s[0:WINDOW, :] = jnp.zeros((WINDOW, 2 * LANES), F32)
        u_s[0:pad, :] = jnp.zeros((pad, CONV_CH), F32)
        qi = lax.broadcasted_iota(jnp.int32, (WINDOW, 2 * WINDOW), 0)
        ki = lax.broadcasted_iota(jnp.int32, (WINDOW, 2 * WINDOW), 1)
        dist = qi - ki + WINDOW
        valid = (dist >= 0) & (dist < WINDOW)
        distf = dist.astype(F32)
        for g in range(N_KV_HEADS):
            for r in range(KV_REP):
                slope = 2.0 ** (-8.0 * (g * KV_REP + r + 1) / N_HEADS)
                bias_s[g, r * WINDOW:(r + 1) * WINDOW, :] = jnp.where(valid, -slope * distf, NEG)

    kv_s[WINDOW:, :] = kv_ref[...]
    u_s[pad:, :] = a_ref[...] * _sigmoid(gg_ref[...])
    for b in range(1, sub):
        ush_s[b - 1] = u_s[b:b + tm + pad - sub, :]
    ctx_ref[...] = u_s[tm + off:tm + pad, :]
    yield

    def conv_units():
        for r0 in range(0, tm, CONV_ROWS):
            acc = jnp.zeros((CONV_ROWS, CONV_CH), F32) + cb_ref[...]
            for j in range(CONV_WIDTH):
                a0, b = divmod(off + j, sub)
                lo = r0 + a0 * sub
                rows = u_s[lo:lo + CONV_ROWS, :] if b == 0 else ush_s[b - 1, lo:lo + CONV_ROWS, :]
                acc = acc + cw_ref[j:j + 1, :] * rows
            mu = jnp.mean(acc, -1, keepdims=True)
            cen = acc - mu
            var = jnp.mean(cen * cen, -1, keepdims=True)
            yn = cen * lax.rsqrt(var + EPS) * lg_ref[...] + lb_ref[...]
            o_ref[r0:r0 + CONV_ROWS, ATTN_WIDTH:] = (yn * _sigmoid(yn)).astype(BF16)
            yield

    scale = 1.0 / math.sqrt(HEAD_DIM)
    k_lane_group = lax.broadcasted_iota(jnp.int32, (2 * WINDOW, LANES), 1) // HEAD_DIM
    o_lane_group = lax.broadcasted_iota(jnp.int32, (KV_REP * WINDOW, LANES), 1) // HEAD_DIM
    key_in_prev = lax.broadcasted_iota(jnp.int32, (1, 2 * WINDOW), 1) < WINDOW
    no_prev_block = jnp.logical_and(key_in_prev, first_tile)
    sinks = [jnp.concatenate([jnp.broadcast_to(sink_ref[:, g * KV_REP + r:g * KV_REP + r + 1], (WINDOW, 1))
                              for r in range(KV_REP)], 0) for g in range(N_KV_HEADS)]

    def attention_units():
        n_blocks = tm // WINDOW
        for b0 in range(0, n_blocks, ATTN_BLOCKS):
            blocks = range(b0, min(b0 + ATTN_BLOCKS, n_blocks))
            pairs = [(blk, g) for blk in blocks for g in range(N_KV_HEADS)]
            kblk = {blk: kv_s[blk * WINDOW:(blk + 2) * WINDOW, 0:LANES] for blk in blocks}
            vblk = {blk: kv_s[blk * WINDOW:(blk + 2) * WINDOW, LANES:].astype(BF16) for blk in blocks}
            qs = {blk: (jnp.concatenate([q_ref[blk * WINDOW:(blk + 1) * WINDOW, r * LANES:(r + 1) * LANES]
                                         for r in range(KV_REP)], 0) * scale).astype(BF16) for blk in blocks}
            kg = {(blk, g): jnp.where(k_lane_group == g, kblk[blk], 0.0).astype(BF16) for blk, g in pairs}
            s = {(blk, g): lax.dot_general(qs[blk], kg[blk, g], (((1,), (1,)), ((), ())),
                                           preferred_element_type=F32) + bias_s[g] for blk, g in pairs}
            for g in range(N_KV_HEADS):
                if b0 == 0:
                    s[0, g] = jnp.where(no_prev_block, NEG, s[0, g])
            m = {k: jnp.maximum(jnp.max(s[k], -1, keepdims=True), sinks[k[1]]) for k in pairs}
            p = {k: jnp.exp(s[k] - m[k]) for k in pairs}
            denom = {k: jnp.sum(p[k], -1, keepdims=True) + jnp.exp(sinks[k[1]] - m[k]) for k in pairs}
            og = {k: _dot(p[k].astype(BF16), vblk[k[0]]) / denom[k] for k in pairs}
            for blk in blocks:
                o = jnp.where(o_lane_group == 0, og[blk, 0], og[blk, 1])
                for r in range(KV_REP):
                    o_ref[blk * WINDOW:(blk + 1) * WINDOW, r * LANES:(r + 1) * LANES] = (
                        o[r * WINDOW:(r + 1) * WINDOW, :].astype(BF16))
            yield

    yield from conv_units()
    yield from attention_units()

    kv_s[0:WINDOW, :] = kv_s[tm:tm + WINDOW, :]
    u_s[0:pad, :] = u_s[tm:tm + pad, :]


def _mix_body(tm, *refs):
    for _ in _mix_steps(tm, pl.program_id(1) == 0, *refs):
        pass


def _mix_prompt(z, sinks3, cw, cb3, lg3, lb3, layer, n_seq, seq_len):
    tm = TM_ROWS
    n_tiles = seq_len // tm
    rows = n_seq * seq_len
    row = lambda s, i: s * n_tiles + i
    vec = lambda width: pl.BlockSpec((None, 1, width), lambda s, i: (layer, 0, 0))
    return pl.pallas_call(
        functools.partial(_mix_body, tm),
        grid=(n_seq, n_tiles),
        in_specs=[
            pl.BlockSpec((tm, ATTN_WIDTH), lambda s, i: (row(s, i), 0)),
            pl.BlockSpec((tm, 2 * LANES), lambda s, i: (row(s, i), Q_END // (2 * LANES))),
            pl.BlockSpec((tm, CONV_CH), lambda s, i: (row(s, i), V_END // CONV_CH)),
            pl.BlockSpec((tm, CONV_CH), lambda s, i: (row(s, i), V_END // CONV_CH + 1)),
            vec(N_HEADS),
            pl.BlockSpec((None, CONV_WIDTH, CONV_CH), lambda s, i: (layer, 0, 0)),
            vec(CONV_CH), vec(CONV_CH), vec(CONV_CH),
        ],
        out_specs=[
            pl.BlockSpec((tm, V_END), lambda s, i: (row(s, i), 0)),
            pl.BlockSpec((None, CONV_WIDTH - 1, CONV_CH), lambda s, i: (s, 0, 0)),
        ],
        out_shape=[jax.ShapeDtypeStruct((rows, V_END), BF16),
                   jax.ShapeDtypeStruct((n_seq, CONV_WIDTH - 1, CONV_CH), F32)],
        scratch_shapes=[pltpu.VMEM((tm + WINDOW, 2 * LANES), F32),
                        pltpu.VMEM((tm + 32, CONV_CH), F32),
                        pltpu.VMEM((7, tm + 24, CONV_CH), F32),
                        pltpu.VMEM((N_KV_HEADS, KV_REP * WINDOW, 2 * WINDOW), F32)],
        compiler_params=_params(2),
        name="mix_prompt",
    )(z, z, z, z, sinks3, cw, cb3, lg3, lb3)


def _dec_body(q_ref, kn_ref, vn_ref, k2_ref, v2_ref, ck_ref, cv_ref, a_ref, gg_ref, cc_ref, u_ref, hr_ref, hi_ref,
              sink_ref, cw_ref, cb_ref, lg_ref, lb_ref, bm_ref, cm_ref, ab_ref, d_ref, gw_ref, gb_ref,
              kall_ref, vall_ref, o_ref, ok_ref, ov_ref, co_ref, oc_ref, so_ref, or_ref, oi_ref):
    nb = DEC_BLOCK
    win = ck_ref.shape[2]

    q3 = q_ref[...].reshape(nb, N_HEADS, LANES)
    kn = kn_ref[...]
    vn = vn_ref[...]
    ck = ck_ref[...]
    cv = cv_ref[...]
    s = jnp.einsum("nsc,ncj->nsj", q3, ck.astype(BF16), preferred_element_type=F32)
    si = lax.broadcasted_iota(jnp.int32, (N_HEADS, win), 0)
    ji = lax.broadcasted_iota(jnp.int32, (N_HEADS, win), 1)
    head = (si % 2) * KV_REP + si // 2
    slope = jnp.zeros((N_HEADS, win), F32)
    for h in range(N_HEADS):
        slope = jnp.where(head == h, 2.0 ** (-8.0 * (h + 1) / N_HEADS), slope)
    dist = win - ji
    bias = jnp.where(dist < WINDOW, -slope * dist.astype(F32), NEG)
    s = s + bias[None]
    s_new = jnp.sum(q3.astype(F32) * kn.astype(BF16).astype(F32), -1, keepdims=True)
    sink = sink_ref[...][None]
    m = jnp.maximum(jnp.maximum(jnp.max(s, -1, keepdims=True), s_new), sink)
    p = jnp.exp(s - m)
    p_new = jnp.exp(s_new - m)
    denom = jnp.sum(p, -1, keepdims=True) + p_new + jnp.exp(sink - m)
    o = jnp.einsum("nsj,ncj->nsc", p.astype(BF16), cv.astype(BF16), preferred_element_type=F32)
    o = o + p_new.astype(BF16).astype(F32) * vn.astype(BF16).astype(F32)
    o_ref[...] = (o / denom).reshape(nb * N_HEADS, LANES)

    last = lax.broadcasted_iota(jnp.int32, (LANES, win), 1) == win - 1
    fill = jnp.zeros((LANES - nb, LANES), F32)
    knt = jnp.concatenate([k2_ref[...], fill], 0).T
    vnt = jnp.concatenate([v2_ref[...], fill], 0).T
    for i in range(nb):
        ok_ref[i] = jnp.where(last, pltpu.roll(knt, win - 1 - i, 1), pltpu.roll(ck[i], win - 1, 1))
        ov_ref[i] = jnp.where(last, pltpu.roll(vnt, win - 1 - i, 1), pltpu.roll(cv[i], win - 1, 1))

    u = a_ref[...] * _sigmoid(gg_ref[...])
    acc = cb_ref[...] + cw_ref[CONV_WIDTH - 1:CONV_WIDTH, :] * u
    for j in range(CONV_WIDTH - 1):
        acc = acc + cw_ref[j:j + 1, :] * cc_ref[j]
    mu = jnp.mean(acc, -1, keepdims=True)
    cen = acc - mu
    var = jnp.mean(cen * cen, -1, keepdims=True)
    yn = cen * lax.rsqrt(var + EPS) * lg_ref[...] + lb_ref[...]
    co_ref[...] = yn * _sigmoid(yn)
    for j in range(CONV_WIDTH - 2):
        oc_ref[j] = cc_ref[j + 1]
    oc_ref[CONV_WIDTH - 2] = u

    us = jnp.concatenate([u_ref[0], u_ref[1]], -1)
    bu = _dot(us.astype(BF16), bm_ref[...])
    a_re = ab_ref[:, :N_STATE]
    a_im = ab_ref[:, N_STATE:]
    h_re = hr_ref[...]
    h_im = hi_ref[...]
    n_re = a_re * h_re - a_im * h_im + bu[:, :N_STATE]
    n_im = a_re * h_im + a_im * h_re + bu[:, N_STATE:]
    or_ref[...] = n_re
    oi_ref[...] = n_im
    hcat = jnp.concatenate([n_re, n_im], -1).astype(BF16)
    y = _gelu_tanh(_dot(hcat, cm_ref[...]) + d_ref[...] * us)
    gate = _dot(y.astype(BF16), gw_ref[...]) + gb_ref[...]
    so_ref[...] = y * _sigmoid(gate)


def _dec_mix(q3, kn3, vn3, ck, cv, z, cct, u, hr, hi, sinks3, cw, cb3, lg3, lb3,
             bmat, cmat, ab3, d3, gw, gb3, k_all, v_all, layer):
    n = kn3.shape[0]
    win = ck.shape[3]
    nb = DEC_BLOCK
    vec = lambda width: pl.BlockSpec((None, 1, width), lambda i: (layer, 0, 0))
    mat = lambda r, c: pl.BlockSpec((None, r, c), lambda i: (layer, 0, 0))
    cache = pl.BlockSpec((None, nb, LANES, win), lambda i: (layer, i, 0, 0))
    ctx = pl.BlockSpec((None, CONV_WIDTH - 1, nb, CONV_CH), lambda i: (layer, 0, i, 0))
    state = pl.BlockSpec((None, nb, N_STATE), lambda i: (layer, i, 0))
    rowblk = lambda width: pl.BlockSpec((nb, width), lambda i: (i, 0))
    new3 = pl.BlockSpec((nb, 1, LANES), lambda i: (i, 0, 0))
    return pl.pallas_call(
        _dec_body,
        grid=(n // nb,),
        in_specs=[
            pl.BlockSpec((nb * N_HEADS, LANES), lambda i: (i, 0)), new3, new3,
            pl.BlockSpec((nb, LANES), lambda i: (i, Q_END // LANES)),
            pl.BlockSpec((nb, LANES), lambda i: (i, K_END // LANES)),
            cache, cache,
            pl.BlockSpec((nb, CONV_CH), lambda i: (i, V_END // CONV_CH)),
            pl.BlockSpec((nb, CONV_CH), lambda i: (i, V_END // CONV_CH + 1)),
            ctx,
            pl.BlockSpec((2, nb, LANES), lambda i: (0, i, 0)),
            state, state,
            mat(N_HEADS, 1), mat(CONV_WIDTH, CONV_CH), vec(CONV_CH), vec(CONV_CH), vec(CONV_CH),
            mat(SSM_CH, 2 * N_STATE), mat(2 * N_STATE, SSM_CH), vec(2 * N_STATE), vec(SSM_CH),
            mat(SSM_CH, SSM_CH), vec(SSM_CH),
            pl.BlockSpec(memory_space=pl.ANY), pl.BlockSpec(memory_space=pl.ANY),
        ],
        out_specs=[
            pl.BlockSpec((nb * N_HEADS, LANES), lambda i: (i, 0)),
            cache, cache,
            rowblk(CONV_CH),
            pl.BlockSpec((CONV_WIDTH - 1, nb, CONV_CH), lambda i: (0, i, 0)),
            rowblk(SSM_CH), rowblk(N_STATE), rowblk(N_STATE),
        ],
        out_shape=[
            jax.ShapeDtypeStruct((n * N_HEADS, LANES), F32),
            jax.ShapeDtypeStruct(k_all.shape, F32),
            jax.ShapeDtypeStruct(v_all.shape, F32),
            jax.ShapeDtypeStruct((n, CONV_CH), F32),
            jax.ShapeDtypeStruct((CONV_WIDTH - 1, n, CONV_CH), F32),
            jax.ShapeDtypeStruct((n, SSM_CH), F32),
            jax.ShapeDtypeStruct((n, N_STATE), F32),
            jax.ShapeDtypeStruct((n, N_STATE), F32),
        ],
        input_output_aliases={24: 1, 25: 2},
        compiler_params=_params(1),
        name="dec_mix",
    )(q3, kn3, vn3, z, z, ck, cv, z, z, cct, u, hr, hi, sinks3, cw, cb3, lg3, lb3,
      bmat, cmat, ab3, d3, gw, gb3, k_all, v_all)


def _ssm_operands(a_re, a_im, log_dt, b_re, b_im, c_re, c_im):
    hi = lax.Precision.HIGHEST
    l_n, g_n, p_n, c_n = SSM_CHUNK, SSM_GROUPS, SSM_STATE, SSM_GROUP
    dt = jnp.exp(log_dt)[:, None]
    lam_re, lam_im = a_re * dt, a_im * dt
    steps = jnp.arange(l_n + 1, dtype=F32)[:, None, None]
    mag = jnp.exp(steps * lam_re)
    pw_re, pw_im = mag * jnp.cos(steps * lam_im), mag * jnp.sin(steps * lam_im)
    ab_re, ab_im = pw_re[1], pw_im[1]
    den = a_re * a_re + a_im * a_im
    q_re = ((ab_re - 1.0) * a_re + ab_im * a_im) / den
    q_im = (ab_im * a_re - (ab_re - 1.0) * a_im) / den
    bt_re, bt_im = jnp.transpose(b_re, (2, 0, 1)), jnp.transpose(b_im, (2, 0, 1))
    bb_re = q_re * bt_re - q_im * bt_im
    bb_im = q_re * bt_im + q_im * bt_re
    pl_re, pl_im = pw_re[:l_n, None], pw_im[:l_n, None]
    pb_re = pl_re * bb_re - pl_im * bb_im
    pb_im = pl_re * bb_im + pl_im * bb_re

    ktau = (jnp.einsum("gop,tcgp->tcgo", c_re, pb_re, precision=hi)
            - jnp.einsum("gop,tcgp->tcgo", c_im, pb_im, precision=hi)).reshape(l_n, c_n, SSM_CH)
    lag = jnp.arange(l_n)[None, :] - jnp.arange(l_n)[:, None]
    k1 = jnp.where((lag >= 0)[:, :, None, None], jnp.take(ktau, jnp.maximum(lag, 0), axis=0), 0.0)
    k1 = jnp.transpose(k1, (0, 2, 1, 3)).reshape(l_n * c_n, CHUNK_COLS)
    back = l_n - 1 - jnp.arange(l_n)
    m2 = jnp.concatenate([jnp.take(pb_re, back, axis=0).reshape(l_n, c_n, N_STATE),
                          jnp.take(pb_im, back, axis=0).reshape(l_n, c_n, N_STATE)], -1)
    m2 = m2.reshape(l_n * c_n, 2 * N_STATE)
    ct_re, ct_im = jnp.transpose(c_re, (0, 2, 1)), jnp.transpose(c_im, (0, 2, 1))
    pn_re = jnp.transpose(pw_re[1:], (1, 2, 0))[..., None]
    pn_im = jnp.transpose(pw_im[1:], (1, 2, 0))[..., None]
    n4_re = (ct_re[:, :, None, :] * pn_re - ct_im[:, :, None, :] * pn_im).reshape(N_STATE, l_n * c_n)
    n4_im = (ct_re[:, :, None, :] * pn_im + ct_im[:, :, None, :] * pn_re).reshape(N_STATE, l_n * c_n)
    n4 = jnp.concatenate([n4_re, -n4_im], 0)

    chan_group = jnp.arange(SSM_CH) // c_n
    state_group = (jnp.arange(2 * N_STATE) % N_STATE) // p_n
    bmat = jnp.where(chan_group[:, None] == state_group[None, :],
                     jnp.tile(m2[(l_n - 1) * c_n:], (g_n, 1)), 0.0)
    cc = jnp.concatenate([ct_re.reshape(N_STATE, c_n), -ct_im.reshape(N_STATE, c_n)], 0)
    cmat = jnp.where(state_group[:, None] == chan_group[None, :], jnp.tile(cc, (1, g_n)), 0.0)

    k1_h = jnp.transpose(k1.reshape(l_n * c_n, l_n, SSM_HALVES, LANES), (2, 0, 1, 3)).reshape(
        SSM_HALVES, l_n * c_n, HALF_COLS)
    m2_h = jnp.transpose(m2.reshape(l_n * c_n, 2, SSM_HALVES, HALF_STATE), (2, 0, 1, 3)).reshape(
        SSM_HALVES, l_n * c_n, 2 * HALF_STATE)
    n4_h = jnp.transpose(n4.reshape(2, SSM_HALVES, HALF_STATE, l_n * c_n), (1, 0, 2, 3)).reshape(
        SSM_HALVES, 2 * HALF_STATE, l_n * c_n)

    flat = lambda re, im: jnp.concatenate([re.reshape(1, N_STATE), im.reshape(1, N_STATE)], -1)
    return (k1_h, m2_h, n4_h, bmat.astype(BF16), cmat.astype(BF16),
            flat(pw_re[l_n], pw_im[l_n]), flat(ab_re, ab_im))


def _decode_head_order():
    s = jnp.arange(N_HEADS)
    return (s % 2) * KV_REP + s // 2


def kernel(x_prompt, x_sample, cache_swa_k, cache_swa_v, cache_conv, state_ssm_re, state_ssm_im,
           norm_mix_g, w_in, attn_sinks, conv_dw_w, conv_dw_b, conv_ln_g, conv_ln_b,
           ssm_a_re, ssm_a_im, ssm_log_dt, ssm_b_re, ssm_b_im, ssm_c_re, ssm_c_im,
           ssm_d, ssm_glu_w, ssm_glu_b, w_out, norm_ffn_g, w_ff_gate, w_ff_up, w_ff_down,
           norm_final_g):
    n_seq, seq_len, _ = x_prompt.shape
    n_dec = x_sample.shape[0]
    win = cache_swa_k.shape[2]
    assert x_sample.shape[1] == 1 and win == WINDOW
    assert seq_len % TS_ROWS == 0 and n_dec % DEC_BLOCK == 0

    row3 = lambda v: v.reshape(DEPTH, 1, -1)
    g_mix, g_ffn = row3(norm_mix_g), row3(norm_ffn_g)
    sinks3 = row3(attn_sinks)
    order = _decode_head_order()
    sinks_dec = attn_sinks[:, order][:, :, None]
    cb3, lg3, lb3 = row3(conv_dw_b), row3(conv_ln_g), row3(conv_ln_b)
    d3, gb3 = row3(ssm_d), row3(ssm_glu_b)
    w_in_q = jnp.transpose(w_in[:, :, :Q_END].astype(BF16).reshape(DEPTH, D_MODEL, N_KV_HEADS, KV_REP, HEAD_DIM),
                           (0, 1, 3, 2, 4)).reshape(DEPTH, D_MODEL, Q_END)
    w_in_rest = w_in[:, :, Q_END:].astype(BF16)
    w_out_b = w_out.astype(BF16)
    wg_b, wu_b, wd_b = w_ff_gate.astype(BF16), w_ff_up.astype(BF16), w_ff_down.astype(BF16)
    gw_b = ssm_glu_w.astype(BF16)
    k1, m2, n4, bmat, cmat, a_chunk, a_step = jax.vmap(_ssm_operands)(
        ssm_a_re, ssm_a_im, ssm_log_dt, ssm_b_re, ssm_b_im, ssm_c_re, ssm_c_im)

    wo_heads = w_out_b[:, :ATTN_WIDTH].reshape(DEPTH, N_HEADS, HEAD_DIM, D_MODEL)[:, order]
    own = (jnp.arange(N_HEADS)[:, None] % 2) == jnp.arange(N_KV_HEADS)[None, :]
    wo_dec = jnp.where(own[None, :, :, None, None], wo_heads[:, :, None], 0).reshape(
        DEPTH, N_HEADS * LANES, D_MODEL)
    wo_attn = jnp.transpose(w_out_b[:, :ATTN_WIDTH].reshape(DEPTH, N_KV_HEADS, KV_REP, HEAD_DIM, D_MODEL),
                            (0, 2, 1, 3, 4)).reshape(DEPTH, ATTN_WIDTH, D_MODEL)
    wo_attn_conv = jnp.concatenate([wo_attn, w_out_b[:, ATTN_WIDTH:V_END]], 1)
    wo_conv = w_out_b[:, ATTN_WIDTH:V_END]
    wo_ssm = w_out_b[:, V_END:]

    ck = jnp.transpose(cache_swa_k, (0, 1, 3, 4, 2)).reshape(DEPTH, n_dec, LANES, win)
    cv = jnp.transpose(cache_swa_v, (0, 1, 3, 4, 2)).reshape(DEPTH, n_dec, LANES, win)
    cct = jnp.transpose(cache_conv, (0, 2, 1, 3))
    hr = state_ssm_re.reshape(DEPTH, n_dec, N_STATE)
    hi = state_ssm_im.reshape(DEPTH, n_dec, N_STATE)

    xp = x_prompt.reshape(n_seq * seq_len, D_MODEL)
    xs = x_sample.reshape(n_dec, D_MODEL)
    tm = TM_ROWS
    decspec = lambda width: pl.BlockSpec((n_dec, width), lambda i: (i, 0))
    scale = 1.0 / math.sqrt(HEAD_DIM)
    own_lane = ((jnp.arange(LANES) // HEAD_DIM)[None, None, None, :]
                == jnp.arange(N_KV_HEADS)[None, None, :, None])

    cache_out = lambda c: jnp.transpose(c.reshape(DEPTH, n_dec, N_KV_HEADS, HEAD_DIM, win), (0, 1, 4, 2, 3))
    k_all = jnp.zeros((DEPTH, n_dec, LANES, win), F32)
    v_all = jnp.zeros((DEPTH, n_dec, LANES, win), F32)
    kp, vp, cp, hrp, hip = [], [], [], [], []
    cs, hrs, his = [], [], []
    for l in range(DEPTH):
        final = norm_final_g.reshape(1, D_MODEL) if l == DEPTH - 1 else None

        z, u = _inproj(xp, g_mix, w_in_q, w_in_rest, l, TI_ROWS)
        ssm, h_last = _ssm_prompt(u, k1, m2, n4, a_chunk, d3, gw_b, gb3, l, n_seq, seq_len)
        mix, ctx = _mix_prompt(z, sinks3, conv_dw_w, cb3, lg3, lb3, l, n_seq, seq_len)
        xp = _tail(xp, g_ffn,
                   [(mix, pl.BlockSpec((tm, V_END), lambda i: (i, 0)), wo_attn_conv),
                    (ssm, pl.BlockSpec((2, tm, LANES), lambda i: (0, i, 0)), wo_ssm)],
                   wg_b, wu_b, wd_b, l, tm, final)
        z3 = z.reshape(n_seq, seq_len, C_END)[:, seq_len - WINDOW:]
        kp.append(z3[..., Q_END:K_END].reshape(n_seq, WINDOW, N_KV_HEADS, HEAD_DIM))
        vp.append(z3[..., K_END:V_END].reshape(n_seq, WINDOW, N_KV_HEADS, HEAD_DIM))
        cp.append(ctx)
        hrp.append(h_last[:, 0, :N_STATE].reshape(n_seq, SSM_GROUPS, SSM_STATE))
        hip.append(h_last[:, 0, N_STATE:].reshape(n_seq, SSM_GROUPS, SSM_STATE))

        zs, us = _inproj(xs, g_mix, w_in_q, w_in_rest, l, n_dec)
        zq = (zs[:, :Q_END] * scale).reshape(n_dec, KV_REP, 1, LANES)
        q3 = jnp.where(own_lane, zq, 0.0).astype(BF16).reshape(n_dec * N_HEADS, LANES)
        kn3 = zs[:, Q_END:K_END].reshape(n_dec, 1, LANES)
        vn3 = zs[:, K_END:V_END].reshape(n_dec, 1, LANES)
        o3, k_all, v_all, conv_s, nct, ssm_s, nhr, nhi = _dec_mix(
            q3, kn3, vn3, ck, cv, zs, cct, us, hr, hi, sinks_dec, conv_dw_w, cb3, lg3, lb3,
            bmat, cmat, a_step, d3, gw_b, gb3, k_all, v_all, l)
        xs = _tail(xs, g_ffn,
                   [(o3.reshape(n_dec, N_HEADS * LANES), decspec(N_HEADS * LANES), wo_dec),
                    (conv_s, decspec(CONV_CH), wo_conv),
                    (ssm_s, decspec(SSM_CH), wo_ssm)],
                   wg_b, wu_b, wd_b, l, n_dec, final)
        cs.append(jnp.transpose(nct, (1, 0, 2)))
        hrs.append(nhr.reshape(n_dec, SSM_GROUPS, SSM_STATE))
        his.append(nhi.reshape(n_dec, SSM_GROUPS, SSM_STATE))

    return (xp.reshape(n_seq, seq_len, D_MODEL), xs.reshape(n_dec, 1, D_MODEL),
            jnp.stack(kp), jnp.stack(vp), jnp.stack(cp), jnp.stack(hrp), jnp.stack(hip),
            cache_out(k_all), cache_out(v_all), jnp.stack(cs), jnp.stack(hrs), jnp.stack(his))
```

```python
import functools
import math

import jax
import jax.numpy as jnp
from jax import lax
from jax.experimental import pallas as pl
from jax.experimental.pallas import tpu as pltpu

D_MODEL = 1024
DEPTH = 4
HEAD_DIM = 64
ATTN_WIDTH = 512
N_HEADS = 8
N_KV_HEADS = 2
KV_REP = 4
WINDOW = 128
CONV_CH = 256
CONV_WIDTH = 31
SSM_CH = 256
SSM_GROUP = 16
SSM_GROUPS = 16
SSM_STATE = 64
D_FF = 2816
EPS = 1e-6

Q_END = ATTN_WIDTH
K_END = Q_END + N_KV_HEADS * HEAD_DIM
V_END = K_END + N_KV_HEADS * HEAD_DIM
C_END = V_END + 2 * CONV_CH
IN_COLS = C_END + SSM_CH

N_STATE = SSM_GROUPS * SSM_STATE
LANES = 128
SSM_CHUNK = 8
CHUNK_COLS = SSM_CHUNK * SSM_CH
SSM_HALVES = SSM_CH // LANES
HALF_COLS = CHUNK_COLS // SSM_HALVES
HALF_STATE = N_STATE // SSM_HALVES
NEG = -1e30

TM_ROWS = 512
TS_ROWS = 4096
TI_ROWS = 1024
FF_CHUNK = 256
OUT_CHUNK = 1024
DEC_BLOCK = 16
CONV_ROWS = 64
ATTN_BLOCKS = 4
VMEM_LIMIT = 56 * 1024 * 1024

F32 = jnp.float32
BF16 = jnp.bfloat16


def _params(n_axes, flags=None):
    return pltpu.CompilerParams(dimension_semantics=("arbitrary",) * n_axes,
                                vmem_limit_bytes=VMEM_LIMIT, flags=flags)


def _resident(shape, index_map):
    return pl.BlockSpec(shape, index_map, pipeline_mode=pl.Buffered(1))


def _rms(x, g):
    return x * lax.rsqrt(jnp.mean(x * x, -1, keepdims=True) + EPS) * g


def _sigmoid(x):
    return 1.0 / (1.0 + jnp.exp(-x))


def _gelu_tanh(x):
    c = math.sqrt(2.0 / math.pi)
    return 0.5 * x * (1.0 + jnp.tanh(c * (x + 0.044715 * (x * x * x))))


def _dot(a, b):
    return jnp.dot(a, b, preferred_element_type=F32)


def _inproj_body(x_ref, g_ref, wq_ref, wr_ref, q_ref, kv_ref, c_ref, u_ref, kvl_ref):
    h = _rms(x_ref[...], g_ref[...]).astype(BF16)
    q_ref[...] = _dot(h, wq_ref[...]).astype(BF16)
    zr = _dot(h, wr_ref[...])
    kv = zr[:, :V_END - Q_END]
    kv_ref[...] = kv.astype(BF16)
    kvl_ref[...] = kv[kv.shape[0] - WINDOW:, :]
    c_ref[...] = zr[:, V_END - Q_END:C_END - Q_END]
    u_ref[0] = zr[:, C_END - Q_END:C_END - Q_END + LANES]
    u_ref[1] = zr[:, C_END - Q_END + LANES:]


def _inproj(x, g3, w_q, w_rest, layer, tm, n_seq):
    rows = x.shape[0]
    tiles_per_seq = rows // n_seq // tm
    kv_w = V_END - Q_END
    return pl.pallas_call(
        _inproj_body,
        grid=(rows // tm,),
        in_specs=[
            pl.BlockSpec((tm, D_MODEL), lambda i: (i, 0)),
            pl.BlockSpec((None, 1, D_MODEL), lambda i: (layer, 0, 0)),
            _resident((None, D_MODEL, Q_END), lambda i: (layer, 0, 0)),
            _resident((None, D_MODEL, IN_COLS - Q_END), lambda i: (layer, 0, 0)),
        ],
        out_specs=[
            pl.BlockSpec((tm, Q_END), lambda i: (i, 0)),
            pl.BlockSpec((tm, kv_w), lambda i: (i, 0)),
            pl.BlockSpec((tm, 2 * CONV_CH), lambda i: (i, 0)),
            pl.BlockSpec((2, tm, LANES), lambda i: (0, i, 0)),
            pl.BlockSpec((None, WINDOW, kv_w), lambda i: (i // tiles_per_seq, 0, 0)),
        ],
        out_shape=[jax.ShapeDtypeStruct((rows, Q_END), BF16),
                   jax.ShapeDtypeStruct((rows, kv_w), BF16),
                   jax.ShapeDtypeStruct((rows, 2 * CONV_CH), F32),
                   jax.ShapeDtypeStruct((2, rows, LANES), F32),
                   jax.ShapeDtypeStruct((n_seq, WINDOW, kv_w), F32)],
        compiler_params=_params(1),
        name="inproj",
    )(x, g3, w_q, w_rest)


def _tail_steps(x, acts, g_ref, wg_ref, wu_ref, wd_ref, gf_ref, o_ref, x1_s, hf_s, act_s):
    x1 = x
    for act, w_ref in acts:
        x1 = x1 + _dot(act, w_ref[...])
    hf_s[...] = _rms(x1, g_ref[...]).astype(BF16)
    x1_s[...] = x1
    yield
    for c in range(0, D_FF, FF_CHUNK):
        gate = _dot(hf_s[...], wg_ref[:, c:c + FF_CHUNK])
        up = _dot(hf_s[...], wu_ref[:, c:c + FF_CHUNK])
        act_s[:, c:c + FF_CHUNK] = (gate * _sigmoid(gate) * up).astype(BF16)
        yield
    for n in range(0, D_MODEL, OUT_CHUNK):
        o_ref[:, n:n + OUT_CHUNK] = x1_s[:, n:n + OUT_CHUNK] + _dot(act_s[...], wd_ref[:, n:n + OUT_CHUNK])
        yield
    if gf_ref is not None:
        o_ref[...] = _rms(o_ref[...], gf_ref[...])


def _tail_body(n_parts, final, *refs):
    x_ref, g_ref = refs[0], refs[1]
    parts = refs[2:2 + 2 * n_parts]
    wg_ref, wu_ref, wd_ref = refs[2 + 2 * n_parts:5 + 2 * n_parts]
    gf_ref = refs[5 + 2 * n_parts] if final else None
    o_ref, x1_s, hf_s, act_s = refs[-4:]
    acts = []
    for p in range(n_parts):
        act_ref = parts[2 * p]
        if len(act_ref.shape) == 3:
            act = jnp.concatenate([act_ref[i] for i in range(act_ref.shape[0])], -1)
        else:
            act = act_ref[...]
        acts.append((act.astype(BF16), parts[2 * p + 1]))
    for _ in _tail_steps(x_ref[...], acts, g_ref, wg_ref, wu_ref, wd_ref, gf_ref, o_ref, x1_s, hf_s, act_s):
        pass


def _tail(x, g3, parts, wg, wu, wd, layer, tm, final_g=None):
    rows = x.shape[0]
    final = final_g is not None
    in_specs = [pl.BlockSpec((tm, D_MODEL), lambda i: (i, 0)),
                pl.BlockSpec((None, 1, D_MODEL), lambda i: (layer, 0, 0))]
    args = [x, g3]
    for act, spec, w in parts:
        in_specs.append(spec)
        in_specs.append(pl.BlockSpec((None,) + w.shape[1:], lambda i: (layer, 0, 0)))
        args += [act, w]
    in_specs += [
        _resident((None, D_MODEL, D_FF), lambda i: (layer, 0, 0)),
        _resident((None, D_MODEL, D_FF), lambda i: (layer, 0, 0)),
        _resident((None, D_FF, D_MODEL), lambda i: (layer, 0, 0)),
    ]
    args += [wg, wu, wd]
    if final:
        in_specs.append(pl.BlockSpec((1, D_MODEL), lambda i: (0, 0)))
        args.append(final_g)
    return pl.pallas_call(
        functools.partial(_tail_body, len(parts), final),
        grid=(rows // tm,),
        in_specs=in_specs,
        out_specs=pl.BlockSpec((tm, D_MODEL), lambda i: (i, 0)),
        out_shape=jax.ShapeDtypeStruct((rows, D_MODEL), F32),
        scratch_shapes=[pltpu.VMEM((tm, D_MODEL), F32), pltpu.VMEM((tm, D_MODEL), BF16),
                        pltpu.VMEM((tm, D_FF), BF16)],
        compiler_params=_params(1),
        name="tail",
    )(*args)


def _expand_ssm_operands(k1_ref, m2_ref, n4_ref, w1_ref, w2_ref, w4_ref):
    compact = SSM_CHUNK * SSM_GROUP
    col = lax.broadcasted_iota(jnp.int32, (SSM_GROUP, HALF_COLS), 1)
    col_chan_group = (col % LANES) // SSM_GROUP
    col_state_group = (col % HALF_STATE) // SSM_STATE
    e_row = lax.broadcasted_iota(jnp.int32, (compact, HALF_COLS), 0)
    e_col = lax.broadcasted_iota(jnp.int32, (compact, HALF_COLS), 1)
    spread = ((e_row // SSM_GROUP == e_col // LANES) & (e_row % SSM_GROUP == e_col % SSM_GROUP))
    spread = jnp.where(spread, 1.0, 0.0).astype(BF16)
    rows = lax.broadcasted_iota(jnp.int32, (LANES, HALF_COLS), 0)
    cols = lax.broadcasted_iota(jnp.int32, (LANES, HALF_COLS), 1)
    for o in range(SSM_HALVES):
        for l in range(SSM_CHUNK):
            k1 = k1_ref[o, l * SSM_GROUP:(l + 1) * SSM_GROUP, :]
            m2 = m2_ref[o, l * SSM_GROUP:(l + 1) * SSM_GROUP, :]
            for g in range(LANES // SSM_GROUP):
                r0 = l * LANES + g * SSM_GROUP
                w1_ref[o, r0:r0 + SSM_GROUP, :] = jnp.where(col_chan_group == g, k1, 0.0).astype(BF16)
                w2_ref[o, r0:r0 + SSM_GROUP, :] = jnp.where(col_state_group == g, m2, 0.0).astype(BF16)
        for b in range(2 * HALF_STATE // LANES):
            full = _dot(n4_ref[o, b * LANES:(b + 1) * LANES, :].astype(BF16), spread)
            row_group = ((rows + b * LANES) % HALF_STATE) // SSM_STATE
            keep = row_group == (cols % LANES) // SSM_GROUP
            w4_ref[o, b * LANES:(b + 1) * LANES, :] = jnp.where(keep, full, 0.0).astype(BF16)


def _ssm_body(n_chunks, u_ref, k1_ref, m2_ref, n4_ref, al_ref, d_ref, gw_ref, gb_ref,
              o_ref, hl_ref, x_s, gh_s, hc_s, w1_ref, w2_ref, w4_ref):
    @pl.when((pl.program_id(0) == 0) & (pl.program_id(1) == 0))
    def _():
        _expand_ssm_operands(k1_ref, m2_ref, n4_ref, w1_ref, w2_ref, w4_ref)

    @pl.when(pl.program_id(1) == 0)
    def _():
        hc_s[...] = jnp.zeros_like(hc_s)

    def steps(l, o):
        return u_ref[o, pl.ds(l, n_chunks, stride=SSM_CHUNK), :]

    for o in range(SSM_HALVES):
        for l in range(SSM_CHUNK):
            c0 = o * HALF_COLS + l * LANES
            x_s[:, c0:c0 + LANES] = steps(l, o).astype(BF16)

    for o in range(SSM_HALVES):
        g = _dot(x_s[:, o * HALF_COLS:(o + 1) * HALF_COLS], w2_ref[o])
        gh_s[:, o * HALF_STATE:(o + 1) * HALF_STATE] = g[:, :HALF_STATE]
        gh_s[:, N_STATE + o * HALF_STATE:N_STATE + (o + 1) * HALF_STATE] = g[:, HALF_STATE:]

    a_re = al_ref[:, :N_STATE]
    a_im = al_ref[:, N_STATE:]

    def step(k, carry):
        h_re, h_im = carry
        g_re = gh_s[pl.ds(k, 1), :N_STATE]
        g_im = gh_s[pl.ds(k, 1), N_STATE:]
        gh_s[pl.ds(k, 1), :N_STATE] = h_re
        gh_s[pl.ds(k, 1), N_STATE:] = h_im
        return (a_re * h_re - a_im * h_im + g_re, a_re * h_im + a_im * h_re + g_im)

    h_re, h_im = lax.fori_loop(0, n_chunks, step, (hc_s[:, :N_STATE], hc_s[:, N_STATE:]))
    hc_s[:, :N_STATE] = h_re
    hc_s[:, N_STATE:] = h_im
    hl_ref[:, :N_STATE] = h_re
    hl_ref[:, N_STATE:] = h_im

    hb = gh_s[...].astype(BF16)
    hcat = [jnp.concatenate([hb[:, o * HALF_STATE:(o + 1) * HALF_STATE],
                             hb[:, N_STATE + o * HALF_STATE:N_STATE + (o + 1) * HALF_STATE]], -1)
            for o in range(SSM_HALVES)]
    pair = 2 * LANES
    for l0 in range(0, SSM_CHUNK, 2):
        c0, k1 = l0 * LANES, (l0 + 2) * LANES
        ys = [_dot(x_s[:, o * HALF_COLS:o * HALF_COLS + k1], w1_ref[o, :k1, c0:c0 + pair])
              + _dot(hcat[o], w4_ref[o, :, c0:c0 + pair]) for o in range(SSM_HALVES)]
        for dl in range(2):
            l = l0 + dl
            y = jnp.concatenate([ys[o][:, dl * LANES:(dl + 1) * LANES] for o in range(SSM_HALVES)], -1)
            y = _gelu_tanh(y + d_ref[...] * jnp.concatenate([steps(l, o) for o in range(SSM_HALVES)], -1))
            gate = _dot(y.astype(BF16), gw_ref[...]) + gb_ref[...]
            out = y * _sigmoid(gate)
            for o in range(SSM_HALVES):
                o_ref[o, pl.ds(l, n_chunks, stride=SSM_CHUNK), :] = out[:, o * LANES:(o + 1) * LANES]


def _ssm_prompt(u, k1, m2, n4, al, d3, gw, gb3, layer, n_seq, seq_len):
    ts = TS_ROWS
    n_tiles = seq_len // ts
    n_chunks = ts // SSM_CHUNK
    rows = u.shape[1]
    compact = SSM_CHUNK * SSM_GROUP
    return pl.pallas_call(
        functools.partial(_ssm_body, n_chunks),
        grid=(n_seq, n_tiles),
        in_specs=[
            pl.BlockSpec((SSM_HALVES, ts, LANES), lambda s, i: (0, s * n_tiles + i, 0)),
            pl.BlockSpec((None, SSM_HALVES, compact, HALF_COLS), lambda s, i: (layer, 0, 0, 0)),
            pl.BlockSpec((None, SSM_HALVES, compact, 2 * HALF_STATE), lambda s, i: (layer, 0, 0, 0)),
            pl.BlockSpec((None, SSM_HALVES, 2 * HALF_STATE, compact), lambda s, i: (layer, 0, 0, 0)),
            pl.BlockSpec((None, 1, 2 * N_STATE), lambda s, i: (layer, 0, 0)),
            pl.BlockSpec((None, 1, SSM_CH), lambda s, i: (layer, 0, 0)),
            pl.BlockSpec((None, SSM_CH, SSM_CH), lambda s, i: (layer, 0, 0)),
            pl.BlockSpec((None, 1, SSM_CH), lambda s, i: (layer, 0, 0)),
        ],
        out_specs=[
            pl.BlockSpec((SSM_HALVES, ts, LANES), lambda s, i: (0, s * n_tiles + i, 0)),
            pl.BlockSpec((None, 1, 2 * N_STATE), lambda s, i: (s, 0, 0)),
        ],
        out_shape=[jax.ShapeDtypeStruct((SSM_HALVES, rows, LANES), F32),
                   jax.ShapeDtypeStruct((n_seq, 1, 2 * N_STATE), F32)],
        scratch_shapes=[
            pltpu.VMEM((n_chunks, CHUNK_COLS), BF16),
            pltpu.VMEM((n_chunks, 2 * N_STATE), F32),
            pltpu.VMEM((1, 2 * N_STATE), F32),
            pltpu.VMEM((SSM_HALVES, HALF_COLS, HALF_COLS), BF16),
            pltpu.VMEM((SSM_HALVES, HALF_COLS, 2 * HALF_STATE), BF16),
            pltpu.VMEM((SSM_HALVES, 2 * HALF_STATE, HALF_COLS), BF16),
        ],
        compiler_params=_params(2),
        name="ssm_prompt",
    )(u, k1, m2, n4, al, d3, gw, gb3)


def _mix_steps(tm, first_tile, q_ref, kv_ref, a_ref, gg_ref, sink_ref, cw_ref, cb_ref, lg_ref, lb_ref,
                 o_ref, ctx_ref, kv_s, u_s, ush_s, bias_s):
    pad = 32
    off = pad - (CONV_WIDTH - 1)
    sub = 8

    @pl.when(first_tile)
    def _():
        kv_s[0:WINDOW, :] = jnp.zeros((WINDOW, 2 * LANES), F32)
        u_s[0:pad, :] = jnp.zeros((pad, CONV_CH), F32)
        qi = lax.broadcasted_iota(jnp.int32, (WINDOW, 2 * WINDOW), 0)
        ki = lax.broadcasted_iota(jnp.int32, (WINDOW, 2 * WINDOW), 1)
        dist = qi - ki + WINDOW
        valid = (dist >= 0) & (dist < WINDOW)
        distf = dist.astype(F32)
        for g in range(N_KV_HEADS):
            for r in range(KV_REP):
                slope = 2.0 ** (-8.0 * (g * KV_REP + r + 1) / N_HEADS)
                bias_s[g, r * WINDOW:(r + 1) * WINDOW, :] = jnp.where(valid, -slope * distf, NEG)

    kv_s[WINDOW:, :] = kv_ref[...].astype(F32)
    u_s[pad:, :] = a_ref[...] * _sigmoid(gg_ref[...])
    for b in range(1, sub):
        ush_s[b - 1] = u_s[b:b + tm + pad - sub, :]
    ctx_ref[...] = u_s[tm + off:tm + pad, :]
    yield

    def conv_units():
        for r0 in range(0, tm, CONV_ROWS):
            acc = jnp.zeros((CONV_ROWS, CONV_CH), F32) + cb_ref[...]
            for j in range(CONV_WIDTH):
                a0, b = divmod(off + j, sub)
                lo = r0 + a0 * sub
                rows = u_s[lo:lo + CONV_ROWS, :] if b == 0 else ush_s[b - 1, lo:lo + CONV_ROWS, :]
                acc = acc + cw_ref[j:j + 1, :] * rows
            mu = jnp.mean(acc, -1, keepdims=True)
            cen = acc - mu
            var = jnp.mean(cen * cen, -1, keepdims=True)
            yn = cen * lax.rsqrt(var + EPS) * lg_ref[...] + lb_ref[...]
            o_ref[r0:r0 + CONV_ROWS, ATTN_WIDTH:] = (yn * _sigmoid(yn)).astype(BF16)
            yield

    scale = 1.0 / math.sqrt(HEAD_DIM)
    k_lane_group = lax.broadcasted_iota(jnp.int32, (2 * WINDOW, LANES), 1) // HEAD_DIM
    o_lane_group = lax.broadcasted_iota(jnp.int32, (KV_REP * WINDOW, LANES), 1) // HEAD_DIM
    key_in_prev = lax.broadcasted_iota(jnp.int32, (1, 2 * WINDOW), 1) < WINDOW
    no_prev_block = jnp.logical_and(key_in_prev, first_tile)
    sinks = [jnp.concatenate([jnp.broadcast_to(sink_ref[:, g * KV_REP + r:g * KV_REP + r + 1], (WINDOW, 1))
                              for r in range(KV_REP)], 0) for g in range(N_KV_HEADS)]

    def attention_units():
        n_blocks = tm // WINDOW
        for b0 in range(0, n_blocks, ATTN_BLOCKS):
            blocks = range(b0, min(b0 + ATTN_BLOCKS, n_blocks))
            pairs = [(blk, g) for blk in blocks for g in range(N_KV_HEADS)]
            kblk = {blk: kv_s[blk * WINDOW:(blk + 2) * WINDOW, 0:LANES] for blk in blocks}
            vblk = {blk: kv_s[blk * WINDOW:(blk + 2) * WINDOW, LANES:].astype(BF16) for blk in blocks}
            qs = {blk: (jnp.concatenate([q_ref[blk * WINDOW:(blk + 1) * WINDOW, r * LANES:(r + 1) * LANES]
                                         for r in range(KV_REP)], 0) * jnp.asarray(scale, BF16))
                  for blk in blocks}
            kg = {(blk, g): jnp.where(k_lane_group == g, kblk[blk], 0.0).astype(BF16) for blk, g in pairs}
            s = {(blk, g): lax.dot_general(qs[blk], kg[blk, g], (((1,), (1,)), ((), ())),
                                           preferred_element_type=F32) + bias_s[g] for blk, g in pairs}
            for g in range(N_KV_HEADS):
                if b0 == 0:
                    s[0, g] = jnp.where(no_prev_block, NEG, s[0, g])
            m = {k: jnp.maximum(jnp.max(s[k], -1, keepdims=True), sinks[k[1]]) for k in pairs}
            p = {k: jnp.exp(s[k] - m[k]) for k in pairs}
            denom = {k: jnp.sum(p[k], -1, keepdims=True) + jnp.exp(sinks[k[1]] - m[k]) for k in pairs}
            og = {k: _dot(p[k].astype(BF16), vblk[k[0]]) / denom[k] for k in pairs}
            for blk in blocks:
                o = jnp.where(o_lane_group == 0, og[blk, 0], og[blk, 1])
                for r in range(KV_REP):
                    o_ref[blk * WINDOW:(blk + 1) * WINDOW, r * LANES:(r + 1) * LANES] = (
                        o[r * WINDOW:(r + 1) * WINDOW, :].astype(BF16))
            yield

    yield from conv_units()
    yield from attention_units()

    kv_s[0:WINDOW, :] = kv_s[tm:tm + WINDOW, :]
    u_s[0:pad, :] = u_s[tm:tm + pad, :]


def _mix_body(tm, *refs):
    for _ in _mix_steps(tm, pl.program_id(1) == 0, *refs):
        pass


def _mix_prompt(zq, zkv, zc, sinks3, cw, cb3, lg3, lb3, layer, n_seq, seq_len):
    tm = TM_ROWS
    n_tiles = seq_len // tm
    rows = n_seq * seq_len
    row = lambda s, i: s * n_tiles + i
    vec = lambda width: pl.BlockSpec((None, 1, width), lambda s, i: (layer, 0, 0))
    return pl.pallas_call(
        functools.partial(_mix_body, tm),
        grid=(n_seq, n_tiles),
        in_specs=[
            pl.BlockSpec((tm, ATTN_WIDTH), lambda s, i: (row(s, i), 0)),
            pl.BlockSpec((tm, 2 * LANES), lambda s, i: (row(s, i), 0)),
            pl.BlockSpec((tm, CONV_CH), lambda s, i: (row(s, i), 0)),
            pl.BlockSpec((tm, CONV_CH), lambda s, i: (row(s, i), 1)),
            vec(N_HEADS),
            pl.BlockSpec((None, CONV_WIDTH, CONV_CH), lambda s, i: (layer, 0, 0)),
            vec(CONV_CH), vec(CONV_CH), vec(CONV_CH),
        ],
        out_specs=[
            pl.BlockSpec((tm, V_END), lambda s, i: (row(s, i), 0)),
            pl.BlockSpec((None, CONV_WIDTH - 1, CONV_CH), lambda s, i: (s, 0, 0)),
        ],
        out_shape=[jax.ShapeDtypeStruct((rows, V_END), BF16),
                   jax.ShapeDtypeStruct((n_seq, CONV_WIDTH - 1, CONV_CH), F32)],
        scratch_shapes=[pltpu.VMEM((tm + WINDOW, 2 * LANES), F32),
                        pltpu.VMEM((tm + 32, CONV_CH), F32),
                        pltpu.VMEM((7, tm + 24, CONV_CH), F32),
                        pltpu.VMEM((N_KV_HEADS, KV_REP * WINDOW, 2 * WINDOW), F32)],
        compiler_params=_params(2),
        name="mix_prompt",
    )(zq, zkv, zc, zc, sinks3, cw, cb3, lg3, lb3)


def _dec_body(q_ref, kn_ref, vn_ref, k2_ref, v2_ref, ck_ref, cv_ref, a_ref, gg_ref, cc_ref, u_ref, hr_ref, hi_ref,
              sink_ref, cw_ref, cb_ref, lg_ref, lb_ref, bm_ref, cm_ref, ab_ref, d_ref, gw_ref, gb_ref,
              kall_ref, vall_ref, o_ref, ok_ref, ov_ref, co_ref, oc_ref, so_ref, or_ref, oi_ref):
    nb = DEC_BLOCK
    win = ck_ref.shape[2]

    q3 = q_ref[...].reshape(nb, N_HEADS, LANES)
    kn = kn_ref[...]
    vn = vn_ref[...]
    ck = ck_ref[...]
    cv = cv_ref[...]
    s = jnp.einsum("nsc,ncj->nsj", q3, ck.astype(BF16), preferred_element_type=F32)
    si = lax.broadcasted_iota(jnp.int32, (N_HEADS, win), 0)
    ji = lax.broadcasted_iota(jnp.int32, (N_HEADS, win), 1)
    head = (si % 2) * KV_REP + si // 2
    slope = jnp.zeros((N_HEADS, win), F32)
    for h in range(N_HEADS):
        slope = jnp.where(head == h, 2.0 ** (-8.0 * (h + 1) / N_HEADS), slope)
    dist = win - ji
    bias = jnp.where(dist < WINDOW, -slope * dist.astype(F32), NEG)
    s = s + bias[None]
    s_new = jnp.sum(q3.astype(F32) * kn.astype(BF16).astype(F32), -1, keepdims=True)
    sink = sink_ref[...][None]
    m = jnp.maximum(jnp.maximum(jnp.max(s, -1, keepdims=True), s_new), sink)
    p = jnp.exp(s - m)
    p_new = jnp.exp(s_new - m)
    denom = jnp.sum(p, -1, keepdims=True) + p_new + jnp.exp(sink - m)
    o = jnp.einsum("nsj,ncj->nsc", p.astype(BF16), cv.astype(BF16), preferred_element_type=F32)
    o = o + p_new.astype(BF16).astype(F32) * vn.astype(BF16).astype(F32)
    o_ref[...] = (o / denom).reshape(nb * N_HEADS, LANES)

    last = lax.broadcasted_iota(jnp.int32, (LANES, win), 1) == win - 1
    fill = jnp.zeros((LANES - nb, LANES), F32)
    knt = jnp.concatenate([k2_ref[...], fill], 0).T
    vnt = jnp.concatenate([v2_ref[...], fill], 0).T
    for i in range(nb):
        ok_ref[i] = jnp.where(last, pltpu.roll(knt, win - 1 - i, 1), pltpu.roll(ck[i], win - 1, 1))
        ov_ref[i] = jnp.where(last, pltpu.roll(vnt, win - 1 - i, 1), pltpu.roll(cv[i], win - 1, 1))

    @pl.when(pl.program_id(0) == 0)
    def _():
        u = a_ref[...] * _sigmoid(gg_ref[...])
        acc = cb_ref[...] + cw_ref[CONV_WIDTH - 1:CONV_WIDTH, :] * u
        for j in range(CONV_WIDTH - 1):
            acc = acc + cw_ref[j:j + 1, :] * cc_ref[j]
        mu = jnp.mean(acc, -1, keepdims=True)
        cen = acc - mu
        var = jnp.mean(cen * cen, -1, keepdims=True)
        yn = cen * lax.rsqrt(var + EPS) * lg_ref[...] + lb_ref[...]
        co_ref[...] = yn * _sigmoid(yn)
        for j in range(CONV_WIDTH - 2):
            oc_ref[j] = cc_ref[j + 1]
        oc_ref[CONV_WIDTH - 2] = u

        us = jnp.concatenate([u_ref[0], u_ref[1]], -1)
        bu = _dot(us.astype(BF16), bm_ref[...])
        a_re = ab_ref[:, :N_STATE]
        a_im = ab_ref[:, N_STATE:]
        h_re = hr_ref[...]
        h_im = hi_ref[...]
        n_re = a_re * h_re - a_im * h_im + bu[:, :N_STATE]
        n_im = a_re * h_im + a_im * h_re + bu[:, N_STATE:]
        or_ref[...] = n_re
        oi_ref[...] = n_im
        hcat = jnp.concatenate([n_re, n_im], -1).astype(BF16)
        y = _gelu_tanh(_dot(hcat, cm_ref[...]) + d_ref[...] * us)
        gate = _dot(y.astype(BF16), gw_ref[...]) + gb_ref[...]
        so_ref[...] = y * _sigmoid(gate)


def _dec_mix(q3, kn3, vn3, kvn, ck, cv, zc, cct, u, hr, hi, sinks3, cw, cb3, lg3, lb3,
             bmat, cmat, ab3, d3, gw, gb3, k_all, v_all, layer):
    n = kn3.shape[0]
    win = ck.shape[3]
    nb = DEC_BLOCK
    vec = lambda width: pl.BlockSpec((None, 1, width), lambda i: (layer, 0, 0))
    mat = lambda r, c: pl.BlockSpec((None, r, c), lambda i: (layer, 0, 0))
    cache = pl.BlockSpec((None, nb, LANES, win), lambda i: (layer, i, 0, 0))
    ctx = pl.BlockSpec((None, CONV_WIDTH - 1, n, CONV_CH), lambda i: (layer, 0, 0, 0))
    state = pl.BlockSpec((None, n, N_STATE), lambda i: (layer, 0, 0))
    allrows = lambda width: pl.BlockSpec((n, width), lambda i: (0, 0))
    new3 = pl.BlockSpec((nb, 1, LANES), lambda i: (i, 0, 0))
    return pl.pallas_call(
        _dec_body,
        grid=(n // nb,),
        in_specs=[
            pl.BlockSpec((nb * N_HEADS, LANES), lambda i: (i, 0)), new3, new3,
            pl.BlockSpec((nb, LANES), lambda i: (i, 0)),
            pl.BlockSpec((nb, LANES), lambda i: (i, 1)),
            cache, cache,
            pl.BlockSpec((n, CONV_CH), lambda i: (0, 0)),
            pl.BlockSpec((n, CONV_CH), lambda i: (0, 1)),
            ctx,
            pl.BlockSpec((2, n, LANES), lambda i: (0, 0, 0)),
            state, state,
            mat(N_HEADS, 1), mat(CONV_WIDTH, CONV_CH), vec(CONV_CH), vec(CONV_CH), vec(CONV_CH),
            mat(SSM_CH, 2 * N_STATE), mat(2 * N_STATE, SSM_CH), vec(2 * N_STATE), vec(SSM_CH),
            mat(SSM_CH, SSM_CH), vec(SSM_CH),
            pl.BlockSpec(memory_space=pl.ANY), pl.BlockSpec(memory_space=pl.ANY),
        ],
        out_specs=[
            pl.BlockSpec((nb * N_HEADS, LANES), lambda i: (i, 0)),
            cache, cache,
            allrows(CONV_CH),
            pl.BlockSpec((CONV_WIDTH - 1, n, CONV_CH), lambda i: (0, 0, 0)),
            allrows(SSM_CH), allrows(N_STATE), allrows(N_STATE),
        ],
        out_shape=[
            jax.ShapeDtypeStruct((n * N_HEADS, LANES), F32),
            jax.ShapeDtypeStruct(k_all.shape, F32),
            jax.ShapeDtypeStruct(v_all.shape, F32),
            jax.ShapeDtypeStruct((n, CONV_CH), F32),
            jax.ShapeDtypeStruct((CONV_WIDTH - 1, n, CONV_CH), F32),
            jax.ShapeDtypeStruct((n, SSM_CH), F32),
            jax.ShapeDtypeStruct((n, N_STATE), F32),
            jax.ShapeDtypeStruct((n, N_STATE), F32),
        ],
        input_output_aliases={24: 1, 25: 2},
        compiler_params=_params(1),
        name="dec_mix",
    )(q3, kn3, vn3, kvn, kvn, ck, cv, zc, zc, cct, u, hr, hi, sinks3, cw, cb3, lg3, lb3,
      bmat, cmat, ab3, d3, gw, gb3, k_all, v_all)


def _ssm_operands(a_re, a_im, log_dt, b_re, b_im, c_re, c_im):
    hi = lax.Precision.HIGHEST
    l_n, g_n, p_n, c_n = SSM_CHUNK, SSM_GROUPS, SSM_STATE, SSM_GROUP
    dt = jnp.exp(log_dt)[:, None]
    lam_re, lam_im = a_re * dt, a_im * dt
    steps = jnp.arange(l_n + 1, dtype=F32)[:, None, None]
    mag = jnp.exp(steps * lam_re)
    pw_re, pw_im = mag * jnp.cos(steps * lam_im), mag * jnp.sin(steps * lam_im)
    ab_re, ab_im = pw_re[1], pw_im[1]
    den = a_re * a_re + a_im * a_im
    q_re = ((ab_re - 1.0) * a_re + ab_im * a_im) / den
    q_im = (ab_im * a_re - (ab_re - 1.0) * a_im) / den
    bt_re, bt_im = jnp.transpose(b_re, (2, 0, 1)), jnp.transpose(b_im, (2, 0, 1))
    bb_re = q_re * bt_re - q_im * bt_im
    bb_im = q_re * bt_im + q_im * bt_re
    pl_re, pl_im = pw_re[:l_n, None], pw_im[:l_n, None]
    pb_re = pl_re * bb_re - pl_im * bb_im
    pb_im = pl_re * bb_im + pl_im * bb_re

    ktau = (jnp.einsum("gop,tcgp->tcgo", c_re, pb_re, precision=hi)
            - jnp.einsum("gop,tcgp->tcgo", c_im, pb_im, precision=hi)).reshape(l_n, c_n, SSM_CH)
    lag = jnp.arange(l_n)[None, :] - jnp.arange(l_n)[:, None]
    k1 = jnp.where((lag >= 0)[:, :, None, None], jnp.take(ktau, jnp.maximum(lag, 0), axis=0), 0.0)
    k1 = jnp.transpose(k1, (0, 2, 1, 3)).reshape(l_n * c_n, CHUNK_COLS)
    back = l_n - 1 - jnp.arange(l_n)
    m2 = jnp.concatenate([jnp.take(pb_re, back, axis=0).reshape(l_n, c_n, N_STATE),
                          jnp.take(pb_im, back, axis=0).reshape(l_n, c_n, N_STATE)], -1)
    m2 = m2.reshape(l_n * c_n, 2 * N_STATE)
    ct_re, ct_im = jnp.transpose(c_re, (0, 2, 1)), jnp.transpose(c_im, (0, 2, 1))
    pn_re = jnp.transpose(pw_re[1:], (1, 2, 0))[..., None]
    pn_im = jnp.transpose(pw_im[1:], (1, 2, 0))[..., None]
    n4_re = (ct_re[:, :, None, :] * pn_re - ct_im[:, :, None, :] * pn_im).reshape(N_STATE, l_n * c_n)
    n4_im = (ct_re[:, :, None, :] * pn_im + ct_im[:, :, None, :] * pn_re).reshape(N_STATE, l_n * c_n)
    n4 = jnp.concatenate([n4_re, -n4_im], 0)

    chan_group = jnp.arange(SSM_CH) // c_n
    state_group = (jnp.arange(2 * N_STATE) % N_STATE) // p_n
    bmat = jnp.where(chan_group[:, None] == state_group[None, :],
                     jnp.tile(m2[(l_n - 1) * c_n:], (g_n, 1)), 0.0)
    cc = jnp.concatenate([ct_re.reshape(N_STATE, c_n), -ct_im.reshape(N_STATE, c_n)], 0)
    cmat = jnp.where(state_group[:, None] == chan_group[None, :], jnp.tile(cc, (1, g_n)), 0.0)

    k1_h = jnp.transpose(k1.reshape(l_n * c_n, l_n, SSM_HALVES, LANES), (2, 0, 1, 3)).reshape(
        SSM_HALVES, l_n * c_n, HALF_COLS)
    m2_h = jnp.transpose(m2.reshape(l_n * c_n, 2, SSM_HALVES, HALF_STATE), (2, 0, 1, 3)).reshape(
        SSM_HALVES, l_n * c_n, 2 * HALF_STATE)
    n4_h = jnp.transpose(n4.reshape(2, SSM_HALVES, HALF_STATE, l_n * c_n), (1, 0, 2, 3)).reshape(
        SSM_HALVES, 2 * HALF_STATE, l_n * c_n)

    flat = lambda re, im: jnp.concatenate([re.reshape(1, N_STATE), im.reshape(1, N_STATE)], -1)
    return (k1_h, m2_h, n4_h, bmat.astype(BF16), cmat.astype(BF16),
            flat(pw_re[l_n], pw_im[l_n]), flat(ab_re, ab_im))


def _decode_head_order():
    s = jnp.arange(N_HEADS)
    return (s % 2) * KV_REP + s // 2


def kernel(x_prompt, x_sample, cache_swa_k, cache_swa_v, cache_conv, state_ssm_re, state_ssm_im,
           norm_mix_g, w_in, attn_sinks, conv_dw_w, conv_dw_b, conv_ln_g, conv_ln_b,
           ssm_a_re, ssm_a_im, ssm_log_dt, ssm_b_re, ssm_b_im, ssm_c_re, ssm_c_im,
           ssm_d, ssm_glu_w, ssm_glu_b, w_out, norm_ffn_g, w_ff_gate, w_ff_up, w_ff_down,
           norm_final_g):
    n_seq, seq_len, _ = x_prompt.shape
    n_dec = x_sample.shape[0]
    win = cache_swa_k.shape[2]
    assert x_sample.shape[1] == 1 and win == WINDOW
    assert n_dec == WINDOW
    assert seq_len % TS_ROWS == 0 and n_dec % DEC_BLOCK == 0

    row3 = lambda v: v.reshape(DEPTH, 1, -1)
    g_mix, g_ffn = row3(norm_mix_g), row3(norm_ffn_g)
    sinks3 = row3(attn_sinks)
    order = _decode_head_order()
    sinks_dec = attn_sinks[:, order][:, :, None]
    cb3, lg3, lb3 = row3(conv_dw_b), row3(conv_ln_g), row3(conv_ln_b)
    d3, gb3 = row3(ssm_d), row3(ssm_glu_b)
    w_in_q = jnp.transpose(w_in[:, :, :Q_END].astype(BF16).reshape(DEPTH, D_MODEL, N_KV_HEADS, KV_REP, HEAD_DIM),
                           (0, 1, 3, 2, 4)).reshape(DEPTH, D_MODEL, Q_END)
    w_in_rest = w_in[:, :, Q_END:].astype(BF16)
    w_out_b = w_out.astype(BF16)
    wg_b, wu_b, wd_b = w_ff_gate.astype(BF16), w_ff_up.astype(BF16), w_ff_down.astype(BF16)
    gw_b = ssm_glu_w.astype(BF16)
    k1, m2, n4, bmat, cmat, a_chunk, a_step = jax.vmap(_ssm_operands)(
        ssm_a_re, ssm_a_im, ssm_log_dt, ssm_b_re, ssm_b_im, ssm_c_re, ssm_c_im)

    wo_heads = w_out_b[:, :ATTN_WIDTH].reshape(DEPTH, N_HEADS, HEAD_DIM, D_MODEL)[:, order]
    own = (jnp.arange(N_HEADS)[:, None] % 2) == jnp.arange(N_KV_HEADS)[None, :]
    wo_dec = jnp.where(own[None, :, :, None, None], wo_heads[:, :, None], 0).reshape(
        DEPTH, N_HEADS * LANES, D_MODEL)
    wo_attn = jnp.transpose(w_out_b[:, :ATTN_WIDTH].reshape(DEPTH, N_KV_HEADS, KV_REP, HEAD_DIM, D_MODEL),
                            (0, 2, 1, 3, 4)).reshape(DEPTH, ATTN_WIDTH, D_MODEL)
    wo_attn_conv = jnp.concatenate([wo_attn, w_out_b[:, ATTN_WIDTH:V_END]], 1)
    wo_conv = w_out_b[:, ATTN_WIDTH:V_END]
    wo_ssm = w_out_b[:, V_END:]

    ck = jnp.transpose(cache_swa_k, (0, 1, 3, 4, 2)).reshape(DEPTH, n_dec, LANES, win)
    cv = jnp.transpose(cache_swa_v, (0, 1, 3, 4, 2)).reshape(DEPTH, n_dec, LANES, win)
    cct = jnp.transpose(cache_conv, (0, 2, 1, 3))
    hr = state_ssm_re.reshape(DEPTH, n_dec, N_STATE)
    hi = state_ssm_im.reshape(DEPTH, n_dec, N_STATE)

    xp = x_prompt.reshape(n_seq * seq_len, D_MODEL)
    xs = x_sample.reshape(n_dec, D_MODEL)
    tm = TM_ROWS
    decspec = lambda width: pl.BlockSpec((n_dec, width), lambda i: (i, 0))
    scale = 1.0 / math.sqrt(HEAD_DIM)
    own_lane = ((jnp.arange(LANES) // HEAD_DIM)[None, None, None, :]
                == jnp.arange(N_KV_HEADS)[None, None, :, None])

    cache_out = lambda c: jnp.transpose(c.reshape(DEPTH, n_dec, N_KV_HEADS, HEAD_DIM, win), (0, 1, 4, 2, 3))
    k_all = jnp.zeros((DEPTH, n_dec, LANES, win), F32)
    v_all = jnp.zeros((DEPTH, n_dec, LANES, win), F32)
    kp, vp, cp, hrp, hip = [], [], [], [], []
    cs, hrs, his = [], [], []
    for l in range(DEPTH):
        final = norm_final_g.reshape(1, D_MODEL) if l == DEPTH - 1 else None

        zq, zkv, zc, u, kvl = _inproj(xp, g_mix, w_in_q, w_in_rest, l, TI_ROWS, n_seq)
        ssm, h_last = _ssm_prompt(u, k1, m2, n4, a_chunk, d3, gw_b, gb3, l, n_seq, seq_len)
        mix, ctx = _mix_prompt(zq, zkv, zc, sinks3, conv_dw_w, cb3, lg3, lb3, l, n_seq, seq_len)
        xp = _tail(xp, g_ffn,
                   [(mix, pl.BlockSpec((tm, V_END), lambda i: (i, 0)), wo_attn_conv),
                    (ssm, pl.BlockSpec((2, tm, LANES), lambda i: (0, i, 0)), wo_ssm)],
                   wg_b, wu_b, wd_b, l, tm, final)
        kp.append(kvl[..., :LANES].reshape(n_seq, WINDOW, N_KV_HEADS, HEAD_DIM))
        vp.append(kvl[..., LANES:].reshape(n_seq, WINDOW, N_KV_HEADS, HEAD_DIM))
        cp.append(ctx)
        hrp.append(h_last[:, 0, :N_STATE].reshape(n_seq, SSM_GROUPS, SSM_STATE))
        hip.append(h_last[:, 0, N_STATE:].reshape(n_seq, SSM_GROUPS, SSM_STATE))

        zq_s, _, zc_s, us, kvn = _inproj(xs, g_mix, w_in_q, w_in_rest, l, n_dec, 1)
        kvn = kvn.reshape(n_dec, V_END - Q_END)
        zq4 = (zq_s * jnp.asarray(scale, BF16)).reshape(n_dec, KV_REP, 1, LANES)
        q3 = jnp.where(own_lane, zq4, jnp.zeros((), BF16)).reshape(n_dec * N_HEADS, LANES)
        kn3 = kvn[:, :LANES].reshape(n_dec, 1, LANES)
        vn3 = kvn[:, LANES:].reshape(n_dec, 1, LANES)
        o3, k_all, v_all, conv_s, nct, ssm_s, nhr, nhi = _dec_mix(
            q3, kn3, vn3, kvn, ck, cv, zc_s, cct, us, hr, hi, sinks_dec, conv_dw_w, cb3, lg3, lb3,
            bmat, cmat, a_step, d3, gw_b, gb3, k_all, v_all, l)
        xs = _tail(xs, g_ffn,
                   [(o3.reshape(n_dec, N_HEADS * LANES), decspec(N_HEADS * LANES), wo_dec),
                    (conv_s, decspec(CONV_CH), wo_conv),
                    (ssm_s, decspec(SSM_CH), wo_ssm)],
                   wg_b, wu_b, wd_b, l, n_dec, final)
        cs.append(jnp.transpose(nct, (1, 0, 2)))
        hrs.append(nhr.reshape(n_dec, SSM_GROUPS, SSM_STATE))
        his.append(nhi.reshape(n_dec, SSM_GROUPS, SSM_STATE))

    return (xp.reshape(n_seq, seq_len, D_MODEL), xs.reshape(n_dec, 1, D_MODEL),
            jnp.stack(kp), jnp.stack(vp), jnp.stack(cp), jnp.stack(hrp), jnp.stack(hip),
            cache_out(k_all), cache_out(v_all), jnp.stack(cs), jnp.stack(hrs), jnp.stack(his))
```

```python
import functools
import math

import jax
import jax.numpy as jnp
from jax import lax
from jax.experimental import pallas as pl
from jax.experimental.pallas import tpu as pltpu

D_MODEL = 1024
DEPTH = 4
HEAD_DIM = 64
ATTN_WIDTH = 512
N_HEADS = 8
N_KV_HEADS = 2
KV_REP = 4
WINDOW = 128
CONV_CH = 256
CONV_WIDTH = 31
SSM_CH = 256
SSM_GROUP = 16
SSM_GROUPS = 16
SSM_STATE = 64
D_FF = 2816
EPS = 1e-6

Q_END = ATTN_WIDTH
K_END = Q_END + N_KV_HEADS * HEAD_DIM
V_END = K_END + N_KV_HEADS * HEAD_DIM
C_END = V_END + 2 * CONV_CH
IN_COLS = C_END + SSM_CH

N_STATE = SSM_GROUPS * SSM_STATE
LANES = 128
SSM_CHUNK = 8
CHUNK_COLS = SSM_CHUNK * SSM_CH
SSM_PARTS = 4
PART_CH = SSM_CH // SSM_PARTS
PART_GROUPS = PART_CH // SSM_GROUP
PART_COLS = SSM_CHUNK * PART_CH
PART_STATE = PART_GROUPS * SSM_STATE
NEG = -1e30

TM_ROWS = 512
TS_ROWS = 4096
TI_ROWS = 1024
FF_CHUNK = 256
OUT_CHUNK = 1024
DEC_BLOCK = 16
CONV_ROWS = 64
ATTN_BLOCKS = 4
VMEM_LIMIT = 56 * 1024 * 1024

F32 = jnp.float32
BF16 = jnp.bfloat16


def _params(n_axes, flags=None):
    return pltpu.CompilerParams(dimension_semantics=("arbitrary",) * n_axes,
                                vmem_limit_bytes=VMEM_LIMIT, flags=flags)


def _resident(shape, index_map):
    return pl.BlockSpec(shape, index_map, pipeline_mode=pl.Buffered(1))


def _rms(x, g):
    return x * lax.rsqrt(jnp.mean(x * x, -1, keepdims=True) + EPS) * g


def _sigmoid(x):
    return 1.0 / (1.0 + jnp.exp(-x))


def _gelu_tanh(x):
    c = math.sqrt(2.0 / math.pi)
    return 0.5 * x * (1.0 + jnp.tanh(c * (x + 0.044715 * (x * x * x))))


def _dot(a, b):
    return jnp.dot(a, b, preferred_element_type=F32)


def _inproj_body(x_ref, g_ref, wq_ref, wr_ref, q_ref, kv_ref, c_ref, u_ref, kvl_ref):
    h = _rms(x_ref[...], g_ref[...]).astype(BF16)
    q_ref[...] = _dot(h, wq_ref[...]).astype(BF16)
    zr = _dot(h, wr_ref[...])
    kv = zr[:, :V_END - Q_END]
    kv_ref[...] = kv.astype(BF16)
    kvl_ref[...] = kv[kv.shape[0] - WINDOW:, :]
    c_ref[...] = zr[:, V_END - Q_END:C_END - Q_END]
    u_ref[0] = zr[:, C_END - Q_END:C_END - Q_END + LANES]
    u_ref[1] = zr[:, C_END - Q_END + LANES:]


def _inproj(x, g3, w_q, w_rest, layer, tm, n_seq):
    rows = x.shape[0]
    tiles_per_seq = rows // n_seq // tm
    kv_w = V_END - Q_END
    return pl.pallas_call(
        _inproj_body,
        grid=(rows // tm,),
        in_specs=[
            pl.BlockSpec((tm, D_MODEL), lambda i: (i, 0)),
            pl.BlockSpec((None, 1, D_MODEL), lambda i: (layer, 0, 0)),
            _resident((None, D_MODEL, Q_END), lambda i: (layer, 0, 0)),
            _resident((None, D_MODEL, IN_COLS - Q_END), lambda i: (layer, 0, 0)),
        ],
        out_specs=[
            pl.BlockSpec((tm, Q_END), lambda i: (i, 0)),
            pl.BlockSpec((tm, kv_w), lambda i: (i, 0)),
            pl.BlockSpec((tm, 2 * CONV_CH), lambda i: (i, 0)),
            pl.BlockSpec((2, tm, LANES), lambda i: (0, i, 0)),
            pl.BlockSpec((None, WINDOW, kv_w), lambda i: (i // tiles_per_seq, 0, 0)),
        ],
        out_shape=[jax.ShapeDtypeStruct((rows, Q_END), BF16),
                   jax.ShapeDtypeStruct((rows, kv_w), BF16),
                   jax.ShapeDtypeStruct((rows, 2 * CONV_CH), F32),
                   jax.ShapeDtypeStruct((2, rows, LANES), F32),
                   jax.ShapeDtypeStruct((n_seq, WINDOW, kv_w), F32)],
        compiler_params=_params(1),
        name="inproj",
    )(x, g3, w_q, w_rest)


def _tail_steps(x, acts, g_ref, wg_ref, wu_ref, wd_ref, gf_ref, o_ref, x1_s, hf_s, act_s):
    x1 = x
    for act, w_ref in acts:
        x1 = x1 + _dot(act, w_ref[...])
    hf_s[...] = _rms(x1, g_ref[...]).astype(BF16)
    x1_s[...] = x1
    yield
    for c in range(0, D_FF, FF_CHUNK):
        gate = _dot(hf_s[...], wg_ref[:, c:c + FF_CHUNK])
        up = _dot(hf_s[...], wu_ref[:, c:c + FF_CHUNK])
        act_s[:, c:c + FF_CHUNK] = (gate * _sigmoid(gate) * up).astype(BF16)
        yield
    for n in range(0, D_MODEL, OUT_CHUNK):
        o_ref[:, n:n + OUT_CHUNK] = x1_s[:, n:n + OUT_CHUNK] + _dot(act_s[...], wd_ref[:, n:n + OUT_CHUNK])
        yield
    if gf_ref is not None:
        o_ref[...] = _rms(o_ref[...], gf_ref[...])


def _tail_body(n_parts, final, *refs):
    x_ref, g_ref = refs[0], refs[1]
    parts = refs[2:2 + 2 * n_parts]
    wg_ref, wu_ref, wd_ref = refs[2 + 2 * n_parts:5 + 2 * n_parts]
    gf_ref = refs[5 + 2 * n_parts] if final else None
    o_ref, x1_s, hf_s, act_s = refs[-4:]
    acts = []
    for p in range(n_parts):
        act_ref = parts[2 * p]
        if len(act_ref.shape) == 3:
            act = jnp.concatenate([act_ref[i] for i in range(act_ref.shape[0])], -1)
        else:
            act = act_ref[...]
        acts.append((act.astype(BF16), parts[2 * p + 1]))
    for _ in _tail_steps(x_ref[...], acts, g_ref, wg_ref, wu_ref, wd_ref, gf_ref, o_ref, x1_s, hf_s, act_s):
        pass


def _tail(x, g3, parts, wg, wu, wd, layer, tm, final_g=None):
    rows = x.shape[0]
    final = final_g is not None
    in_specs = [pl.BlockSpec((tm, D_MODEL), lambda i: (i, 0)),
                pl.BlockSpec((None, 1, D_MODEL), lambda i: (layer, 0, 0))]
    args = [x, g3]
    for act, spec, w in parts:
        in_specs.append(spec)
        in_specs.append(pl.BlockSpec((None,) + w.shape[1:], lambda i: (layer, 0, 0)))
        args += [act, w]
    in_specs += [
        _resident((None, D_MODEL, D_FF), lambda i: (layer, 0, 0)),
        _resident((None, D_MODEL, D_FF), lambda i: (layer, 0, 0)),
        _resident((None, D_FF, D_MODEL), lambda i: (layer, 0, 0)),
    ]
    args += [wg, wu, wd]
    if final:
        in_specs.append(pl.BlockSpec((1, D_MODEL), lambda i: (0, 0)))
        args.append(final_g)
    return pl.pallas_call(
        functools.partial(_tail_body, len(parts), final),
        grid=(rows // tm,),
        in_specs=in_specs,
        out_specs=pl.BlockSpec((tm, D_MODEL), lambda i: (i, 0)),
        out_shape=jax.ShapeDtypeStruct((rows, D_MODEL), F32),
        scratch_shapes=[pltpu.VMEM((tm, D_MODEL), F32), pltpu.VMEM((tm, D_MODEL), BF16),
                        pltpu.VMEM((tm, D_FF), BF16)],
        compiler_params=_params(1),
        name="tail",
    )(*args)


def _expand_ssm_operands(k1_ref, m2_ref, n4_ref, w1_ref, w2_ref, w4_ref):
    compact = SSM_CHUNK * SSM_GROUP
    col = lax.broadcasted_iota(jnp.int32, (SSM_GROUP, PART_COLS), 1)
    col_chan_group = (col % PART_CH) // SSM_GROUP
    col_state_group = (col % PART_STATE) // SSM_STATE
    e_row = lax.broadcasted_iota(jnp.int32, (compact, PART_COLS), 0)
    e_col = lax.broadcasted_iota(jnp.int32, (compact, PART_COLS), 1)
    spread = ((e_row // SSM_GROUP == e_col // PART_CH) & (e_row % SSM_GROUP == e_col % SSM_GROUP))
    spread = jnp.where(spread, 1.0, 0.0).astype(BF16)
    rows = lax.broadcasted_iota(jnp.int32, (LANES, PART_COLS), 0)
    cols = lax.broadcasted_iota(jnp.int32, (LANES, PART_COLS), 1)
    for q in range(SSM_PARTS):
        for l in range(SSM_CHUNK):
            k1 = k1_ref[q, l * SSM_GROUP:(l + 1) * SSM_GROUP, :]
            m2 = m2_ref[q, l * SSM_GROUP:(l + 1) * SSM_GROUP, :]
            for g in range(PART_GROUPS):
                r0 = l * PART_CH + g * SSM_GROUP
                w1_ref[q, r0:r0 + SSM_GROUP, :] = jnp.where(col_chan_group == g, k1, 0.0).astype(BF16)
                w2_ref[q, r0:r0 + SSM_GROUP, :] = jnp.where(col_state_group == g, m2, 0.0).astype(BF16)
        for b in range(2 * PART_STATE // LANES):
            full = _dot(n4_ref[q, b * LANES:(b + 1) * LANES, :].astype(BF16), spread)
            row_group = ((rows + b * LANES) % PART_STATE) // SSM_STATE
            keep = row_group == (cols % PART_CH) // SSM_GROUP
            w4_ref[q, b * LANES:(b + 1) * LANES, :] = jnp.where(keep, full, 0.0).astype(BF16)


def _ssm_body(n_chunks, u_ref, k1_ref, m2_ref, n4_ref, al_ref, d_ref, gw_ref, gb_ref,
              o_ref, hl_ref, x_s, gh_s, hc_s, w1_ref, w2_ref, w4_ref):
    @pl.when((pl.program_id(0) == 0) & (pl.program_id(1) == 0))
    def _():
        _expand_ssm_operands(k1_ref, m2_ref, n4_ref, w1_ref, w2_ref, w4_ref)

    @pl.when(pl.program_id(1) == 0)
    def _():
        hc_s[...] = jnp.zeros_like(hc_s)

    n_slabs = SSM_CH // LANES

    def steps(l, o):
        return u_ref[o, pl.ds(l, n_chunks, stride=SSM_CHUNK), :]

    low = lax.broadcasted_iota(jnp.int32, (n_chunks, LANES), 1) < PART_CH
    for o in range(n_slabs):
        for l in range(0, SSM_CHUNK, 2):
            a, b = steps(l, o), steps(l + 1, o)
            c0 = l * PART_CH
            first = jnp.where(low, a, pltpu.roll(b, PART_CH, 1))
            second = jnp.where(low, pltpu.roll(a, PART_CH, 1), b)
            x_s[:, 2 * o * PART_COLS + c0:2 * o * PART_COLS + c0 + LANES] = first.astype(BF16)
            x_s[:, (2 * o + 1) * PART_COLS + c0:(2 * o + 1) * PART_COLS + c0 + LANES] = second.astype(BF16)

    for q in range(SSM_PARTS):
        g = _dot(x_s[:, q * PART_COLS:(q + 1) * PART_COLS], w2_ref[q])
        gh_s[:, q * PART_STATE:(q + 1) * PART_STATE] = g[:, :PART_STATE]
        gh_s[:, N_STATE + q * PART_STATE:N_STATE + (q + 1) * PART_STATE] = g[:, PART_STATE:]

    a_re = al_ref[:, :N_STATE]
    a_im = al_ref[:, N_STATE:]

    def step(k, carry):
        h_re, h_im = carry
        g_re = gh_s[pl.ds(k, 1), :N_STATE]
        g_im = gh_s[pl.ds(k, 1), N_STATE:]
        gh_s[pl.ds(k, 1), :N_STATE] = h_re
        gh_s[pl.ds(k, 1), N_STATE:] = h_im
        return (a_re * h_re - a_im * h_im + g_re, a_re * h_im + a_im * h_re + g_im)

    h_re, h_im = lax.fori_loop(0, n_chunks, step, (hc_s[:, :N_STATE], hc_s[:, N_STATE:]))
    hc_s[:, :N_STATE] = h_re
    hc_s[:, N_STATE:] = h_im
    hl_ref[:, :N_STATE] = h_re
    hl_ref[:, N_STATE:] = h_im

    hb = gh_s[...].astype(BF16)
    hcat = [jnp.concatenate([hb[:, q * PART_STATE:(q + 1) * PART_STATE],
                             hb[:, N_STATE + q * PART_STATE:N_STATE + (q + 1) * PART_STATE]], -1)
            for q in range(SSM_PARTS)]
    width = 2 * LANES
    per_dot = width // PART_CH
    for l0 in range(0, SSM_CHUNK, per_dot):
        c0, k1 = l0 * PART_CH, (l0 + per_dot) * PART_CH
        ys = [_dot(x_s[:, q * PART_COLS:q * PART_COLS + k1], w1_ref[q, :k1, c0:c0 + width])
              + _dot(hcat[q], w4_ref[q, :, c0:c0 + width]) for q in range(SSM_PARTS)]
        for dl in range(per_dot):
            l = l0 + dl
            y = jnp.concatenate([ys[q][:, dl * PART_CH:(dl + 1) * PART_CH] for q in range(SSM_PARTS)], -1)
            y = _gelu_tanh(y + d_ref[...] * jnp.concatenate([steps(l, o) for o in range(n_slabs)], -1))
            gate = _dot(y.astype(BF16), gw_ref[...]) + gb_ref[...]
            out = y * _sigmoid(gate)
            for o in range(n_slabs):
                o_ref[o, pl.ds(l, n_chunks, stride=SSM_CHUNK), :] = out[:, o * LANES:(o + 1) * LANES]


def _ssm_prompt(u, k1, m2, n4, al, d3, gw, gb3, layer, n_seq, seq_len):
    ts = TS_ROWS
    n_tiles = seq_len // ts
    n_chunks = ts // SSM_CHUNK
    rows = u.shape[1]
    compact = SSM_CHUNK * SSM_GROUP
    return pl.pallas_call(
        functools.partial(_ssm_body, n_chunks),
        grid=(n_seq, n_tiles),
        in_specs=[
            pl.BlockSpec((SSM_CH // LANES, ts, LANES), lambda s, i: (0, s * n_tiles + i, 0)),
            pl.BlockSpec((None, SSM_PARTS, compact, PART_COLS), lambda s, i: (layer, 0, 0, 0)),
            pl.BlockSpec((None, SSM_PARTS, compact, 2 * PART_STATE), lambda s, i: (layer, 0, 0, 0)),
            pl.BlockSpec((None, SSM_PARTS, 2 * PART_STATE, compact), lambda s, i: (layer, 0, 0, 0)),
            pl.BlockSpec((None, 1, 2 * N_STATE), lambda s, i: (layer, 0, 0)),
            pl.BlockSpec((None, 1, SSM_CH), lambda s, i: (layer, 0, 0)),
            pl.BlockSpec((None, SSM_CH, SSM_CH), lambda s, i: (layer, 0, 0)),
            pl.BlockSpec((None, 1, SSM_CH), lambda s, i: (layer, 0, 0)),
        ],
        out_specs=[
            pl.BlockSpec((SSM_CH // LANES, ts, LANES), lambda s, i: (0, s * n_tiles + i, 0)),
            pl.BlockSpec((None, 1, 2 * N_STATE), lambda s, i: (s, 0, 0)),
        ],
        out_shape=[jax.ShapeDtypeStruct((SSM_CH // LANES, rows, LANES), F32),
                   jax.ShapeDtypeStruct((n_seq, 1, 2 * N_STATE), F32)],
        scratch_shapes=[
            pltpu.VMEM((n_chunks, CHUNK_COLS), BF16),
            pltpu.VMEM((n_chunks, 2 * N_STATE), F32),
            pltpu.VMEM((1, 2 * N_STATE), F32),
            pltpu.VMEM((SSM_PARTS, PART_COLS, PART_COLS), BF16),
            pltpu.VMEM((SSM_PARTS, PART_COLS, 2 * PART_STATE), BF16),
            pltpu.VMEM((SSM_PARTS, 2 * PART_STATE, PART_COLS), BF16),
        ],
        compiler_params=_params(2),
        name="ssm_prompt",
    )(u, k1, m2, n4, al, d3, gw, gb3)


def _mix_steps(tm, first_tile, q_ref, kv_ref, a_ref, gg_ref, sink_ref, cw_ref, cb_ref, lg_ref, lb_ref,
                 o_ref, ctx_ref, kv_s, u_s, ush_s, bias_s):
    pad = 32
    off = pad - (CONV_WIDTH - 1)
    sub = 8

    @pl.when(first_tile)
    def _():
        kv_s[0:WINDOW, :] = jnp.zeros((WINDOW, 2 * LANES), F32)
        u_s[0:pad, :] = jnp.zeros((pad, CONV_CH), F32)
        qi = lax.broadcasted_iota(jnp.int32, (WINDOW, 2 * WINDOW), 0)
        ki = lax.broadcasted_iota(jnp.int32, (WINDOW, 2 * WINDOW), 1)
        dist = qi - ki + WINDOW
        valid = (dist >= 0) & (dist < WINDOW)
        distf = dist.astype(F32)
        for g in range(N_KV_HEADS):
            for r in range(KV_REP):
                slope = 2.0 ** (-8.0 * (g * KV_REP + r + 1) / N_HEADS)
                bias_s[g, r * WINDOW:(r + 1) * WINDOW, :] = jnp.where(valid, -slope * distf, NEG)

    kv_s[WINDOW:, :] = kv_ref[...].astype(F32)
    u_s[pad:, :] = a_ref[...] * _sigmoid(gg_ref[...])
    for b in range(1, sub):
        ush_s[b - 1] = u_s[b:b + tm + pad - sub, :]
    ctx_ref[...] = u_s[tm + off:tm + pad, :]
    yield

    def conv_units():
        for r0 in range(0, tm, CONV_ROWS):
            acc = jnp.zeros((CONV_ROWS, CONV_CH), F32) + cb_ref[...]
            for j in range(CONV_WIDTH):
                a0, b = divmod(off + j, sub)
                lo = r0 + a0 * sub
                rows = u_s[lo:lo + CONV_ROWS, :] if b == 0 else ush_s[b - 1, lo:lo + CONV_ROWS, :]
                acc = acc + cw_ref[j:j + 1, :] * rows
            mu = jnp.mean(acc, -1, keepdims=True)
            cen = acc - mu
            var = jnp.mean(cen * cen, -1, keepdims=True)
            yn = cen * lax.rsqrt(var + EPS) * lg_ref[...] + lb_ref[...]
            o_ref[r0:r0 + CONV_ROWS, ATTN_WIDTH:] = (yn * _sigmoid(yn)).astype(BF16)
            yield

    scale = 1.0 / math.sqrt(HEAD_DIM)
    k_lane_group = lax.broadcasted_iota(jnp.int32, (2 * WINDOW, LANES), 1) // HEAD_DIM
    o_lane_group = lax.broadcasted_iota(jnp.int32, (KV_REP * WINDOW, LANES), 1) // HEAD_DIM
    key_in_prev = lax.broadcasted_iota(jnp.int32, (1, 2 * WINDOW), 1) < WINDOW
    no_prev_block = jnp.logical_and(key_in_prev, first_tile)
    sinks = [jnp.concatenate([jnp.broadcast_to(sink_ref[:, g * KV_REP + r:g * KV_REP + r + 1], (WINDOW, 1))
                              for r in range(KV_REP)], 0) for g in range(N_KV_HEADS)]

    def attention_units():
        n_blocks = tm // WINDOW
        for b0 in range(0, n_blocks, ATTN_BLOCKS):
            blocks = range(b0, min(b0 + ATTN_BLOCKS, n_blocks))
            pairs = [(blk, g) for blk in blocks for g in range(N_KV_HEADS)]
            kblk = {blk: kv_s[blk * WINDOW:(blk + 2) * WINDOW, 0:LANES] for blk in blocks}
            vblk = {blk: kv_s[blk * WINDOW:(blk + 2) * WINDOW, LANES:].astype(BF16) for blk in blocks}
            qs = {blk: (jnp.concatenate([q_ref[blk * WINDOW:(blk + 1) * WINDOW, r * LANES:(r + 1) * LANES]
                                         for r in range(KV_REP)], 0) * jnp.asarray(scale, BF16))
                  for blk in blocks}
            kg = {(blk, g): jnp.where(k_lane_group == g, kblk[blk], 0.0).astype(BF16) for blk, g in pairs}
            s = {(blk, g): lax.dot_general(qs[blk], kg[blk, g], (((1,), (1,)), ((), ())),
                                           preferred_element_type=F32) + bias_s[g] for blk, g in pairs}
            for g in range(N_KV_HEADS):
                if b0 == 0:
                    s[0, g] = jnp.where(no_prev_block, NEG, s[0, g])
            m = {k: jnp.maximum(jnp.max(s[k], -1, keepdims=True), sinks[k[1]]) for k in pairs}
            p = {k: jnp.exp(s[k] - m[k]) for k in pairs}
            denom = {k: jnp.sum(p[k], -1, keepdims=True) + jnp.exp(sinks[k[1]] - m[k]) for k in pairs}
            og = {k: _dot(p[k].astype(BF16), vblk[k[0]]) / denom[k] for k in pairs}
            for blk in blocks:
                o = jnp.where(o_lane_group == 0, og[blk, 0], og[blk, 1])
                for r in range(KV_REP):
                    o_ref[blk * WINDOW:(blk + 1) * WINDOW, r * LANES:(r + 1) * LANES] = (
                        o[r * WINDOW:(r + 1) * WINDOW, :].astype(BF16))
            yield

    yield from conv_units()
    yield from attention_units()

    kv_s[0:WINDOW, :] = kv_s[tm:tm + WINDOW, :]
    u_s[0:pad, :] = u_s[tm:tm + pad, :]


def _mix_body(tm, *refs):
    for _ in _mix_steps(tm, pl.program_id(1) == 0, *refs):
        pass


def _mix_prompt(zq, zkv, zc, sinks3, cw, cb3, lg3, lb3, layer, n_seq, seq_len):
    tm = TM_ROWS
    n_tiles = seq_len // tm
    rows = n_seq * seq_len
    row = lambda s, i: s * n_tiles + i
    vec = lambda width: pl.BlockSpec((None, 1, width), lambda s, i: (layer, 0, 0))
    return pl.pallas_call(
        functools.partial(_mix_body, tm),
        grid=(n_seq, n_tiles),
        in_specs=[
            pl.BlockSpec((tm, ATTN_WIDTH), lambda s, i: (row(s, i), 0)),
            pl.BlockSpec((tm, 2 * LANES), lambda s, i: (row(s, i), 0)),
            pl.BlockSpec((tm, CONV_CH), lambda s, i: (row(s, i), 0)),
            pl.BlockSpec((tm, CONV_CH), lambda s, i: (row(s, i), 1)),
            vec(N_HEADS),
            pl.BlockSpec((None, CONV_WIDTH, CONV_CH), lambda s, i: (layer, 0, 0)),
            vec(CONV_CH), vec(CONV_CH), vec(CONV_CH),
        ],
        out_specs=[
            pl.BlockSpec((tm, V_END), lambda s, i: (row(s, i), 0)),
            pl.BlockSpec((None, CONV_WIDTH - 1, CONV_CH), lambda s, i: (s, 0, 0)),
        ],
        out_shape=[jax.ShapeDtypeStruct((rows, V_END), BF16),
                   jax.ShapeDtypeStruct((n_seq, CONV_WIDTH - 1, CONV_CH), F32)],
        scratch_shapes=[pltpu.VMEM((tm + WINDOW, 2 * LANES), F32),
                        pltpu.VMEM((tm + 32, CONV_CH), F32),
                        pltpu.VMEM((7, tm + 24, CONV_CH), F32),
                        pltpu.VMEM((N_KV_HEADS, KV_REP * WINDOW, 2 * WINDOW), F32)],
        compiler_params=_params(2),
        name="mix_prompt",
    )(zq, zkv, zc, zc, sinks3, cw, cb3, lg3, lb3)


def _dec_body(q_ref, kn_ref, vn_ref, k2_ref, v2_ref, ck_ref, cv_ref, a_ref, gg_ref, cc_ref, u_ref, hr_ref, hi_ref,
              sink_ref, cw_ref, cb_ref, lg_ref, lb_ref, bm_ref, cm_ref, ab_ref, d_ref, gw_ref, gb_ref,
              kall_ref, vall_ref, o_ref, ok_ref, ov_ref, co_ref, oc_ref, so_ref, or_ref, oi_ref):
    nb = DEC_BLOCK
    win = ck_ref.shape[2]

    q3 = q_ref[...].reshape(nb, N_HEADS, LANES)
    kn = kn_ref[...]
    vn = vn_ref[...]
    ck = ck_ref[...]
    cv = cv_ref[...]
    s = jnp.einsum("nsc,ncj->nsj", q3, ck.astype(BF16), preferred_element_type=F32)
    si = lax.broadcasted_iota(jnp.int32, (N_HEADS, win), 0)
    ji = lax.broadcasted_iota(jnp.int32, (N_HEADS, win), 1)
    head = (si % 2) * KV_REP + si // 2
    slope = jnp.zeros((N_HEADS, win), F32)
    for h in range(N_HEADS):
        slope = jnp.where(head == h, 2.0 ** (-8.0 * (h + 1) / N_HEADS), slope)
    dist = win - ji
    bias = jnp.where(dist < WINDOW, -slope * dist.astype(F32), NEG)
    s = s + bias[None]
    s_new = jnp.sum(q3.astype(F32) * kn.astype(BF16).astype(F32), -1, keepdims=True)
    sink = sink_ref[...][None]
    m = jnp.maximum(jnp.maximum(jnp.max(s, -1, keepdims=True), s_new), sink)
    p = jnp.exp(s - m)
    p_new = jnp.exp(s_new - m)
    denom = jnp.sum(p, -1, keepdims=True) + p_new + jnp.exp(sink - m)
    o = jnp.einsum("nsj,ncj->nsc", p.astype(BF16), cv.astype(BF16), preferred_element_type=F32)
    o = o + p_new.astype(BF16).astype(F32) * vn.astype(BF16).astype(F32)
    o_ref[...] = (o / denom).reshape(nb * N_HEADS, LANES)

    last = lax.broadcasted_iota(jnp.int32, (LANES, win), 1) == win - 1
    fill = jnp.zeros((LANES - nb, LANES), F32)
    knt = jnp.concatenate([k2_ref[...], fill], 0).T
    vnt = jnp.concatenate([v2_ref[...], fill], 0).T
    for i in range(nb):
        ok_ref[i] = jnp.where(last, pltpu.roll(knt, win - 1 - i, 1), pltpu.roll(ck[i], win - 1, 1))
        ov_ref[i] = jnp.where(last, pltpu.roll(vnt, win - 1 - i, 1), pltpu.roll(cv[i], win - 1, 1))

    @pl.when(pl.program_id(0) == 0)
    def _():
        u = a_ref[...] * _sigmoid(gg_ref[...])
        acc = cb_ref[...] + cw_ref[CONV_WIDTH - 1:CONV_WIDTH, :] * u
        for j in range(CONV_WIDTH - 1):
            acc = acc + cw_ref[j:j + 1, :] * cc_ref[j]
        mu = jnp.mean(acc, -1, keepdims=True)
        cen = acc - mu
        var = jnp.mean(cen * cen, -1, keepdims=True)
        yn = cen * lax.rsqrt(var + EPS) * lg_ref[...] + lb_ref[...]
        co_ref[...] = yn * _sigmoid(yn)
        for j in range(CONV_WIDTH - 2):
            oc_ref[j] = cc_ref[j + 1]
        oc_ref[CONV_WIDTH - 2] = u

        us = jnp.concatenate([u_ref[0], u_ref[1]], -1)
        bu = _dot(us.astype(BF16), bm_ref[...])
        a_re = ab_ref[:, :N_STATE]
        a_im = ab_ref[:, N_STATE:]
        h_re = hr_ref[...]
        h_im = hi_ref[...]
        n_re = a_re * h_re - a_im * h_im + bu[:, :N_STATE]
        n_im = a_re * h_im + a_im * h_re + bu[:, N_STATE:]
        or_ref[...] = n_re
        oi_ref[...] = n_im
        hcat = jnp.concatenate([n_re, n_im], -1).astype(BF16)
        y = _gelu_tanh(_dot(hcat, cm_ref[...]) + d_ref[...] * us)
        gate = _dot(y.astype(BF16), gw_ref[...]) + gb_ref[...]
        so_ref[...] = y * _sigmoid(gate)


def _dec_mix(q3, kn3, vn3, kvn, ck, cv, zc, cct, u, hr, hi, sinks3, cw, cb3, lg3, lb3,
             bmat, cmat, ab3, d3, gw, gb3, k_all, v_all, layer):
    n = kn3.shape[0]
    win = ck.shape[3]
    nb = DEC_BLOCK
    vec = lambda width: pl.BlockSpec((None, 1, width), lambda i: (layer, 0, 0))
    mat = lambda r, c: pl.BlockSpec((None, r, c), lambda i: (layer, 0, 0))
    cache = pl.BlockSpec((None, nb, LANES, win), lambda i: (layer, i, 0, 0))
    ctx = pl.BlockSpec((None, CONV_WIDTH - 1, n, CONV_CH), lambda i: (layer, 0, 0, 0))
    state = pl.BlockSpec((None, n, N_STATE), lambda i: (layer, 0, 0))
    allrows = lambda width: pl.BlockSpec((n, width), lambda i: (0, 0))
    new3 = pl.BlockSpec((nb, 1, LANES), lambda i: (i, 0, 0))
    return pl.pallas_call(
        _dec_body,
        grid=(n // nb,),
        in_specs=[
            pl.BlockSpec((nb * N_HEADS, LANES), lambda i: (i, 0)), new3, new3,
            pl.BlockSpec((nb, LANES), lambda i: (i, 0)),
            pl.BlockSpec((nb, LANES), lambda i: (i, 1)),
            cache, cache,
            pl.BlockSpec((n, CONV_CH), lambda i: (0, 0)),
            pl.BlockSpec((n, CONV_CH), lambda i: (0, 1)),
            ctx,
            pl.BlockSpec((2, n, LANES), lambda i: (0, 0, 0)),
            state, state,
            mat(N_HEADS, 1), mat(CONV_WIDTH, CONV_CH), vec(CONV_CH), vec(CONV_CH), vec(CONV_CH),
            mat(SSM_CH, 2 * N_STATE), mat(2 * N_STATE, SSM_CH), vec(2 * N_STATE), vec(SSM_CH),
            mat(SSM_CH, SSM_CH), vec(SSM_CH),
            pl.BlockSpec(memory_space=pl.ANY), pl.BlockSpec(memory_space=pl.ANY),
        ],
        out_specs=[
            pl.BlockSpec((nb * N_HEADS, LANES), lambda i: (i, 0)),
            cache, cache,
            allrows(CONV_CH),
            pl.BlockSpec((CONV_WIDTH - 1, n, CONV_CH), lambda i: (0, 0, 0)),
            allrows(SSM_CH), allrows(N_STATE), allrows(N_STATE),
        ],
        out_shape=[
            jax.ShapeDtypeStruct((n * N_HEADS, LANES), F32),
            jax.ShapeDtypeStruct(k_all.shape, F32),
            jax.ShapeDtypeStruct(v_all.shape, F32),
            jax.ShapeDtypeStruct((n, CONV_CH), F32),
            jax.ShapeDtypeStruct((CONV_WIDTH - 1, n, CONV_CH), F32),
            jax.ShapeDtypeStruct((n, SSM_CH), F32),
            jax.ShapeDtypeStruct((n, N_STATE), F32),
            jax.ShapeDtypeStruct((n, N_STATE), F32),
        ],
        input_output_aliases={24: 1, 25: 2},
        compiler_params=_params(1),
        name="dec_mix",
    )(q3, kn3, vn3, kvn, kvn, ck, cv, zc, zc, cct, u, hr, hi, sinks3, cw, cb3, lg3, lb3,
      bmat, cmat, ab3, d3, gw, gb3, k_all, v_all)


def _ssm_operands(a_re, a_im, log_dt, b_re, b_im, c_re, c_im):
    hi = lax.Precision.HIGHEST
    l_n, g_n, p_n, c_n = SSM_CHUNK, SSM_GROUPS, SSM_STATE, SSM_GROUP
    dt = jnp.exp(log_dt)[:, None]
    lam_re, lam_im = a_re * dt, a_im * dt
    steps = jnp.arange(l_n + 1, dtype=F32)[:, None, None]
    mag = jnp.exp(steps * lam_re)
    pw_re, pw_im = mag * jnp.cos(steps * lam_im), mag * jnp.sin(steps * lam_im)
    ab_re, ab_im = pw_re[1], pw_im[1]
    den = a_re * a_re + a_im * a_im
    q_re = ((ab_re - 1.0) * a_re + ab_im * a_im) / den
    q_im = (ab_im * a_re - (ab_re - 1.0) * a_im) / den
    bt_re, bt_im = jnp.transpose(b_re, (2, 0, 1)), jnp.transpose(b_im, (2, 0, 1))
    bb_re = q_re * bt_re - q_im * bt_im
    bb_im = q_re * bt_im + q_im * bt_re
    pl_re, pl_im = pw_re[:l_n, None], pw_im[:l_n, None]
    pb_re = pl_re * bb_re - pl_im * bb_im
    pb_im = pl_re * bb_im + pl_im * bb_re

    ktau = (jnp.einsum("gop,tcgp->tcgo", c_re, pb_re, precision=hi)
            - jnp.einsum("gop,tcgp->tcgo", c_im, pb_im, precision=hi)).reshape(l_n, c_n, SSM_CH)
    lag = jnp.arange(l_n)[None, :] - jnp.arange(l_n)[:, None]
    k1 = jnp.where((lag >= 0)[:, :, None, None], jnp.take(ktau, jnp.maximum(lag, 0), axis=0), 0.0)
    k1 = jnp.transpose(k1, (0, 2, 1, 3)).reshape(l_n * c_n, CHUNK_COLS)
    back = l_n - 1 - jnp.arange(l_n)
    m2 = jnp.concatenate([jnp.take(pb_re, back, axis=0).reshape(l_n, c_n, N_STATE),
                          jnp.take(pb_im, back, axis=0).reshape(l_n, c_n, N_STATE)], -1)
    m2 = m2.reshape(l_n * c_n, 2 * N_STATE)
    ct_re, ct_im = jnp.transpose(c_re, (0, 2, 1)), jnp.transpose(c_im, (0, 2, 1))
    pn_re = jnp.transpose(pw_re[1:], (1, 2, 0))[..., None]
    pn_im = jnp.transpose(pw_im[1:], (1, 2, 0))[..., None]
    n4_re = (ct_re[:, :, None, :] * pn_re - ct_im[:, :, None, :] * pn_im).reshape(N_STATE, l_n * c_n)
    n4_im = (ct_re[:, :, None, :] * pn_im + ct_im[:, :, None, :] * pn_re).reshape(N_STATE, l_n * c_n)
    n4 = jnp.concatenate([n4_re, -n4_im], 0)

    chan_group = jnp.arange(SSM_CH) // c_n
    state_group = (jnp.arange(2 * N_STATE) % N_STATE) // p_n
    bmat = jnp.where(chan_group[:, None] == state_group[None, :],
                     jnp.tile(m2[(l_n - 1) * c_n:], (g_n, 1)), 0.0)
    cc = jnp.concatenate([ct_re.reshape(N_STATE, c_n), -ct_im.reshape(N_STATE, c_n)], 0)
    cmat = jnp.where(state_group[:, None] == chan_group[None, :], jnp.tile(cc, (1, g_n)), 0.0)

    k1_h = jnp.transpose(k1.reshape(l_n * c_n, l_n, SSM_PARTS, PART_CH), (2, 0, 1, 3)).reshape(
        SSM_PARTS, l_n * c_n, PART_COLS)
    m2_h = jnp.transpose(m2.reshape(l_n * c_n, 2, SSM_PARTS, PART_STATE), (2, 0, 1, 3)).reshape(
        SSM_PARTS, l_n * c_n, 2 * PART_STATE)
    n4_h = jnp.transpose(n4.reshape(2, SSM_PARTS, PART_STATE, l_n * c_n), (1, 0, 2, 3)).reshape(
        SSM_PARTS, 2 * PART_STATE, l_n * c_n)

    flat = lambda re, im: jnp.concatenate([re.reshape(1, N_STATE), im.reshape(1, N_STATE)], -1)
    return (k1_h, m2_h, n4_h, bmat.astype(BF16), cmat.astype(BF16),
            flat(pw_re[l_n], pw_im[l_n]), flat(ab_re, ab_im))


def _decode_head_order():
    s = jnp.arange(N_HEADS)
    return (s % 2) * KV_REP + s // 2


def kernel(x_prompt, x_sample, cache_swa_k, cache_swa_v, cache_conv, state_ssm_re, state_ssm_im,
           norm_mix_g, w_in, attn_sinks, conv_dw_w, conv_dw_b, conv_ln_g, conv_ln_b,
           ssm_a_re, ssm_a_im, ssm_log_dt, ssm_b_re, ssm_b_im, ssm_c_re, ssm_c_im,
           ssm_d, ssm_glu_w, ssm_glu_b, w_out, norm_ffn_g, w_ff_gate, w_ff_up, w_ff_down,
           norm_final_g):
    n_seq, seq_len, _ = x_prompt.shape
    n_dec = x_sample.shape[0]
    win = cache_swa_k.shape[2]
    assert x_sample.shape[1] == 1 and win == WINDOW
    assert n_dec == WINDOW
    assert seq_len % TS_ROWS == 0 and n_dec % DEC_BLOCK == 0

    row3 = lambda v: v.reshape(DEPTH, 1, -1)
    g_mix, g_ffn = row3(norm_mix_g), row3(norm_ffn_g)
    sinks3 = row3(attn_sinks)
    order = _decode_head_order()
    sinks_dec = attn_sinks[:, order][:, :, None]
    cb3, lg3, lb3 = row3(conv_dw_b), row3(conv_ln_g), row3(conv_ln_b)
    d3, gb3 = row3(ssm_d), row3(ssm_glu_b)
    w_in_q = jnp.transpose(w_in[:, :, :Q_END].astype(BF16).reshape(DEPTH, D_MODEL, N_KV_HEADS, KV_REP, HEAD_DIM),
                           (0, 1, 3, 2, 4)).reshape(DEPTH, D_MODEL, Q_END)
    w_in_rest = w_in[:, :, Q_END:].astype(BF16)
    w_out_b = w_out.astype(BF16)
    wg_b, wu_b, wd_b = w_ff_gate.astype(BF16), w_ff_up.astype(BF16), w_ff_down.astype(BF16)
    gw_b = ssm_glu_w.astype(BF16)
    k1, m2, n4, bmat, cmat, a_chunk, a_step = jax.vmap(_ssm_operands)(
        ssm_a_re, ssm_a_im, ssm_log_dt, ssm_b_re, ssm_b_im, ssm_c_re, ssm_c_im)

    wo_heads = w_out_b[:, :ATTN_WIDTH].reshape(DEPTH, N_HEADS, HEAD_DIM, D_MODEL)[:, order]
    own = (jnp.arange(N_HEADS)[:, None] % 2) == jnp.arange(N_KV_HEADS)[None, :]
    wo_dec = jnp.where(own[None, :, :, None, None], wo_heads[:, :, None], 0).reshape(
        DEPTH, N_HEADS * LANES, D_MODEL)
    wo_attn = jnp.transpose(w_out_b[:, :ATTN_WIDTH].reshape(DEPTH, N_KV_HEADS, KV_REP, HEAD_DIM, D_MODEL),
                            (0, 2, 1, 3, 4)).reshape(DEPTH, ATTN_WIDTH, D_MODEL)
    wo_attn_conv = jnp.concatenate([wo_attn, w_out_b[:, ATTN_WIDTH:V_END]], 1)
    wo_conv = w_out_b[:, ATTN_WIDTH:V_END]
    wo_ssm = w_out_b[:, V_END:]

    ck = jnp.transpose(cache_swa_k, (0, 1, 3, 4, 2)).reshape(DEPTH, n_dec, LANES, win)
    cv = jnp.transpose(cache_swa_v, (0, 1, 3, 4, 2)).reshape(DEPTH, n_dec, LANES, win)
    cct = jnp.transpose(cache_conv, (0, 2, 1, 3))
    hr = state_ssm_re.reshape(DEPTH, n_dec, N_STATE)
    hi = state_ssm_im.reshape(DEPTH, n_dec, N_STATE)

    xp = x_prompt.reshape(n_seq * seq_len, D_MODEL)
    xs = x_sample.reshape(n_dec, D_MODEL)
    tm = TM_ROWS
    decspec = lambda width: pl.BlockSpec((n_dec, width), lambda i: (i, 0))
    scale = 1.0 / math.sqrt(HEAD_DIM)
    own_lane = ((jnp.arange(LANES) // HEAD_DIM)[None, None, None, :]
                == jnp.arange(N_KV_HEADS)[None, None, :, None])

    cache_out = lambda c: jnp.transpose(c.reshape(DEPTH, n_dec, N_KV_HEADS, HEAD_DIM, win), (0, 1, 4, 2, 3))
    k_all = jnp.zeros((DEPTH, n_dec, LANES, win), F32)
    v_all = jnp.zeros((DEPTH, n_dec, LANES, win), F32)
    kp, vp, cp, hrp, hip = [], [], [], [], []
    cs, hrs, his = [], [], []
    for l in range(DEPTH):
        final = norm_final_g.reshape(1, D_MODEL) if l == DEPTH - 1 else None

        zq, zkv, zc, u, kvl = _inproj(xp, g_mix, w_in_q, w_in_rest, l, TI_ROWS, n_seq)
        ssm, h_last = _ssm_prompt(u, k1, m2, n4, a_chunk, d3, gw_b, gb3, l, n_seq, seq_len)
        mix, ctx = _mix_prompt(zq, zkv, zc, sinks3, conv_dw_w, cb3, lg3, lb3, l, n_seq, seq_len)
        xp = _tail(xp, g_ffn,
                   [(mix, pl.BlockSpec((tm, V_END), lambda i: (i, 0)), wo_attn_conv),
                    (ssm, pl.BlockSpec((2, tm, LANES), lambda i: (0, i, 0)), wo_ssm)],
                   wg_b, wu_b, wd_b, l, tm, final)
        kp.append(kvl[..., :LANES].reshape(n_seq, WINDOW, N_KV_HEADS, HEAD_DIM))
        vp.append(kvl[..., LANES:].reshape(n_seq, WINDOW, N_KV_HEADS, HEAD_DIM))
        cp.append(ctx)
        hrp.append(h_last[:, 0, :N_STATE].reshape(n_seq, SSM_GROUPS, SSM_STATE))
        hip.append(h_last[:, 0, N_STATE:].reshape(n_seq, SSM_GROUPS, SSM_STATE))

        zq_s, _, zc_s, us, kvn = _inproj(xs, g_mix, w_in_q, w_in_rest, l, n_dec, 1)
        kvn = kvn.reshape(n_dec, V_END - Q_END)
        zq4 = (zq_s * jnp.asarray(scale, BF16)).reshape(n_dec, KV_REP, 1, LANES)
        q3 = jnp.where(own_lane, zq4, jnp.zeros((), BF16)).reshape(n_dec * N_HEADS, LANES)
        kn3 = kvn[:, :LANES].reshape(n_dec, 1, LANES)
        vn3 = kvn[:, LANES:].reshape(n_dec, 1, LANES)
        o3, k_all, v_all, conv_s, nct, ssm_s, nhr, nhi = _dec_mix(
            q3, kn3, vn3, kvn, ck, cv, zc_s, cct, us, hr, hi, sinks_dec, conv_dw_w, cb3, lg3, lb3,
            bmat, cmat, a_step, d3, gw_b, gb3, k_all, v_all, l)
        xs = _tail(xs, g_ffn,
                   [(o3.reshape(n_dec, N_HEADS * LANES), decspec(N_HEADS * LANES), wo_dec),
                    (conv_s, decspec(CONV_CH), wo_conv),
                    (ssm_s, decspec(SSM_CH), wo_ssm)],
                   wg_b, wu_b, wd_b, l, n_dec, final)
        cs.append(jnp.transpose(nct, (1, 0, 2)))
        hrs.append(nhr.reshape(n_dec, SSM_GROUPS, SSM_STATE))
        his.append(nhi.reshape(n_dec, SSM_GROUPS, SSM_STATE))

    return (xp.reshape(n_seq, seq_len, D_MODEL), xs.reshape(n_dec, 1, D_MODEL),
            jnp.stack(kp), jnp.stack(vp), jnp.stack(cp), jnp.stack(hrp), jnp.stack(hip),
            cache_out(k_all), cache_out(v_all), jnp.stack(cs), jnp.stack(hrs), jnp.stack(his))
```

```python
import functools
import math

import jax
import jax.numpy as jnp
from jax import lax
from jax.experimental import pallas as pl
from jax.experimental.pallas import tpu as pltpu

D_MODEL = 1024
DEPTH = 4
HEAD_DIM = 64
ATTN_WIDTH = 512
N_HEADS = 8
N_KV_HEADS = 2
KV_REP = 4
WINDOW = 128
CONV_CH = 256
CONV_WIDTH = 31
SSM_CH = 256
SSM_GROUP = 16
SSM_GROUPS = 16
SSM_STATE = 64
D_FF = 2816
EPS = 1e-6

Q_END = ATTN_WIDTH
K_END = Q_END + N_KV_HEADS * HEAD_DIM
V_END = K_END + N_KV_HEADS * HEAD_DIM
C_END = V_END + 2 * CONV_CH
IN_COLS = C_END + SSM_CH

N_STATE = SSM_GROUPS * SSM_STATE
LANES = 128
SSM_CHUNK = 8
CHUNK_COLS = SSM_CHUNK * SSM_CH
SSM_PARTS = 4
PART_CH = SSM_CH // SSM_PARTS
PART_GROUPS = PART_CH // SSM_GROUP
PART_COLS = SSM_CHUNK * PART_CH
PART_STATE = PART_GROUPS * SSM_STATE
NEG = -1e30

TM_ROWS = 512
TS_ROWS = 4096
TI_ROWS = 2048
FF_CHUNK = 256
OUT_CHUNK = 1024
DEC_BLOCK = 16
CONV_ROWS = 256
ATTN_BLOCKS = 4
VMEM_LIMIT = 56 * 1024 * 1024

F32 = jnp.float32
BF16 = jnp.bfloat16


def _params(n_axes, flags=None):
    return pltpu.CompilerParams(dimension_semantics=("arbitrary",) * n_axes,
                                vmem_limit_bytes=VMEM_LIMIT, flags=flags)


def _resident(shape, index_map):
    return pl.BlockSpec(shape, index_map, pipeline_mode=pl.Buffered(1))


def _rms(x, g):
    return x * lax.rsqrt(jnp.mean(x * x, -1, keepdims=True) + EPS) * g


def _sigmoid(x):
    return 1.0 / (1.0 + jnp.exp(-x))


def _gelu_tanh(x):
    c = math.sqrt(2.0 / math.pi)
    return 0.5 * x * (1.0 + jnp.tanh(c * (x + 0.044715 * (x * x * x))))


def _dot(a, b):
    return jnp.dot(a, b, preferred_element_type=F32)


def _inproj_body(x_ref, g_ref, wq_ref, wr_ref, q_ref, kv_ref, c_ref, u_ref, kvl_ref):
    h = _rms(x_ref[...], g_ref[...]).astype(BF16)
    q_ref[...] = _dot(h, wq_ref[...]).astype(BF16)
    zr = _dot(h, wr_ref[...])
    kv = zr[:, :V_END - Q_END]
    kv_ref[...] = kv.astype(BF16)
    kvl_ref[...] = kv[kv.shape[0] - WINDOW:, :]
    c_ref[...] = zr[:, V_END - Q_END:C_END - Q_END]
    u_ref[0] = zr[:, C_END - Q_END:C_END - Q_END + LANES]
    u_ref[1] = zr[:, C_END - Q_END + LANES:]


def _inproj(x, g3, w_q, w_rest, layer, tm, n_seq):
    rows = x.shape[0]
    tiles_per_seq = rows // n_seq // tm
    kv_w = V_END - Q_END
    return pl.pallas_call(
        _inproj_body,
        grid=(rows // tm,),
        in_specs=[
            pl.BlockSpec((tm, D_MODEL), lambda i: (i, 0)),
            pl.BlockSpec((None, 1, D_MODEL), lambda i: (layer, 0, 0)),
            _resident((None, D_MODEL, Q_END), lambda i: (layer, 0, 0)),
            _resident((None, D_MODEL, IN_COLS - Q_END), lambda i: (layer, 0, 0)),
        ],
        out_specs=[
            pl.BlockSpec((tm, Q_END), lambda i: (i, 0)),
            pl.BlockSpec((tm, kv_w), lambda i: (i, 0)),
            pl.BlockSpec((tm, 2 * CONV_CH), lambda i: (i, 0)),
            pl.BlockSpec((2, tm, LANES), lambda i: (0, i, 0)),
            pl.BlockSpec((None, WINDOW, kv_w), lambda i: (i // tiles_per_seq, 0, 0)),
        ],
        out_shape=[jax.ShapeDtypeStruct((rows, Q_END), BF16),
                   jax.ShapeDtypeStruct((rows, kv_w), BF16),
                   jax.ShapeDtypeStruct((rows, 2 * CONV_CH), F32),
                   jax.ShapeDtypeStruct((2, rows, LANES), F32),
                   jax.ShapeDtypeStruct((n_seq, WINDOW, kv_w), F32)],
        compiler_params=_params(1),
        name="inproj",
    )(x, g3, w_q, w_rest)


def _tail_steps(x, acts, g_ref, wg_ref, wu_ref, wd_ref, gf_ref, o_ref, x1_s, hf_s, act_s):
    x1 = x
    for act, w_ref in acts:
        x1 = x1 + _dot(act, w_ref[...])
    hf_s[...] = _rms(x1, g_ref[...]).astype(BF16)
    x1_s[...] = x1
    yield
    for c in range(0, D_FF, FF_CHUNK):
        gate = _dot(hf_s[...], wg_ref[:, c:c + FF_CHUNK])
        up = _dot(hf_s[...], wu_ref[:, c:c + FF_CHUNK])
        act_s[:, c:c + FF_CHUNK] = (gate * _sigmoid(gate) * up).astype(BF16)
        yield
    for n in range(0, D_MODEL, OUT_CHUNK):
        o_ref[:, n:n + OUT_CHUNK] = x1_s[:, n:n + OUT_CHUNK] + _dot(act_s[...], wd_ref[:, n:n + OUT_CHUNK])
        yield
    if gf_ref is not None:
        o_ref[...] = _rms(o_ref[...], gf_ref[...])


def _tail_body(n_parts, final, *refs):
    x_ref, g_ref = refs[0], refs[1]
    parts = refs[2:2 + 2 * n_parts]
    wg_ref, wu_ref, wd_ref = refs[2 + 2 * n_parts:5 + 2 * n_parts]
    gf_ref = refs[5 + 2 * n_parts] if final else None
    o_ref, x1_s, hf_s, act_s = refs[-4:]
    acts = []
    for p in range(n_parts):
        act_ref = parts[2 * p]
        if len(act_ref.shape) == 3:
            act = jnp.concatenate([act_ref[i] for i in range(act_ref.shape[0])], -1)
        else:
            act = act_ref[...]
        acts.append((act.astype(BF16), parts[2 * p + 1]))
    for _ in _tail_steps(x_ref[...], acts, g_ref, wg_ref, wu_ref, wd_ref, gf_ref, o_ref, x1_s, hf_s, act_s):
        pass


def _tail(x, g3, parts, wg, wu, wd, layer, tm, final_g=None):
    rows = x.shape[0]
    final = final_g is not None
    in_specs = [pl.BlockSpec((tm, D_MODEL), lambda i: (i, 0)),
                pl.BlockSpec((None, 1, D_MODEL), lambda i: (layer, 0, 0))]
    args = [x, g3]
    for act, spec, w in parts:
        in_specs.append(spec)
        in_specs.append(pl.BlockSpec((None,) + w.shape[1:], lambda i: (layer, 0, 0)))
        args += [act, w]
    in_specs += [
        _resident((None, D_MODEL, D_FF), lambda i: (layer, 0, 0)),
        _resident((None, D_MODEL, D_FF), lambda i: (layer, 0, 0)),
        _resident((None, D_FF, D_MODEL), lambda i: (layer, 0, 0)),
    ]
    args += [wg, wu, wd]
    if final:
        in_specs.append(pl.BlockSpec((1, D_MODEL), lambda i: (0, 0)))
        args.append(final_g)
    return pl.pallas_call(
        functools.partial(_tail_body, len(parts), final),
        grid=(rows // tm,),
        in_specs=in_specs,
        out_specs=pl.BlockSpec((tm, D_MODEL), lambda i: (i, 0)),
        out_shape=jax.ShapeDtypeStruct((rows, D_MODEL), F32),
        scratch_shapes=[pltpu.VMEM((tm, D_MODEL), F32), pltpu.VMEM((tm, D_MODEL), BF16),
                        pltpu.VMEM((tm, D_FF), BF16)],
        compiler_params=_params(1),
        name="tail",
    )(*args)


def _expand_ssm_operands(k1_ref, m2_ref, n4_ref, w1_ref, w2_ref, w4_ref):
    compact = SSM_CHUNK * SSM_GROUP
    col = lax.broadcasted_iota(jnp.int32, (SSM_GROUP, PART_COLS), 1)
    col_chan_group = (col % PART_CH) // SSM_GROUP
    col_state_group = (col % PART_STATE) // SSM_STATE
    e_row = lax.broadcasted_iota(jnp.int32, (compact, PART_COLS), 0)
    e_col = lax.broadcasted_iota(jnp.int32, (compact, PART_COLS), 1)
    spread = ((e_row // SSM_GROUP == e_col // PART_CH) & (e_row % SSM_GROUP == e_col % SSM_GROUP))
    spread = jnp.where(spread, 1.0, 0.0).astype(BF16)
    rows = lax.broadcasted_iota(jnp.int32, (LANES, PART_COLS), 0)
    cols = lax.broadcasted_iota(jnp.int32, (LANES, PART_COLS), 1)
    for q in range(SSM_PARTS):
        for l in range(SSM_CHUNK):
            k1 = k1_ref[q, l * SSM_GROUP:(l + 1) * SSM_GROUP, :]
            m2 = m2_ref[q, l * SSM_GROUP:(l + 1) * SSM_GROUP, :]
            for g in range(PART_GROUPS):
                r0 = l * PART_CH + g * SSM_GROUP
                w1_ref[q, r0:r0 + SSM_GROUP, :] = jnp.where(col_chan_group == g, k1, 0.0).astype(BF16)
                w2_ref[q, r0:r0 + SSM_GROUP, :] = jnp.where(col_state_group == g, m2, 0.0).astype(BF16)
        for b in range(2 * PART_STATE // LANES):
            full = _dot(n4_ref[q, b * LANES:(b + 1) * LANES, :].astype(BF16), spread)
            row_group = ((rows + b * LANES) % PART_STATE) // SSM_STATE
            keep = row_group == (cols % PART_CH) // SSM_GROUP
            w4_ref[q, b * LANES:(b + 1) * LANES, :] = jnp.where(keep, full, 0.0).astype(BF16)


def _ssm_body(n_chunks, u_ref, k1_ref, m2_ref, n4_ref, al_ref, d_ref, gw_ref, gb_ref,
              o_ref, hl_ref, x_s, gh_s, hc_s, w1_ref, w2_ref, w4_ref):
    @pl.when((pl.program_id(0) == 0) & (pl.program_id(1) == 0))
    def _():
        _expand_ssm_operands(k1_ref, m2_ref, n4_ref, w1_ref, w2_ref, w4_ref)

    @pl.when(pl.program_id(1) == 0)
    def _():
        hc_s[...] = jnp.zeros_like(hc_s)

    n_slabs = SSM_CH // LANES

    def steps(l, o):
        return u_ref[o, pl.ds(l, n_chunks, stride=SSM_CHUNK), :]

    low = lax.broadcasted_iota(jnp.int32, (n_chunks, LANES), 1) < PART_CH
    for o in range(n_slabs):
        for l in range(0, SSM_CHUNK, 2):
            a, b = steps(l, o), steps(l + 1, o)
            c0 = l * PART_CH
            first = jnp.where(low, a, pltpu.roll(b, PART_CH, 1))
            second = jnp.where(low, pltpu.roll(a, PART_CH, 1), b)
            x_s[:, 2 * o * PART_COLS + c0:2 * o * PART_COLS + c0 + LANES] = first.astype(BF16)
            x_s[:, (2 * o + 1) * PART_COLS + c0:(2 * o + 1) * PART_COLS + c0 + LANES] = second.astype(BF16)

    for q in range(SSM_PARTS):
        g = _dot(x_s[:, q * PART_COLS:(q + 1) * PART_COLS], w2_ref[q])
        gh_s[:, q * PART_STATE:(q + 1) * PART_STATE] = g[:, :PART_STATE]
        gh_s[:, N_STATE + q * PART_STATE:N_STATE + (q + 1) * PART_STATE] = g[:, PART_STATE:]

    a_re = al_ref[:, :N_STATE]
    a_im = al_ref[:, N_STATE:]

    def step(k, carry):
        h_re, h_im = carry
        g_re = gh_s[pl.ds(k, 1), :N_STATE]
        g_im = gh_s[pl.ds(k, 1), N_STATE:]
        gh_s[pl.ds(k, 1), :N_STATE] = h_re
        gh_s[pl.ds(k, 1), N_STATE:] = h_im
        return (a_re * h_re - a_im * h_im + g_re, a_re * h_im + a_im * h_re + g_im)

    h_re, h_im = lax.fori_loop(0, n_chunks, step, (hc_s[:, :N_STATE], hc_s[:, N_STATE:]))
    hc_s[:, :N_STATE] = h_re
    hc_s[:, N_STATE:] = h_im
    hl_ref[:, :N_STATE] = h_re
    hl_ref[:, N_STATE:] = h_im

    hb = gh_s[...].astype(BF16)
    hcat = [jnp.concatenate([hb[:, q * PART_STATE:(q + 1) * PART_STATE],
                             hb[:, N_STATE + q * PART_STATE:N_STATE + (q + 1) * PART_STATE]], -1)
            for q in range(SSM_PARTS)]
    width = 2 * LANES
    per_dot = width // PART_CH
    for l0 in range(0, SSM_CHUNK, per_dot):
        c0, k1 = l0 * PART_CH, (l0 + per_dot) * PART_CH
        ys = [_dot(x_s[:, q * PART_COLS:q * PART_COLS + k1], w1_ref[q, :k1, c0:c0 + width])
              + _dot(hcat[q], w4_ref[q, :, c0:c0 + width]) for q in range(SSM_PARTS)]
        for dl in range(per_dot):
            l = l0 + dl
            y = jnp.concatenate([ys[q][:, dl * PART_CH:(dl + 1) * PART_CH] for q in range(SSM_PARTS)], -1)
            y = _gelu_tanh(y + d_ref[...] * jnp.concatenate([steps(l, o) for o in range(n_slabs)], -1))
            gate = _dot(y.astype(BF16), gw_ref[...]) + gb_ref[...]
            out = y * _sigmoid(gate)
            for o in range(n_slabs):
                o_ref[o, pl.ds(l, n_chunks, stride=SSM_CHUNK), :] = out[:, o * LANES:(o + 1) * LANES]


def _ssm_prompt(u, k1, m2, n4, al, d3, gw, gb3, layer, n_seq, seq_len):
    ts = TS_ROWS
    n_tiles = seq_len // ts
    n_chunks = ts // SSM_CHUNK
    rows = u.shape[1]
    compact = SSM_CHUNK * SSM_GROUP
    return pl.pallas_call(
        functools.partial(_ssm_body, n_chunks),
        grid=(n_seq, n_tiles),
        in_specs=[
            pl.BlockSpec((SSM_CH // LANES, ts, LANES), lambda s, i: (0, s * n_tiles + i, 0)),
            pl.BlockSpec((None, SSM_PARTS, compact, PART_COLS), lambda s, i: (layer, 0, 0, 0)),
            pl.BlockSpec((None, SSM_PARTS, compact, 2 * PART_STATE), lambda s, i: (layer, 0, 0, 0)),
            pl.BlockSpec((None, SSM_PARTS, 2 * PART_STATE, compact), lambda s, i: (layer, 0, 0, 0)),
            pl.BlockSpec((None, 1, 2 * N_STATE), lambda s, i: (layer, 0, 0)),
            pl.BlockSpec((None, 1, SSM_CH), lambda s, i: (layer, 0, 0)),
            pl.BlockSpec((None, SSM_CH, SSM_CH), lambda s, i: (layer, 0, 0)),
            pl.BlockSpec((None, 1, SSM_CH), lambda s, i: (layer, 0, 0)),
        ],
        out_specs=[
            pl.BlockSpec((SSM_CH // LANES, ts, LANES), lambda s, i: (0, s * n_tiles + i, 0)),
            pl.BlockSpec((None, 1, 2 * N_STATE), lambda s, i: (s, 0, 0)),
        ],
        out_shape=[jax.ShapeDtypeStruct((SSM_CH // LANES, rows, LANES), F32),
                   jax.ShapeDtypeStruct((n_seq, 1, 2 * N_STATE), F32)],
        scratch_shapes=[
            pltpu.VMEM((n_chunks, CHUNK_COLS), BF16),
            pltpu.VMEM((n_chunks, 2 * N_STATE), F32),
            pltpu.VMEM((1, 2 * N_STATE), F32),
            pltpu.VMEM((SSM_PARTS, PART_COLS, PART_COLS), BF16),
            pltpu.VMEM((SSM_PARTS, PART_COLS, 2 * PART_STATE), BF16),
            pltpu.VMEM((SSM_PARTS, 2 * PART_STATE, PART_COLS), BF16),
        ],
        compiler_params=_params(2),
        name="ssm_prompt",
    )(u, k1, m2, n4, al, d3, gw, gb3)


def _mix_steps(tm, first_tile, q_ref, kv_ref, a_ref, gg_ref, sink_ref, cw_ref, cb_ref, lg_ref, lb_ref,
                 o_ref, ctx_ref, kv_s, u_s, ush_s, bias_s):
    pad = 32
    off = pad - (CONV_WIDTH - 1)
    sub = 8

    @pl.when(first_tile)
    def _():
        kv_s[0:WINDOW, :] = jnp.zeros((WINDOW, 2 * LANES), F32)
        u_s[0:pad, :] = jnp.zeros((pad, CONV_CH), F32)
        qi = lax.broadcasted_iota(jnp.int32, (WINDOW, 2 * WINDOW), 0)
        ki = lax.broadcasted_iota(jnp.int32, (WINDOW, 2 * WINDOW), 1)
        dist = qi - ki + WINDOW
        valid = (dist >= 0) & (dist < WINDOW)
        distf = dist.astype(F32)
        for g in range(N_KV_HEADS):
            for r in range(KV_REP):
                slope = 2.0 ** (-8.0 * (g * KV_REP + r + 1) / N_HEADS)
                bias_s[g, r * WINDOW:(r + 1) * WINDOW, :] = jnp.where(valid, -slope * distf, NEG)

    kv_s[WINDOW:, :] = kv_ref[...].astype(F32)
    u_s[pad:, :] = a_ref[...] * _sigmoid(gg_ref[...])
    for b in range(1, sub):
        ush_s[b - 1] = u_s[b:b + tm + pad - sub, :]
    ctx_ref[...] = u_s[tm + off:tm + pad, :]
    yield

    def conv_units():
        for r0 in range(0, tm, CONV_ROWS):
            acc = jnp.zeros((CONV_ROWS, CONV_CH), F32) + cb_ref[...]
            for j in range(CONV_WIDTH):
                a0, b = divmod(off + j, sub)
                lo = r0 + a0 * sub
                rows = u_s[lo:lo + CONV_ROWS, :] if b == 0 else ush_s[b - 1, lo:lo + CONV_ROWS, :]
                acc = acc + cw_ref[j:j + 1, :] * rows
            mu = jnp.mean(acc, -1, keepdims=True)
            cen = acc - mu
            var = jnp.mean(cen * cen, -1, keepdims=True)
            yn = cen * lax.rsqrt(var + EPS) * lg_ref[...] + lb_ref[...]
            o_ref[r0:r0 + CONV_ROWS, ATTN_WIDTH:] = (yn * _sigmoid(yn)).astype(BF16)
            yield

    scale = 1.0 / math.sqrt(HEAD_DIM)
    k_lane_group = lax.broadcasted_iota(jnp.int32, (2 * WINDOW, LANES), 1) // HEAD_DIM
    o_lane_group = lax.broadcasted_iota(jnp.int32, (KV_REP * WINDOW, LANES), 1) // HEAD_DIM
    key_in_prev = lax.broadcasted_iota(jnp.int32, (1, 2 * WINDOW), 1) < WINDOW
    no_prev_block = jnp.logical_and(key_in_prev, first_tile)
    sinks = [jnp.concatenate([jnp.broadcast_to(sink_ref[:, g * KV_REP + r:g * KV_REP + r + 1], (WINDOW, 1))
                              for r in range(KV_REP)], 0) for g in range(N_KV_HEADS)]

    def attention_units():
        n_blocks = tm // WINDOW
        for b0 in range(0, n_blocks, ATTN_BLOCKS):
            blocks = range(b0, min(b0 + ATTN_BLOCKS, n_blocks))
            pairs = [(blk, g) for blk in blocks for g in range(N_KV_HEADS)]
            kblk = {blk: kv_s[blk * WINDOW:(blk + 2) * WINDOW, 0:LANES] for blk in blocks}
            vblk = {blk: kv_s[blk * WINDOW:(blk + 2) * WINDOW, LANES:].astype(BF16) for blk in blocks}
            qs = {blk: (jnp.concatenate([q_ref[blk * WINDOW:(blk + 1) * WINDOW, r * LANES:(r + 1) * LANES]
                                         for r in range(KV_REP)], 0) * jnp.asarray(scale, BF16))
                  for blk in blocks}
            kg = {(blk, g): jnp.where(k_lane_group == g, kblk[blk], 0.0).astype(BF16) for blk, g in pairs}
            s = {(blk, g): lax.dot_general(qs[blk], kg[blk, g], (((1,), (1,)), ((), ())),
                                           preferred_element_type=F32) + bias_s[g] for blk, g in pairs}
            for g in range(N_KV_HEADS):
                if b0 == 0:
                    s[0, g] = jnp.where(no_prev_block, NEG, s[0, g])
            m = {k: jnp.maximum(jnp.max(s[k], -1, keepdims=True), sinks[k[1]]) for k in pairs}
            p = {k: jnp.exp(s[k] - m[k]) for k in pairs}
            denom = {k: jnp.sum(p[k], -1, keepdims=True) + jnp.exp(sinks[k[1]] - m[k]) for k in pairs}
            og = {k: _dot(p[k].astype(BF16), vblk[k[0]]) / denom[k] for k in pairs}
            for blk in blocks:
                o = jnp.where(o_lane_group == 0, og[blk, 0], og[blk, 1])
                for r in range(KV_REP):
                    o_ref[blk * WINDOW:(blk + 1) * WINDOW, r * LANES:(r + 1) * LANES] = (
                        o[r * WINDOW:(r + 1) * WINDOW, :].astype(BF16))
            yield

    yield from conv_units()
    yield from attention_units()

    kv_s[0:WINDOW, :] = kv_s[tm:tm + WINDOW, :]
    u_s[0:pad, :] = u_s[tm:tm + pad, :]


def _mix_body(tm, *refs):
    for _ in _mix_steps(tm, pl.program_id(1) == 0, *refs):
        pass


def _mix_prompt(zq, zkv, zc, sinks3, cw, cb3, lg3, lb3, layer, n_seq, seq_len):
    tm = TM_ROWS
    n_tiles = seq_len // tm
    rows = n_seq * seq_len
    row = lambda s, i: s * n_tiles + i
    vec = lambda width: pl.BlockSpec((None, 1, width), lambda s, i: (layer, 0, 0))
    return pl.pallas_call(
        functools.partial(_mix_body, tm),
        grid=(n_seq, n_tiles),
        in_specs=[
            pl.BlockSpec((tm, ATTN_WIDTH), lambda s, i: (row(s, i), 0)),
            pl.BlockSpec((tm, 2 * LANES), lambda s, i: (row(s, i), 0)),
            pl.BlockSpec((tm, CONV_CH), lambda s, i: (row(s, i), 0)),
            pl.BlockSpec((tm, CONV_CH), lambda s, i: (row(s, i), 1)),
            vec(N_HEADS),
            pl.BlockSpec((None, CONV_WIDTH, CONV_CH), lambda s, i: (layer, 0, 0)),
            vec(CONV_CH), vec(CONV_CH), vec(CONV_CH),
        ],
        out_specs=[
            pl.BlockSpec((tm, V_END), lambda s, i: (row(s, i), 0)),
            pl.BlockSpec((None, CONV_WIDTH - 1, CONV_CH), lambda s, i: (s, 0, 0)),
        ],
        out_shape=[jax.ShapeDtypeStruct((rows, V_END), BF16),
                   jax.ShapeDtypeStruct((n_seq, CONV_WIDTH - 1, CONV_CH), F32)],
        scratch_shapes=[pltpu.VMEM((tm + WINDOW, 2 * LANES), F32),
                        pltpu.VMEM((tm + 32, CONV_CH), F32),
                        pltpu.VMEM((7, tm + 24, CONV_CH), F32),
                        pltpu.VMEM((N_KV_HEADS, KV_REP * WINDOW, 2 * WINDOW), F32)],
        compiler_params=_params(2),
        name="mix_prompt",
    )(zq, zkv, zc, zc, sinks3, cw, cb3, lg3, lb3)


def _dec_body(q_ref, kn_ref, vn_ref, k2_ref, v2_ref, ck_ref, cv_ref, a_ref, gg_ref, cc_ref, u_ref, hr_ref, hi_ref,
              sink_ref, cw_ref, cb_ref, lg_ref, lb_ref, bm_ref, cm_ref, ab_ref, d_ref, gw_ref, gb_ref,
              kall_ref, vall_ref, o_ref, ok_ref, ov_ref, co_ref, oc_ref, so_ref, or_ref, oi_ref):
    nb = DEC_BLOCK
    win = ck_ref.shape[2]

    q3 = q_ref[...].reshape(nb, N_HEADS, LANES)
    kn = kn_ref[...]
    vn = vn_ref[...]
    ck = ck_ref[...]
    cv = cv_ref[...]
    s = jnp.einsum("nsc,ncj->nsj", q3, ck.astype(BF16), preferred_element_type=F32)
    si = lax.broadcasted_iota(jnp.int32, (N_HEADS, win), 0)
    ji = lax.broadcasted_iota(jnp.int32, (N_HEADS, win), 1)
    head = (si % 2) * KV_REP + si // 2
    slope = jnp.zeros((N_HEADS, win), F32)
    for h in range(N_HEADS):
        slope = jnp.where(head == h, 2.0 ** (-8.0 * (h + 1) / N_HEADS), slope)
    dist = win - ji
    bias = jnp.where(dist < WINDOW, -slope * dist.astype(F32), NEG)
    s = s + bias[None]
    s_new = jnp.sum(q3.astype(F32) * kn.astype(BF16).astype(F32), -1, keepdims=True)
    sink = sink_ref[...][None]
    m = jnp.maximum(jnp.maximum(jnp.max(s, -1, keepdims=True), s_new), sink)
    p = jnp.exp(s - m)
    p_new = jnp.exp(s_new - m)
    denom = jnp.sum(p, -1, keepdims=True) + p_new + jnp.exp(sink - m)
    o = jnp.einsum("nsj,ncj->nsc", p.astype(BF16), cv.astype(BF16), preferred_element_type=F32)
    o = o + p_new.astype(BF16).astype(F32) * vn.astype(BF16).astype(F32)
    o_ref[...] = (o / denom).reshape(nb * N_HEADS, LANES)

    last = lax.broadcasted_iota(jnp.int32, (LANES, win), 1) == win - 1
    fill = jnp.zeros((LANES - nb, LANES), F32)
    knt = jnp.concatenate([k2_ref[...], fill], 0).T
    vnt = jnp.concatenate([v2_ref[...], fill], 0).T
    for i in range(nb):
        ok_ref[i] = jnp.where(last, pltpu.roll(knt, win - 1 - i, 1), pltpu.roll(ck[i], win - 1, 1))
        ov_ref[i] = jnp.where(last, pltpu.roll(vnt, win - 1 - i, 1), pltpu.roll(cv[i], win - 1, 1))

    @pl.when(pl.program_id(0) == 0)
    def _():
        u = a_ref[...] * _sigmoid(gg_ref[...])
        acc = cb_ref[...] + cw_ref[CONV_WIDTH - 1:CONV_WIDTH, :] * u
        for j in range(CONV_WIDTH - 1):
            acc = acc + cw_ref[j:j + 1, :] * cc_ref[j]
        mu = jnp.mean(acc, -1, keepdims=True)
        cen = acc - mu
        var = jnp.mean(cen * cen, -1, keepdims=True)
        yn = cen * lax.rsqrt(var + EPS) * lg_ref[...] + lb_ref[...]
        co_ref[...] = yn * _sigmoid(yn)
        for j in range(CONV_WIDTH - 2):
            oc_ref[j] = cc_ref[j + 1]
        oc_ref[CONV_WIDTH - 2] = u

        us = jnp.concatenate([u_ref[0], u_ref[1]], -1)
        bu = _dot(us.astype(BF16), bm_ref[...])
        a_re = ab_ref[:, :N_STATE]
        a_im = ab_ref[:, N_STATE:]
        h_re = hr_ref[...]
        h_im = hi_ref[...]
        n_re = a_re * h_re - a_im * h_im + bu[:, :N_STATE]
        n_im = a_re * h_im + a_im * h_re + bu[:, N_STATE:]
        or_ref[...] = n_re
        oi_ref[...] = n_im
        hcat = jnp.concatenate([n_re, n_im], -1).astype(BF16)
        y = _gelu_tanh(_dot(hcat, cm_ref[...]) + d_ref[...] * us)
        gate = _dot(y.astype(BF16), gw_ref[...]) + gb_ref[...]
        so_ref[...] = y * _sigmoid(gate)


def _dec_mix(q3, kn3, vn3, kvn, ck, cv, zc, cct, u, hr, hi, sinks3, cw, cb3, lg3, lb3,
             bmat, cmat, ab3, d3, gw, gb3, k_all, v_all, layer):
    n = kn3.shape[0]
    win = ck.shape[3]
    nb = DEC_BLOCK
    vec = lambda width: pl.BlockSpec((None, 1, width), lambda i: (layer, 0, 0))
    mat = lambda r, c: pl.BlockSpec((None, r, c), lambda i: (layer, 0, 0))
    cache = pl.BlockSpec((None, nb, LANES, win), lambda i: (layer, i, 0, 0))
    ctx = pl.BlockSpec((None, CONV_WIDTH - 1, n, CONV_CH), lambda i: (layer, 0, 0, 0))
    state = pl.BlockSpec((None, n, N_STATE), lambda i: (layer, 0, 0))
    allrows = lambda width: pl.BlockSpec((n, width), lambda i: (0, 0))
    new3 = pl.BlockSpec((nb, 1, LANES), lambda i: (i, 0, 0))
    return pl.pallas_call(
        _dec_body,
        grid=(n // nb,),
        in_specs=[
            pl.BlockSpec((nb * N_HEADS, LANES), lambda i: (i, 0)), new3, new3,
            pl.BlockSpec((nb, LANES), lambda i: (i, 0)),
            pl.BlockSpec((nb, LANES), lambda i: (i, 1)),
            cache, cache,
            pl.BlockSpec((n, CONV_CH), lambda i: (0, 0)),
            pl.BlockSpec((n, CONV_CH), lambda i: (0, 1)),
            ctx,
            pl.BlockSpec((2, n, LANES), lambda i: (0, 0, 0)),
            state, state,
            mat(N_HEADS, 1), mat(CONV_WIDTH, CONV_CH), vec(CONV_CH), vec(CONV_CH), vec(CONV_CH),
            mat(SSM_CH, 2 * N_STATE), mat(2 * N_STATE, SSM_CH), vec(2 * N_STATE), vec(SSM_CH),
            mat(SSM_CH, SSM_CH), vec(SSM_CH),
            pl.BlockSpec(memory_space=pl.ANY), pl.BlockSpec(memory_space=pl.ANY),
        ],
        out_specs=[
            pl.BlockSpec((nb * N_HEADS, LANES), lambda i: (i, 0)),
            cache, cache,
            allrows(CONV_CH),
            pl.BlockSpec((CONV_WIDTH - 1, n, CONV_CH), lambda i: (0, 0, 0)),
            allrows(SSM_CH), allrows(N_STATE), allrows(N_STATE),
        ],
        out_shape=[
            jax.ShapeDtypeStruct((n * N_HEADS, LANES), F32),
            jax.ShapeDtypeStruct(k_all.shape, F32),
            jax.ShapeDtypeStruct(v_all.shape, F32),
            jax.ShapeDtypeStruct((n, CONV_CH), F32),
            jax.ShapeDtypeStruct((CONV_WIDTH - 1, n, CONV_CH), F32),
            jax.ShapeDtypeStruct((n, SSM_CH), F32),
            jax.ShapeDtypeStruct((n, N_STATE), F32),
            jax.ShapeDtypeStruct((n, N_STATE), F32),
        ],
        input_output_aliases={24: 1, 25: 2},
        compiler_params=_params(1),
        name="dec_mix",
    )(q3, kn3, vn3, kvn, kvn, ck, cv, zc, zc, cct, u, hr, hi, sinks3, cw, cb3, lg3, lb3,
      bmat, cmat, ab3, d3, gw, gb3, k_all, v_all)


def _ssm_operands(a_re, a_im, log_dt, b_re, b_im, c_re, c_im):
    hi = lax.Precision.HIGHEST
    l_n, g_n, p_n, c_n = SSM_CHUNK, SSM_GROUPS, SSM_STATE, SSM_GROUP
    dt = jnp.exp(log_dt)[:, None]
    lam_re, lam_im = a_re * dt, a_im * dt
    steps = jnp.arange(l_n + 1, dtype=F32)[:, None, None]
    mag = jnp.exp(steps * lam_re)
    pw_re, pw_im = mag * jnp.cos(steps * lam_im), mag * jnp.sin(steps * lam_im)
    ab_re, ab_im = pw_re[1], pw_im[1]
    den = a_re * a_re + a_im * a_im
    q_re = ((ab_re - 1.0) * a_re + ab_im * a_im) / den
    q_im = (ab_im * a_re - (ab_re - 1.0) * a_im) / den
    bt_re, bt_im = jnp.transpose(b_re, (2, 0, 1)), jnp.transpose(b_im, (2, 0, 1))
    bb_re = q_re * bt_re - q_im * bt_im
    bb_im = q_re * bt_im + q_im * bt_re
    pl_re, pl_im = pw_re[:l_n, None], pw_im[:l_n, None]
    pb_re = pl_re * bb_re - pl_im * bb_im
    pb_im = pl_re * bb_im + pl_im * bb_re

    ktau = (jnp.einsum("gop,tcgp->tcgo", c_re, pb_re, precision=hi)
            - jnp.einsum("gop,tcgp->tcgo", c_im, pb_im, precision=hi)).reshape(l_n, c_n, SSM_CH)
    lag = jnp.arange(l_n)[None, :] - jnp.arange(l_n)[:, None]
    k1 = jnp.where((lag >= 0)[:, :, None, None], jnp.take(ktau, jnp.maximum(lag, 0), axis=0), 0.0)
    k1 = jnp.transpose(k1, (0, 2, 1, 3)).reshape(l_n * c_n, CHUNK_COLS)
    back = l_n - 1 - jnp.arange(l_n)
    m2 = jnp.concatenate([jnp.take(pb_re, back, axis=0).reshape(l_n, c_n, N_STATE),
                          jnp.take(pb_im, back, axis=0).reshape(l_n, c_n, N_STATE)], -1)
    m2 = m2.reshape(l_n * c_n, 2 * N_STATE)
    ct_re, ct_im = jnp.transpose(c_re, (0, 2, 1)), jnp.transpose(c_im, (0, 2, 1))
    pn_re = jnp.transpose(pw_re[1:], (1, 2, 0))[..., None]
    pn_im = jnp.transpose(pw_im[1:], (1, 2, 0))[..., None]
    n4_re = (ct_re[:, :, None, :] * pn_re - ct_im[:, :, None, :] * pn_im).reshape(N_STATE, l_n * c_n)
    n4_im = (ct_re[:, :, None, :] * pn_im + ct_im[:, :, None, :] * pn_re).reshape(N_STATE, l_n * c_n)
    n4 = jnp.concatenate([n4_re, -n4_im], 0)

    chan_group = jnp.arange(SSM_CH) // c_n
    state_group = (jnp.arange(2 * N_STATE) % N_STATE) // p_n
    bmat = jnp.where(chan_group[:, None] == state_group[None, :],
                     jnp.tile(m2[(l_n - 1) * c_n:], (g_n, 1)), 0.0)
    cc = jnp.concatenate([ct_re.reshape(N_STATE, c_n), -ct_im.reshape(N_STATE, c_n)], 0)
    cmat = jnp.where(state_group[:, None] == chan_group[None, :], jnp.tile(cc, (1, g_n)), 0.0)

    k1_h = jnp.transpose(k1.reshape(l_n * c_n, l_n, SSM_PARTS, PART_CH), (2, 0, 1, 3)).reshape(
        SSM_PARTS, l_n * c_n, PART_COLS)
    m2_h = jnp.transpose(m2.reshape(l_n * c_n, 2, SSM_PARTS, PART_STATE), (2, 0, 1, 3)).reshape(
        SSM_PARTS, l_n * c_n, 2 * PART_STATE)
    n4_h = jnp.transpose(n4.reshape(2, SSM_PARTS, PART_STATE, l_n * c_n), (1, 0, 2, 3)).reshape(
        SSM_PARTS, 2 * PART_STATE, l_n * c_n)

    flat = lambda re, im: jnp.concatenate([re.reshape(1, N_STATE), im.reshape(1, N_STATE)], -1)
    return (k1_h, m2_h, n4_h, bmat.astype(BF16), cmat.astype(BF16),
            flat(pw_re[l_n], pw_im[l_n]), flat(ab_re, ab_im))


def _decode_head_order():
    s = jnp.arange(N_HEADS)
    return (s % 2) * KV_REP + s // 2


def kernel(x_prompt, x_sample, cache_swa_k, cache_swa_v, cache_conv, state_ssm_re, state_ssm_im,
           norm_mix_g, w_in, attn_sinks, conv_dw_w, conv_dw_b, conv_ln_g, conv_ln_b,
           ssm_a_re, ssm_a_im, ssm_log_dt, ssm_b_re, ssm_b_im, ssm_c_re, ssm_c_im,
           ssm_d, ssm_glu_w, ssm_glu_b, w_out, norm_ffn_g, w_ff_gate, w_ff_up, w_ff_down,
           norm_final_g):
    n_seq, seq_len, _ = x_prompt.shape
    n_dec = x_sample.shape[0]
    win = cache_swa_k.shape[2]
    assert x_sample.shape[1] == 1 and win == WINDOW
    assert n_dec == WINDOW
    assert seq_len % TS_ROWS == 0 and n_dec % DEC_BLOCK == 0

    row3 = lambda v: v.reshape(DEPTH, 1, -1)
    g_mix, g_ffn = row3(norm_mix_g), row3(norm_ffn_g)
    sinks3 = row3(attn_sinks)
    order = _decode_head_order()
    sinks_dec = attn_sinks[:, order][:, :, None]
    cb3, lg3, lb3 = row3(conv_dw_b), row3(conv_ln_g), row3(conv_ln_b)
    d3, gb3 = row3(ssm_d), row3(ssm_glu_b)
    w_in_q = jnp.transpose(w_in[:, :, :Q_END].astype(BF16).reshape(DEPTH, D_MODEL, N_KV_HEADS, KV_REP, HEAD_DIM),
                           (0, 1, 3, 2, 4)).reshape(DEPTH, D_MODEL, Q_END)
    w_in_rest = w_in[:, :, Q_END:].astype(BF16)
    w_out_b = w_out.astype(BF16)
    wg_b, wu_b, wd_b = w_ff_gate.astype(BF16), w_ff_up.astype(BF16), w_ff_down.astype(BF16)
    gw_b = ssm_glu_w.astype(BF16)
    k1, m2, n4, bmat, cmat, a_chunk, a_step = jax.vmap(_ssm_operands)(
        ssm_a_re, ssm_a_im, ssm_log_dt, ssm_b_re, ssm_b_im, ssm_c_re, ssm_c_im)

    wo_heads = w_out_b[:, :ATTN_WIDTH].reshape(DEPTH, N_HEADS, HEAD_DIM, D_MODEL)[:, order]
    own = (jnp.arange(N_HEADS)[:, None] % 2) == jnp.arange(N_KV_HEADS)[None, :]
    wo_dec = jnp.where(own[None, :, :, None, None], wo_heads[:, :, None], 0).reshape(
        DEPTH, N_HEADS * LANES, D_MODEL)
    wo_attn = jnp.transpose(w_out_b[:, :ATTN_WIDTH].reshape(DEPTH, N_KV_HEADS, KV_REP, HEAD_DIM, D_MODEL),
                            (0, 2, 1, 3, 4)).reshape(DEPTH, ATTN_WIDTH, D_MODEL)
    wo_attn_conv = jnp.concatenate([wo_attn, w_out_b[:, ATTN_WIDTH:V_END]], 1)
    wo_conv = w_out_b[:, ATTN_WIDTH:V_END]
    wo_ssm = w_out_b[:, V_END:]

    ck = jnp.transpose(cache_swa_k, (0, 1, 3, 4, 2)).reshape(DEPTH, n_dec, LANES, win)
    cv = jnp.transpose(cache_swa_v, (0, 1, 3, 4, 2)).reshape(DEPTH, n_dec, LANES, win)
    cct = jnp.transpose(cache_conv, (0, 2, 1, 3))
    hr = state_ssm_re.reshape(DEPTH, n_dec, N_STATE)
    hi = state_ssm_im.reshape(DEPTH, n_dec, N_STATE)

    xp = x_prompt.reshape(n_seq * seq_len, D_MODEL)
    xs = x_sample.reshape(n_dec, D_MODEL)
    tm = TM_ROWS
    decspec = lambda width: pl.BlockSpec((n_dec, width), lambda i: (i, 0))
    scale = 1.0 / math.sqrt(HEAD_DIM)
    own_lane = ((jnp.arange(LANES) // HEAD_DIM)[None, None, None, :]
                == jnp.arange(N_KV_HEADS)[None, None, :, None])

    cache_out = lambda c: jnp.transpose(c.reshape(DEPTH, n_dec, N_KV_HEADS, HEAD_DIM, win), (0, 1, 4, 2, 3))
    k_all = jnp.zeros((DEPTH, n_dec, LANES, win), F32)
    v_all = jnp.zeros((DEPTH, n_dec, LANES, win), F32)
    kp, vp, cp, hrp, hip = [], [], [], [], []
    cs, hrs, his = [], [], []
    for l in range(DEPTH):
        final = norm_final_g.reshape(1, D_MODEL) if l == DEPTH - 1 else None

        zq, zkv, zc, u, kvl = _inproj(xp, g_mix, w_in_q, w_in_rest, l, TI_ROWS, n_seq)
        ssm, h_last = _ssm_prompt(u, k1, m2, n4, a_chunk, d3, gw_b, gb3, l, n_seq, seq_len)
        mix, ctx = _mix_prompt(zq, zkv, zc, sinks3, conv_dw_w, cb3, lg3, lb3, l, n_seq, seq_len)
        xp = _tail(xp, g_ffn,
                   [(mix, pl.BlockSpec((tm, V_END), lambda i: (i, 0)), wo_attn_conv),
                    (ssm, pl.BlockSpec((2, tm, LANES), lambda i: (0, i, 0)), wo_ssm)],
                   wg_b, wu_b, wd_b, l, tm, final)
        kp.append(kvl[..., :LANES].reshape(n_seq, WINDOW, N_KV_HEADS, HEAD_DIM))
        vp.append(kvl[..., LANES:].reshape(n_seq, WINDOW, N_KV_HEADS, HEAD_DIM))
        cp.append(ctx)
        hrp.append(h_last[:, 0, :N_STATE].reshape(n_seq, SSM_GROUPS, SSM_STATE))
        hip.append(h_last[:, 0, N_STATE:].reshape(n_seq, SSM_GROUPS, SSM_STATE))

        zq_s, _, zc_s, us, kvn = _inproj(xs, g_mix, w_in_q, w_in_rest, l, n_dec, 1)
        kvn = kvn.reshape(n_dec, V_END - Q_END)
        zq4 = (zq_s * jnp.asarray(scale, BF16)).reshape(n_dec, KV_REP, 1, LANES)
        q3 = jnp.where(own_lane, zq4, jnp.zeros((), BF16)).reshape(n_dec * N_HEADS, LANES)
        kn3 = kvn[:, :LANES].reshape(n_dec, 1, LANES)
        vn3 = kvn[:, LANES:].reshape(n_dec, 1, LANES)
        o3, k_all, v_all, conv_s, nct, ssm_s, nhr, nhi = _dec_mix(
            q3, kn3, vn3, kvn, ck, cv, zc_s, cct, us, hr, hi, sinks_dec, conv_dw_w, cb3, lg3, lb3,
            bmat, cmat, a_step, d3, gw_b, gb3, k_all, v_all, l)
        xs = _tail(xs, g_ffn,
                   [(o3.reshape(n_dec, N_HEADS * LANES), decspec(N_HEADS * LANES), wo_dec),
                    (conv_s, decspec(CONV_CH), wo_conv),
                    (ssm_s, decspec(SSM_CH), wo_ssm)],
                   wg_b, wu_b, wd_b, l, n_dec, final)
        cs.append(jnp.transpose(nct, (1, 0, 2)))
        hrs.append(nhr.reshape(n_dec, SSM_GROUPS, SSM_STATE))
        his.append(nhi.reshape(n_dec, SSM_GROUPS, SSM_STATE))

    return (xp.reshape(n_seq, seq_len, D_MODEL), xs.reshape(n_dec, 1, D_MODEL),
            jnp.stack(kp), jnp.stack(vp), jnp.stack(cp), jnp.stack(hrp), jnp.stack(hip),
            cache_out(k_all), cache_out(v_all), jnp.stack(cs), jnp.stack(hrs), jnp.stack(his))
```

```python
import functools
import math

import jax
import jax.numpy as jnp
from jax import lax
from jax.experimental import pallas as pl
from jax.experimental.pallas import tpu as pltpu

D_MODEL = 1024
DEPTH = 4
HEAD_DIM = 64
ATTN_WIDTH = 512
N_HEADS = 8
N_KV_HEADS = 2
KV_REP = 4
WINDOW = 128
CONV_CH = 256
CONV_WIDTH = 31
SSM_CH = 256
SSM_GROUP = 16
SSM_GROUPS = 16
SSM_STATE = 64
D_FF = 2816
EPS = 1e-6

Q_END = ATTN_WIDTH
K_END = Q_END + N_KV_HEADS * HEAD_DIM
V_END = K_END + N_KV_HEADS * HEAD_DIM
C_END = V_END + 2 * CONV_CH
IN_COLS = C_END + SSM_CH

N_STATE = SSM_GROUPS * SSM_STATE
LANES = 128
SSM_CHUNK = 8
CHUNK_COLS = SSM_CHUNK * SSM_CH
SSM_PARTS = 4
PART_CH = SSM_CH // SSM_PARTS
PART_GROUPS = PART_CH // SSM_GROUP
PART_COLS = SSM_CHUNK * PART_CH
PART_STATE = PART_GROUPS * SSM_STATE
NEG = -1e30

TM_ROWS = 512
TS_ROWS = 4096
TI_ROWS = 1024
FF_CHUNK = 256
OUT_CHUNK = 1024
DEC_BLOCK = 16
CONV_ROWS = 256
ATTN_BLOCKS = 4
VMEM_LIMIT = 56 * 1024 * 1024

F32 = jnp.float32
BF16 = jnp.bfloat16


def _params(n_axes, flags=None):
    return pltpu.CompilerParams(dimension_semantics=("arbitrary",) * n_axes,
                                vmem_limit_bytes=VMEM_LIMIT, flags=flags)


def _resident(shape, index_map):
    return pl.BlockSpec(shape, index_map, pipeline_mode=pl.Buffered(1))


def _rms(x, g):
    return x * lax.rsqrt(jnp.mean(x * x, -1, keepdims=True) + EPS) * g


def _sigmoid(x):
    return 1.0 / (1.0 + jnp.exp(-x))


def _gelu_tanh(x):
    c = math.sqrt(2.0 / math.pi)
    return 0.5 * x * (1.0 + jnp.tanh(c * (x + 0.044715 * (x * x * x))))


def _dot(a, b):
    return jnp.dot(a, b, preferred_element_type=F32)


def _inproj_body(x_ref, g_ref, wq_ref, wr_ref, q_ref, kv_ref, c_ref, u_ref, kvl_ref):
    h = _rms(x_ref[...], g_ref[...]).astype(BF16)
    q_ref[...] = _dot(h, wq_ref[...]).astype(BF16)
    zr = _dot(h, wr_ref[...])
    kv = zr[:, :V_END - Q_END]
    kv_ref[...] = kv.astype(BF16)
    kvl_ref[...] = kv[kv.shape[0] - WINDOW:, :]
    c_ref[...] = zr[:, V_END - Q_END:C_END - Q_END]
    u_ref[0] = zr[:, C_END - Q_END:C_END - Q_END + LANES]
    u_ref[1] = zr[:, C_END - Q_END + LANES:]


def _inproj(x, g3, w_q, w_rest, layer, tm, n_seq):
    rows = x.shape[0]
    tiles_per_seq = rows // n_seq // tm
    kv_w = V_END - Q_END
    return pl.pallas_call(
        _inproj_body,
        grid=(rows // tm,),
        in_specs=[
            pl.BlockSpec((tm, D_MODEL), lambda i: (i, 0)),
            pl.BlockSpec((None, 1, D_MODEL), lambda i: (layer, 0, 0)),
            _resident((None, D_MODEL, Q_END), lambda i: (layer, 0, 0)),
            _resident((None, D_MODEL, IN_COLS - Q_END), lambda i: (layer, 0, 0)),
        ],
        out_specs=[
            pl.BlockSpec((tm, Q_END), lambda i: (i, 0)),
            pl.BlockSpec((tm, kv_w), lambda i: (i, 0)),
            pl.BlockSpec((tm, 2 * CONV_CH), lambda i: (i, 0)),
            pl.BlockSpec((2, tm, LANES), lambda i: (0, i, 0)),
            pl.BlockSpec((None, WINDOW, kv_w), lambda i: (i // tiles_per_seq, 0, 0)),
        ],
        out_shape=[jax.ShapeDtypeStruct((rows, Q_END), BF16),
                   jax.ShapeDtypeStruct((rows, kv_w), BF16),
                   jax.ShapeDtypeStruct((rows, 2 * CONV_CH), F32),
                   jax.ShapeDtypeStruct((2, rows, LANES), F32),
                   jax.ShapeDtypeStruct((n_seq, WINDOW, kv_w), F32)],
        compiler_params=_params(1),
        name="inproj",
    )(x, g3, w_q, w_rest)


def _tail_steps(x, acts, g_ref, wg_ref, wu_ref, wd_ref, gf_ref, o_ref, x1_s, hf_s, act_s):
    x1 = x
    for act, w_ref in acts:
        x1 = x1 + _dot(act, w_ref[...])
    hf_s[...] = _rms(x1, g_ref[...]).astype(BF16)
    x1_s[...] = x1
    yield
    for c in range(0, D_FF, FF_CHUNK):
        gate = _dot(hf_s[...], wg_ref[:, c:c + FF_CHUNK])
        up = _dot(hf_s[...], wu_ref[:, c:c + FF_CHUNK])
        act_s[:, c:c + FF_CHUNK] = (gate * _sigmoid(gate) * up).astype(BF16)
        yield
    for n in range(0, D_MODEL, OUT_CHUNK):
        o_ref[:, n:n + OUT_CHUNK] = x1_s[:, n:n + OUT_CHUNK] + _dot(act_s[...], wd_ref[:, n:n + OUT_CHUNK])
        yield
    if gf_ref is not None:
        o_ref[...] = _rms(o_ref[...], gf_ref[...])


def _tail_body(n_parts, final, *refs):
    x_ref, g_ref = refs[0], refs[1]
    parts = refs[2:2 + 2 * n_parts]
    wg_ref, wu_ref, wd_ref = refs[2 + 2 * n_parts:5 + 2 * n_parts]
    gf_ref = refs[5 + 2 * n_parts] if final else None
    o_ref, x1_s, hf_s, act_s = refs[-4:]
    acts = []
    for p in range(n_parts):
        act_ref = parts[2 * p]
        if len(act_ref.shape) == 3:
            act = jnp.concatenate([act_ref[i] for i in range(act_ref.shape[0])], -1)
        else:
            act = act_ref[...]
        acts.append((act.astype(BF16), parts[2 * p + 1]))
    for _ in _tail_steps(x_ref[...], acts, g_ref, wg_ref, wu_ref, wd_ref, gf_ref, o_ref, x1_s, hf_s, act_s):
        pass


def _tail(x, g3, parts, wg, wu, wd, layer, tm, final_g=None):
    rows = x.shape[0]
    final = final_g is not None
    in_specs = [pl.BlockSpec((tm, D_MODEL), lambda i: (i, 0)),
                pl.BlockSpec((None, 1, D_MODEL), lambda i: (layer, 0, 0))]
    args = [x, g3]
    for act, spec, w in parts:
        in_specs.append(spec)
        in_specs.append(pl.BlockSpec((None,) + w.shape[1:], lambda i: (layer, 0, 0)))
        args += [act, w]
    in_specs += [
        _resident((None, D_MODEL, D_FF), lambda i: (layer, 0, 0)),
        _resident((None, D_MODEL, D_FF), lambda i: (layer, 0, 0)),
        _resident((None, D_FF, D_MODEL), lambda i: (layer, 0, 0)),
    ]
    args += [wg, wu, wd]
    if final:
        in_specs.append(pl.BlockSpec((1, D_MODEL), lambda i: (0, 0)))
        args.append(final_g)
    return pl.pallas_call(
        functools.partial(_tail_body, len(parts), final),
        grid=(rows // tm,),
        in_specs=in_specs,
        out_specs=pl.BlockSpec((tm, D_MODEL), lambda i: (i, 0)),
        out_shape=jax.ShapeDtypeStruct((rows, D_MODEL), F32),
        scratch_shapes=[pltpu.VMEM((tm, D_MODEL), F32), pltpu.VMEM((tm, D_MODEL), BF16),
                        pltpu.VMEM((tm, D_FF), BF16)],
        compiler_params=_params(1),
        name="tail",
    )(*args)


def _expand_ssm_operands(k1_ref, m2_ref, n4_ref, w1_ref, w2_ref, w4_ref):
    compact = SSM_CHUNK * SSM_GROUP
    col = lax.broadcasted_iota(jnp.int32, (SSM_GROUP, PART_COLS), 1)
    col_chan_group = (col % PART_CH) // SSM_GROUP
    col_state_group = (col % PART_STATE) // SSM_STATE
    e_row = lax.broadcasted_iota(jnp.int32, (compact, PART_COLS), 0)
    e_col = lax.broadcasted_iota(jnp.int32, (compact, PART_COLS), 1)
    spread = ((e_row // SSM_GROUP == e_col // PART_CH) & (e_row % SSM_GROUP == e_col % SSM_GROUP))
    spread = jnp.where(spread, 1.0, 0.0).astype(BF16)
    rows = lax.broadcasted_iota(jnp.int32, (LANES, PART_COLS), 0)
    cols = lax.broadcasted_iota(jnp.int32, (LANES, PART_COLS), 1)
    for q in range(SSM_PARTS):
        for l in range(SSM_CHUNK):
            k1 = k1_ref[q, l * SSM_GROUP:(l + 1) * SSM_GROUP, :]
            m2 = m2_ref[q, l * SSM_GROUP:(l + 1) * SSM_GROUP, :]
            for g in range(PART_GROUPS):
                r0 = l * PART_CH + g * SSM_GROUP
                w1_ref[q, r0:r0 + SSM_GROUP, :] = jnp.where(col_chan_group == g, k1, 0.0).astype(BF16)
                w2_ref[q, r0:r0 + SSM_GROUP, :] = jnp.where(col_state_group == g, m2, 0.0).astype(BF16)
        for b in range(2 * PART_STATE // LANES):
            full = _dot(n4_ref[q, b * LANES:(b + 1) * LANES, :].astype(BF16), spread)
            row_group = ((rows + b * LANES) % PART_STATE) // SSM_STATE
            keep = row_group == (cols % PART_CH) // SSM_GROUP
            w4_ref[q, b * LANES:(b + 1) * LANES, :] = jnp.where(keep, full, 0.0).astype(BF16)


def _ssm_body(n_chunks, u_ref, k1_ref, m2_ref, n4_ref, al_ref, d_ref, gw_ref, gb_ref,
              o_ref, hl_ref, x_s, gh_s, hc_s, w1_ref, w2_ref, w4_ref):
    @pl.when((pl.program_id(0) == 0) & (pl.program_id(1) == 0))
    def _():
        _expand_ssm_operands(k1_ref, m2_ref, n4_ref, w1_ref, w2_ref, w4_ref)

    @pl.when(pl.program_id(1) == 0)
    def _():
        hc_s[...] = jnp.zeros_like(hc_s)

    n_slabs = SSM_CH // LANES

    def steps(l, o):
        return u_ref[o, pl.ds(l, n_chunks, stride=SSM_CHUNK), :]

    low = lax.broadcasted_iota(jnp.int32, (n_chunks, LANES), 1) < PART_CH
    for o in range(n_slabs):
        for l in range(0, SSM_CHUNK, 2):
            a, b = steps(l, o), steps(l + 1, o)
            c0 = l * PART_CH
            first = jnp.where(low, a, pltpu.roll(b, PART_CH, 1))
            second = jnp.where(low, pltpu.roll(a, PART_CH, 1), b)
            x_s[:, 2 * o * PART_COLS + c0:2 * o * PART_COLS + c0 + LANES] = first.astype(BF16)
            x_s[:, (2 * o + 1) * PART_COLS + c0:(2 * o + 1) * PART_COLS + c0 + LANES] = second.astype(BF16)

    for q in range(SSM_PARTS):
        g = _dot(x_s[:, q * PART_COLS:(q + 1) * PART_COLS], w2_ref[q])
        gh_s[:, q * PART_STATE:(q + 1) * PART_STATE] = g[:, :PART_STATE]
        gh_s[:, N_STATE + q * PART_STATE:N_STATE + (q + 1) * PART_STATE] = g[:, PART_STATE:]

    a_re = al_ref[:, :N_STATE]
    a_im = al_ref[:, N_STATE:]

    def step(k, carry):
        h_re, h_im = carry
        g_re = gh_s[pl.ds(k, 1), :N_STATE]
        g_im = gh_s[pl.ds(k, 1), N_STATE:]
        gh_s[pl.ds(k, 1), :N_STATE] = h_re
        gh_s[pl.ds(k, 1), N_STATE:] = h_im
        return (a_re * h_re - a_im * h_im + g_re, a_re * h_im + a_im * h_re + g_im)

    h_re, h_im = lax.fori_loop(0, n_chunks, step, (hc_s[:, :N_STATE], hc_s[:, N_STATE:]))
    hc_s[:, :N_STATE] = h_re
    hc_s[:, N_STATE:] = h_im
    hl_ref[:, :N_STATE] = h_re
    hl_ref[:, N_STATE:] = h_im

    hb = gh_s[...].astype(BF16)
    hcat = [jnp.concatenate([hb[:, q * PART_STATE:(q + 1) * PART_STATE],
                             hb[:, N_STATE + q * PART_STATE:N_STATE + (q + 1) * PART_STATE]], -1)
            for q in range(SSM_PARTS)]
    width = 2 * LANES
    per_dot = width // PART_CH
    for l0 in range(0, SSM_CHUNK, per_dot):
        c0, k1 = l0 * PART_CH, (l0 + per_dot) * PART_CH
        ys = [_dot(x_s[:, q * PART_COLS:q * PART_COLS + k1], w1_ref[q, :k1, c0:c0 + width])
              + _dot(hcat[q], w4_ref[q, :, c0:c0 + width]) for q in range(SSM_PARTS)]
        for dl in range(per_dot):
            l = l0 + dl
            y = jnp.concatenate([ys[q][:, dl * PART_CH:(dl + 1) * PART_CH] for q in range(SSM_PARTS)], -1)
            y = _gelu_tanh(y + d_ref[...] * jnp.concatenate([steps(l, o) for o in range(n_slabs)], -1))
            gate = _dot(y.astype(BF16), gw_ref[...]) + gb_ref[...]
            out = y * _sigmoid(gate)
            for o in range(n_slabs):
                o_ref[o, pl.ds(l, n_chunks, stride=SSM_CHUNK), :] = out[:, o * LANES:(o + 1) * LANES]


def _ssm_prompt(u, k1, m2, n4, al, d3, gw, gb3, layer, n_seq, seq_len):
    ts = TS_ROWS
    n_tiles = seq_len // ts
    n_chunks = ts // SSM_CHUNK
    rows = u.shape[1]
    compact = SSM_CHUNK * SSM_GROUP
    return pl.pallas_call(
        functools.partial(_ssm_body, n_chunks),
        grid=(n_seq, n_tiles),
        in_specs=[
            pl.BlockSpec((SSM_CH // LANES, ts, LANES), lambda s, i: (0, s * n_tiles + i, 0)),
            pl.BlockSpec((None, SSM_PARTS, compact, PART_COLS), lambda s, i: (layer, 0, 0, 0)),
            pl.BlockSpec((None, SSM_PARTS, compact, 2 * PART_STATE), lambda s, i: (layer, 0, 0, 0)),
            pl.BlockSpec((None, SSM_PARTS, 2 * PART_STATE, compact), lambda s, i: (layer, 0, 0, 0)),
            pl.BlockSpec((None, 1, 2 * N_STATE), lambda s, i: (layer, 0, 0)),
            pl.BlockSpec((None, 1, SSM_CH), lambda s, i: (layer, 0, 0)),
            pl.BlockSpec((None, SSM_CH, SSM_CH), lambda s, i: (layer, 0, 0)),
            pl.BlockSpec((None, 1, SSM_CH), lambda s, i: (layer, 0, 0)),
        ],
        out_specs=[
            pl.BlockSpec((SSM_CH // LANES, ts, LANES), lambda s, i: (0, s * n_tiles + i, 0)),
            pl.BlockSpec((None, 1, 2 * N_STATE), lambda s, i: (s, 0, 0)),
        ],
        out_shape=[jax.ShapeDtypeStruct((SSM_CH // LANES, rows, LANES), F32),
                   jax.ShapeDtypeStruct((n_seq, 1, 2 * N_STATE), F32)],
        scratch_shapes=[
            pltpu.VMEM((n_chunks, CHUNK_COLS), BF16),
            pltpu.VMEM((n_chunks, 2 * N_STATE), F32),
            pltpu.VMEM((1, 2 * N_STATE), F32),
            pltpu.VMEM((SSM_PARTS, PART_COLS, PART_COLS), BF16),
            pltpu.VMEM((SSM_PARTS, PART_COLS, 2 * PART_STATE), BF16),
            pltpu.VMEM((SSM_PARTS, 2 * PART_STATE, PART_COLS), BF16),
        ],
        compiler_params=_params(2),
        name="ssm_prompt",
    )(u, k1, m2, n4, al, d3, gw, gb3)


def _mix_steps(tm, first_tile, q_ref, kv_ref, a_ref, gg_ref, sink_ref, cw_ref, cb_ref, lg_ref, lb_ref,
                 o_ref, ctx_ref, kv_s, u_s, ush_s, bias_s):
    pad = 32
    off = pad - (CONV_WIDTH - 1)
    sub = 8

    @pl.when(first_tile)
    def _():
        kv_s[0:WINDOW, :] = jnp.zeros((WINDOW, 2 * LANES), F32)
        u_s[0:pad, :] = jnp.zeros((pad, CONV_CH), F32)
        qi = lax.broadcasted_iota(jnp.int32, (WINDOW, 2 * WINDOW), 0)
        ki = lax.broadcasted_iota(jnp.int32, (WINDOW, 2 * WINDOW), 1)
        dist = qi - ki + WINDOW
        valid = (dist >= 0) & (dist < WINDOW)
        distf = dist.astype(F32)
        for g in range(N_KV_HEADS):
            for r in range(KV_REP):
                slope = 2.0 ** (-8.0 * (g * KV_REP + r + 1) / N_HEADS)
                bias_s[g, r * WINDOW:(r + 1) * WINDOW, :] = jnp.where(valid, -slope * distf, NEG)

    kv_s[WINDOW:, :] = kv_ref[...].astype(F32)
    u_s[pad:, :] = a_ref[...] * _sigmoid(gg_ref[...])
    for b in range(1, sub):
        ush_s[b - 1] = u_s[b:b + tm + pad - sub, :]
    ctx_ref[...] = u_s[tm + off:tm + pad, :]
    yield

    def conv_units():
        for r0 in range(0, tm, CONV_ROWS):
            acc = jnp.zeros((CONV_ROWS, CONV_CH), F32) + cb_ref[...]
            for j in range(CONV_WIDTH):
                a0, b = divmod(off + j, sub)
                lo = r0 + a0 * sub
                rows = u_s[lo:lo + CONV_ROWS, :] if b == 0 else ush_s[b - 1, lo:lo + CONV_ROWS, :]
                acc = acc + cw_ref[j:j + 1, :] * rows
            mu = jnp.mean(acc, -1, keepdims=True)
            cen = acc - mu
            var = jnp.mean(cen * cen, -1, keepdims=True)
            yn = cen * lax.rsqrt(var + EPS) * lg_ref[...] + lb_ref[...]
            o_ref[r0:r0 + CONV_ROWS, ATTN_WIDTH:] = (yn * _sigmoid(yn)).astype(BF16)
            yield

    scale = 1.0 / math.sqrt(HEAD_DIM)
    k_lane_group = lax.broadcasted_iota(jnp.int32, (2 * WINDOW, LANES), 1) // HEAD_DIM
    o_lane_group = lax.broadcasted_iota(jnp.int32, (KV_REP * WINDOW, LANES), 1) // HEAD_DIM
    key_in_prev = lax.broadcasted_iota(jnp.int32, (1, 2 * WINDOW), 1) < WINDOW
    no_prev_block = jnp.logical_and(key_in_prev, first_tile)
    sinks = [jnp.concatenate([jnp.broadcast_to(sink_ref[:, g * KV_REP + r:g * KV_REP + r + 1], (WINDOW, 1))
                              for r in range(KV_REP)], 0) for g in range(N_KV_HEADS)]

    def attention_units():
        n_blocks = tm // WINDOW
        for b0 in range(0, n_blocks, ATTN_BLOCKS):
            blocks = range(b0, min(b0 + ATTN_BLOCKS, n_blocks))
            pairs = [(blk, g) for blk in blocks for g in range(N_KV_HEADS)]
            kblk = {blk: kv_s[blk * WINDOW:(blk + 2) * WINDOW, 0:LANES] for blk in blocks}
            vblk = {blk: kv_s[blk * WINDOW:(blk + 2) * WINDOW, LANES:].astype(BF16) for blk in blocks}
            qs = {blk: (jnp.concatenate([q_ref[blk * WINDOW:(blk + 1) * WINDOW, r * LANES:(r + 1) * LANES]
                                         for r in range(KV_REP)], 0) * jnp.asarray(scale, BF16))
                  for blk in blocks}
            kg = {(blk, g): jnp.where(k_lane_group == g, kblk[blk], 0.0).astype(BF16) for blk, g in pairs}
            s = {(blk, g): lax.dot_general(qs[blk], kg[blk, g], (((1,), (1,)), ((), ())),
                                           preferred_element_type=F32) + bias_s[g] for blk, g in pairs}
            for g in range(N_KV_HEADS):
                if b0 == 0:
                    s[0, g] = jnp.where(no_prev_block, NEG, s[0, g])
            m = {k: jnp.maximum(jnp.max(s[k], -1, keepdims=True), sinks[k[1]]) for k in pairs}
            p = {k: jnp.exp(s[k] - m[k]) for k in pairs}
            denom = {k: jnp.sum(p[k], -1, keepdims=True) + jnp.exp(sinks[k[1]] - m[k]) for k in pairs}
            og = {k: _dot(p[k].astype(BF16), vblk[k[0]]) / denom[k] for k in pairs}
            for blk in blocks:
                o = jnp.where(o_lane_group == 0, og[blk, 0], og[blk, 1])
                for r in range(KV_REP):
                    o_ref[blk * WINDOW:(blk + 1) * WINDOW, r * LANES:(r + 1) * LANES] = (
                        o[r * WINDOW:(r + 1) * WINDOW, :].astype(BF16))
            yield

    yield from conv_units()
    yield from attention_units()

    kv_s[0:WINDOW, :] = kv_s[tm:tm + WINDOW, :]
    u_s[0:pad, :] = u_s[tm:tm + pad, :]


def _mix_body(tm, *refs):
    for _ in _mix_steps(tm, pl.program_id(1) == 0, *refs):
        pass


def _mix_prompt(zq, zkv, zc, sinks3, cw, cb3, lg3, lb3, layer, n_seq, seq_len):
    tm = 2 * TM_ROWS
    n_tiles = seq_len // tm
    rows = n_seq * seq_len
    row = lambda s, i: s * n_tiles + i
    vec = lambda width: pl.BlockSpec((None, 1, width), lambda s, i: (layer, 0, 0))
    return pl.pallas_call(
        functools.partial(_mix_body, tm),
        grid=(n_seq, n_tiles),
        in_specs=[
            pl.BlockSpec((tm, ATTN_WIDTH), lambda s, i: (row(s, i), 0)),
            pl.BlockSpec((tm, 2 * LANES), lambda s, i: (row(s, i), 0)),
            pl.BlockSpec((tm, CONV_CH), lambda s, i: (row(s, i), 0)),
            pl.BlockSpec((tm, CONV_CH), lambda s, i: (row(s, i), 1)),
            vec(N_HEADS),
            pl.BlockSpec((None, CONV_WIDTH, CONV_CH), lambda s, i: (layer, 0, 0)),
            vec(CONV_CH), vec(CONV_CH), vec(CONV_CH),
        ],
        out_specs=[
            pl.BlockSpec((tm, V_END), lambda s, i: (row(s, i), 0)),
            pl.BlockSpec((None, CONV_WIDTH - 1, CONV_CH), lambda s, i: (s, 0, 0)),
        ],
        out_shape=[jax.ShapeDtypeStruct((rows, V_END), BF16),
                   jax.ShapeDtypeStruct((n_seq, CONV_WIDTH - 1, CONV_CH), F32)],
        scratch_shapes=[pltpu.VMEM((tm + WINDOW, 2 * LANES), F32),
                        pltpu.VMEM((tm + 32, CONV_CH), F32),
                        pltpu.VMEM((7, tm + 24, CONV_CH), F32),
                        pltpu.VMEM((N_KV_HEADS, KV_REP * WINDOW, 2 * WINDOW), F32)],
        compiler_params=_params(2),
        name="mix_prompt",
    )(zq, zkv, zc, zc, sinks3, cw, cb3, lg3, lb3)


def _dec_body(q_ref, kn_ref, vn_ref, k2_ref, v2_ref, ck_ref, cv_ref, a_ref, gg_ref, cc_ref, u_ref, hr_ref, hi_ref,
              sink_ref, cw_ref, cb_ref, lg_ref, lb_ref, bm_ref, cm_ref, ab_ref, d_ref, gw_ref, gb_ref,
              kall_ref, vall_ref, o_ref, ok_ref, ov_ref, co_ref, oc_ref, so_ref, or_ref, oi_ref):
    nb = DEC_BLOCK
    win = ck_ref.shape[2]

    q3 = q_ref[...].reshape(nb, N_HEADS, LANES)
    kn = kn_ref[...]
    vn = vn_ref[...]
    ck = ck_ref[...]
    cv = cv_ref[...]
    s = jnp.einsum("nsc,ncj->nsj", q3, ck.astype(BF16), preferred_element_type=F32)
    si = lax.broadcasted_iota(jnp.int32, (N_HEADS, win), 0)
    ji = lax.broadcasted_iota(jnp.int32, (N_HEADS, win), 1)
    head = (si % 2) * KV_REP + si // 2
    slope = jnp.zeros((N_HEADS, win), F32)
    for h in range(N_HEADS):
        slope = jnp.where(head == h, 2.0 ** (-8.0 * (h + 1) / N_HEADS), slope)
    dist = win - ji
    bias = jnp.where(dist < WINDOW, -slope * dist.astype(F32), NEG)
    s = s + bias[None]
    s_new = jnp.sum(q3.astype(F32) * kn.astype(BF16).astype(F32), -1, keepdims=True)
    sink = sink_ref[...][None]
    m = jnp.maximum(jnp.maximum(jnp.max(s, -1, keepdims=True), s_new), sink)
    p = jnp.exp(s - m)
    p_new = jnp.exp(s_new - m)
    denom = jnp.sum(p, -1, keepdims=True) + p_new + jnp.exp(sink - m)
    o = jnp.einsum("nsj,ncj->nsc", p.astype(BF16), cv.astype(BF16), preferred_element_type=F32)
    o = o + p_new.astype(BF16).astype(F32) * vn.astype(BF16).astype(F32)
    o_ref[...] = (o / denom).reshape(nb * N_HEADS, LANES)

    last = lax.broadcasted_iota(jnp.int32, (LANES, win), 1) == win - 1
    fill = jnp.zeros((LANES - nb, LANES), F32)
    knt = jnp.concatenate([k2_ref[...], fill], 0).T
    vnt = jnp.concatenate([v2_ref[...], fill], 0).T
    for i in range(nb):
        ok_ref[i] = jnp.where(last, pltpu.roll(knt, win - 1 - i, 1), pltpu.roll(ck[i], win - 1, 1))
        ov_ref[i] = jnp.where(last, pltpu.roll(vnt, win - 1 - i, 1), pltpu.roll(cv[i], win - 1, 1))

    @pl.when(pl.program_id(0) == 0)
    def _():
        u = a_ref[...] * _sigmoid(gg_ref[...])
        acc = cb_ref[...] + cw_ref[CONV_WIDTH - 1:CONV_WIDTH, :] * u
        for j in range(CONV_WIDTH - 1):
            acc = acc + cw_ref[j:j + 1, :] * cc_ref[j]
        mu = jnp.mean(acc, -1, keepdims=True)
        cen = acc - mu
        var = jnp.mean(cen * cen, -1, keepdims=True)
        yn = cen * lax.rsqrt(var + EPS) * lg_ref[...] + lb_ref[...]
        co_ref[...] = yn * _sigmoid(yn)
        for j in range(CONV_WIDTH - 2):
            oc_ref[j] = cc_ref[j + 1]
        oc_ref[CONV_WIDTH - 2] = u

        us = jnp.concatenate([u_ref[0], u_ref[1]], -1)
        bu = _dot(us.astype(BF16), bm_ref[...])
        a_re = ab_ref[:, :N_STATE]
        a_im = ab_ref[:, N_STATE:]
        h_re = hr_ref[...]
        h_im = hi_ref[...]
        n_re = a_re * h_re - a_im * h_im + bu[:, :N_STATE]
        n_im = a_re * h_im + a_im * h_re + bu[:, N_STATE:]
        or_ref[...] = n_re
        oi_ref[...] = n_im
        hcat = jnp.concatenate([n_re, n_im], -1).astype(BF16)
        y = _gelu_tanh(_dot(hcat, cm_ref[...]) + d_ref[...] * us)
        gate = _dot(y.astype(BF16), gw_ref[...]) + gb_ref[...]
        so_ref[...] = y * _sigmoid(gate)


def _dec_mix(q3, kn3, vn3, kvn, ck, cv, zc, cct, u, hr, hi, sinks3, cw, cb3, lg3, lb3,
             bmat, cmat, ab3, d3, gw, gb3, k_all, v_all, layer):
    n = kn3.shape[0]
    win = ck.shape[3]
    nb = DEC_BLOCK
    vec = lambda width: pl.BlockSpec((None, 1, width), lambda i: (layer, 0, 0))
    mat = lambda r, c: pl.BlockSpec((None, r, c), lambda i: (layer, 0, 0))
    cache = pl.BlockSpec((None, nb, LANES, win), lambda i: (layer, i, 0, 0))
    ctx = pl.BlockSpec((None, CONV_WIDTH - 1, n, CONV_CH), lambda i: (layer, 0, 0, 0))
    state = pl.BlockSpec((None, n, N_STATE), lambda i: (layer, 0, 0))
    allrows = lambda width: pl.BlockSpec((n, width), lambda i: (0, 0))
    new3 = pl.BlockSpec((nb, 1, LANES), lambda i: (i, 0, 0))
    return pl.pallas_call(
        _dec_body,
        grid=(n // nb,),
        in_specs=[
            pl.BlockSpec((nb * N_HEADS, LANES), lambda i: (i, 0)), new3, new3,
            pl.BlockSpec((nb, LANES), lambda i: (i, 0)),
            pl.BlockSpec((nb, LANES), lambda i: (i, 1)),
            cache, cache,
            pl.BlockSpec((n, CONV_CH), lambda i: (0, 0)),
            pl.BlockSpec((n, CONV_CH), lambda i: (0, 1)),
            ctx,
            pl.BlockSpec((2, n, LANES), lambda i: (0, 0, 0)),
            state, state,
            mat(N_HEADS, 1), mat(CONV_WIDTH, CONV_CH), vec(CONV_CH), vec(CONV_CH), vec(CONV_CH),
            mat(SSM_CH, 2 * N_STATE), mat(2 * N_STATE, SSM_CH), vec(2 * N_STATE), vec(SSM_CH),
            mat(SSM_CH, SSM_CH), vec(SSM_CH),
            pl.BlockSpec(memory_space=pl.ANY), pl.BlockSpec(memory_space=pl.ANY),
        ],
        out_specs=[
            pl.BlockSpec((nb * N_HEADS, LANES), lambda i: (i, 0)),
            cache, cache,
            allrows(CONV_CH),
            pl.BlockSpec((CONV_WIDTH - 1, n, CONV_CH), lambda i: (0, 0, 0)),
            allrows(SSM_CH), allrows(N_STATE), allrows(N_STATE),
        ],
        out_shape=[
            jax.ShapeDtypeStruct((n * N_HEADS, LANES), F32),
            jax.ShapeDtypeStruct(k_all.shape, F32),
            jax.ShapeDtypeStruct(v_all.shape, F32),
            jax.ShapeDtypeStruct((n, CONV_CH), F32),
            jax.ShapeDtypeStruct((CONV_WIDTH - 1, n, CONV_CH), F32),
            jax.ShapeDtypeStruct((n, SSM_CH), F32),
            jax.ShapeDtypeStruct((n, N_STATE), F32),
            jax.ShapeDtypeStruct((n, N_STATE), F32),
        ],
        input_output_aliases={24: 1, 25: 2},
        compiler_params=_params(1),
        name="dec_mix",
    )(q3, kn3, vn3, kvn, kvn, ck, cv, zc, zc, cct, u, hr, hi, sinks3, cw, cb3, lg3, lb3,
      bmat, cmat, ab3, d3, gw, gb3, k_all, v_all)


def _ssm_operands(a_re, a_im, log_dt, b_re, b_im, c_re, c_im):
    hi = lax.Precision.HIGHEST
    l_n, g_n, p_n, c_n = SSM_CHUNK, SSM_GROUPS, SSM_STATE, SSM_GROUP
    dt = jnp.exp(log_dt)[:, None]
    lam_re, lam_im = a_re * dt, a_im * dt
    steps = jnp.arange(l_n + 1, dtype=F32)[:, None, None]
    mag = jnp.exp(steps * lam_re)
    pw_re, pw_im = mag * jnp.cos(steps * lam_im), mag * jnp.sin(steps * lam_im)
    ab_re, ab_im = pw_re[1], pw_im[1]
    den = a_re * a_re + a_im * a_im
    q_re = ((ab_re - 1.0) * a_re + ab_im * a_im) / den
    q_im = (ab_im * a_re - (ab_re - 1.0) * a_im) / den
    bt_re, bt_im = jnp.transpose(b_re, (2, 0, 1)), jnp.transpose(b_im, (2, 0, 1))
    bb_re = q_re * bt_re - q_im * bt_im
    bb_im = q_re * bt_im + q_im * bt_re
    pl_re, pl_im = pw_re[:l_n, None], pw_im[:l_n, None]
    pb_re = pl_re * bb_re - pl_im * bb_im
    pb_im = pl_re * bb_im + pl_im * bb_re

    ktau = (jnp.einsum("gop,tcgp->tcgo", c_re, pb_re, precision=hi)
            - jnp.einsum("gop,tcgp->tcgo", c_im, pb_im, precision=hi)).reshape(l_n, c_n, SSM_CH)
    lag = jnp.arange(l_n)[None, :] - jnp.arange(l_n)[:, None]
    k1 = jnp.where((lag >= 0)[:, :, None, None], jnp.take(ktau, jnp.maximum(lag, 0), axis=0), 0.0)
    k1 = jnp.transpose(k1, (0, 2, 1, 3)).reshape(l_n * c_n, CHUNK_COLS)
    back = l_n - 1 - jnp.arange(l_n)
    m2 = jnp.concatenate([jnp.take(pb_re, back, axis=0).reshape(l_n, c_n, N_STATE),
                          jnp.take(pb_im, back, axis=0).reshape(l_n, c_n, N_STATE)], -1)
    m2 = m2.reshape(l_n * c_n, 2 * N_STATE)
    ct_re, ct_im = jnp.transpose(c_re, (0, 2, 1)), jnp.transpose(c_im, (0, 2, 1))
    pn_re = jnp.transpose(pw_re[1:], (1, 2, 0))[..., None]
    pn_im = jnp.transpose(pw_im[1:], (1, 2, 0))[..., None]
    n4_re = (ct_re[:, :, None, :] * pn_re - ct_im[:, :, None, :] * pn_im).reshape(N_STATE, l_n * c_n)
    n4_im = (ct_re[:, :, None, :] * pn_im + ct_im[:, :, None, :] * pn_re).reshape(N_STATE, l_n * c_n)
    n4 = jnp.concatenate([n4_re, -n4_im], 0)

    chan_group = jnp.arange(SSM_CH) // c_n
    state_group = (jnp.arange(2 * N_STATE) % N_STATE) // p_n
    bmat = jnp.where(chan_group[:, None] == state_group[None, :],
                     jnp.tile(m2[(l_n - 1) * c_n:], (g_n, 1)), 0.0)
    cc = jnp.concatenate([ct_re.reshape(N_STATE, c_n), -ct_im.reshape(N_STATE, c_n)], 0)
    cmat = jnp.where(state_group[:, None] == chan_group[None, :], jnp.tile(cc, (1, g_n)), 0.0)

    k1_h = jnp.transpose(k1.reshape(l_n * c_n, l_n, SSM_PARTS, PART_CH), (2, 0, 1, 3)).reshape(
        SSM_PARTS, l_n * c_n, PART_COLS)
    m2_h = jnp.transpose(m2.reshape(l_n * c_n, 2, SSM_PARTS, PART_STATE), (2, 0, 1, 3)).reshape(
        SSM_PARTS, l_n * c_n, 2 * PART_STATE)
    n4_h = jnp.transpose(n4.reshape(2, SSM_PARTS, PART_STATE, l_n * c_n), (1, 0, 2, 3)).reshape(
        SSM_PARTS, 2 * PART_STATE, l_n * c_n)

    flat = lambda re, im: jnp.concatenate([re.reshape(1, N_STATE), im.reshape(1, N_STATE)], -1)
    return (k1_h, m2_h, n4_h, bmat.astype(BF16), cmat.astype(BF16),
            flat(pw_re[l_n], pw_im[l_n]), flat(ab_re, ab_im))


def _decode_head_order():
    s = jnp.arange(N_HEADS)
    return (s % 2) * KV_REP + s // 2


def kernel(x_prompt, x_sample, cache_swa_k, cache_swa_v, cache_conv, state_ssm_re, state_ssm_im,
           norm_mix_g, w_in, attn_sinks, conv_dw_w, conv_dw_b, conv_ln_g, conv_ln_b,
           ssm_a_re, ssm_a_im, ssm_log_dt, ssm_b_re, ssm_b_im, ssm_c_re, ssm_c_im,
           ssm_d, ssm_glu_w, ssm_glu_b, w_out, norm_ffn_g, w_ff_gate, w_ff_up, w_ff_down,
           norm_final_g):
    n_seq, seq_len, _ = x_prompt.shape
    n_dec = x_sample.shape[0]
    win = cache_swa_k.shape[2]
    assert x_sample.shape[1] == 1 and win == WINDOW
    assert n_dec == WINDOW
    assert seq_len % TS_ROWS == 0 and n_dec % DEC_BLOCK == 0

    row3 = lambda v: v.reshape(DEPTH, 1, -1)
    g_mix, g_ffn = row3(norm_mix_g), row3(norm_ffn_g)
    sinks3 = row3(attn_sinks)
    order = _decode_head_order()
    sinks_dec = attn_sinks[:, order][:, :, None]
    cb3, lg3, lb3 = row3(conv_dw_b), row3(conv_ln_g), row3(conv_ln_b)
    d3, gb3 = row3(ssm_d), row3(ssm_glu_b)
    w_in_q = jnp.transpose(w_in[:, :, :Q_END].astype(BF16).reshape(DEPTH, D_MODEL, N_KV_HEADS, KV_REP, HEAD_DIM),
                           (0, 1, 3, 2, 4)).reshape(DEPTH, D_MODEL, Q_END)
    w_in_rest = w_in[:, :, Q_END:].astype(BF16)
    w_out_b = w_out.astype(BF16)
    wg_b, wu_b, wd_b = w_ff_gate.astype(BF16), w_ff_up.astype(BF16), w_ff_down.astype(BF16)
    gw_b = ssm_glu_w.astype(BF16)
    k1, m2, n4, bmat, cmat, a_chunk, a_step = jax.vmap(_ssm_operands)(
        ssm_a_re, ssm_a_im, ssm_log_dt, ssm_b_re, ssm_b_im, ssm_c_re, ssm_c_im)

    wo_heads = w_out_b[:, :ATTN_WIDTH].reshape(DEPTH, N_HEADS, HEAD_DIM, D_MODEL)[:, order]
    own = (jnp.arange(N_HEADS)[:, None] % 2) == jnp.arange(N_KV_HEADS)[None, :]
    wo_dec = jnp.where(own[None, :, :, None, None], wo_heads[:, :, None], 0).reshape(
        DEPTH, N_HEADS * LANES, D_MODEL)
    wo_attn = jnp.transpose(w_out_b[:, :ATTN_WIDTH].reshape(DEPTH, N_KV_HEADS, KV_REP, HEAD_DIM, D_MODEL),
                            (0, 2, 1, 3, 4)).reshape(DEPTH, ATTN_WIDTH, D_MODEL)
    wo_attn_conv = jnp.concatenate([wo_attn, w_out_b[:, ATTN_WIDTH:V_END]], 1)
    wo_conv = w_out_b[:, ATTN_WIDTH:V_END]
    wo_ssm = w_out_b[:, V_END:]

    ck = jnp.transpose(cache_swa_k, (0, 1, 3, 4, 2)).reshape(DEPTH, n_dec, LANES, win)
    cv = jnp.transpose(cache_swa_v, (0, 1, 3, 4, 2)).reshape(DEPTH, n_dec, LANES, win)
    cct = jnp.transpose(cache_conv, (0, 2, 1, 3))
    hr = state_ssm_re.reshape(DEPTH, n_dec, N_STATE)
    hi = state_ssm_im.reshape(DEPTH, n_dec, N_STATE)

    xp = x_prompt.reshape(n_seq * seq_len, D_MODEL)
    xs = x_sample.reshape(n_dec, D_MODEL)
    tm = TM_ROWS
    decspec = lambda width: pl.BlockSpec((n_dec, width), lambda i: (i, 0))
    scale = 1.0 / math.sqrt(HEAD_DIM)
    own_lane = ((jnp.arange(LANES) // HEAD_DIM)[None, None, None, :]
                == jnp.arange(N_KV_HEADS)[None, None, :, None])

    cache_out = lambda c: jnp.transpose(c.reshape(DEPTH, n_dec, N_KV_HEADS, HEAD_DIM, win), (0, 1, 4, 2, 3))
    k_all = jnp.zeros((DEPTH, n_dec, LANES, win), F32)
    v_all = jnp.zeros((DEPTH, n_dec, LANES, win), F32)
    kp, vp, cp, hrp, hip = [], [], [], [], []
    cs, hrs, his = [], [], []
    for l in range(DEPTH):
        final = norm_final_g.reshape(1, D_MODEL) if l == DEPTH - 1 else None

        zq, zkv, zc, u, kvl = _inproj(xp, g_mix, w_in_q, w_in_rest, l, TI_ROWS, n_seq)
        ssm, h_last = _ssm_prompt(u, k1, m2, n4, a_chunk, d3, gw_b, gb3, l, n_seq, seq_len)
        mix, ctx = _mix_prompt(zq, zkv, zc, sinks3, conv_dw_w, cb3, lg3, lb3, l, n_seq, seq_len)
        xp = _tail(xp, g_ffn,
                   [(mix, pl.BlockSpec((tm, V_END), lambda i: (i, 0)), wo_attn_conv),
                    (ssm, pl.BlockSpec((2, tm, LANES), lambda i: (0, i, 0)), wo_ssm)],
                   wg_b, wu_b, wd_b, l, tm, final)
        kp.append(kvl[..., :LANES].reshape(n_seq, WINDOW, N_KV_HEADS, HEAD_DIM))
        vp.append(kvl[..., LANES:].reshape(n_seq, WINDOW, N_KV_HEADS, HEAD_DIM))
        cp.append(ctx)
        hrp.append(h_last[:, 0, :N_STATE].reshape(n_seq, SSM_GROUPS, SSM_STATE))
        hip.append(h_last[:, 0, N_STATE:].reshape(n_seq, SSM_GROUPS, SSM_STATE))

        zq_s, _, zc_s, us, kvn = _inproj(xs, g_mix, w_in_q, w_in_rest, l, n_dec, 1)
        kvn = kvn.reshape(n_dec, V_END - Q_END)
        zq4 = (zq_s * jnp.asarray(scale, BF16)).reshape(n_dec, KV_REP, 1, LANES)
        q3 = jnp.where(own_lane, zq4, jnp.zeros((), BF16)).reshape(n_dec * N_HEADS, LANES)
        kn3 = kvn[:, :LANES].reshape(n_dec, 1, LANES)
        vn3 = kvn[:, LANES:].reshape(n_dec, 1, LANES)
        o3, k_all, v_all, conv_s, nct, ssm_s, nhr, nhi = _dec_mix(
            q3, kn3, vn3, kvn, ck, cv, zc_s, cct, us, hr, hi, sinks_dec, conv_dw_w, cb3, lg3, lb3,
            bmat, cmat, a_step, d3, gw_b, gb3, k_all, v_all, l)
        xs = _tail(xs, g_ffn,
                   [(o3.reshape(n_dec, N_HEADS * LANES), decspec(N_HEADS * LANES), wo_dec),
                    (conv_s, decspec(CONV_CH), wo_conv),
                    (ssm_s, decspec(SSM_CH), wo_ssm)],
                   wg_b, wu_b, wd_b, l, n_dec, final)
        cs.append(jnp.transpose(nct, (1, 0, 2)))
        hrs.append(nhr.reshape(n_dec, SSM_GROUPS, SSM_STATE))
        his.append(nhi.reshape(n_dec, SSM_GROUPS, SSM_STATE))

    return (xp.reshape(n_seq, seq_len, D_MODEL), xs.reshape(n_dec, 1, D_MODEL),
            jnp.stack(kp), jnp.stack(vp), jnp.stack(cp), jnp.stack(hrp), jnp.stack(hip),
            cache_out(k_all), cache_out(v_all), jnp.stack(cs), jnp.stack(hrs), jnp.stack(his))
```

```python
import functools
import math

import jax
import jax.numpy as jnp
from jax import lax
from jax.experimental import pallas as pl
from jax.experimental.pallas import tpu as pltpu

D_MODEL = 1024
DEPTH = 4
HEAD_DIM = 64
ATTN_WIDTH = 512
N_HEADS = 8
N_KV_HEADS = 2
KV_REP = 4
WINDOW = 128
CONV_CH = 256
CONV_WIDTH = 31
SSM_CH = 256
SSM_GROUP = 16
SSM_GROUPS = 16
SSM_STATE = 64
D_FF = 2816
EPS = 1e-6

Q_END = ATTN_WIDTH
K_END = Q_END + N_KV_HEADS * HEAD_DIM
V_END = K_END + N_KV_HEADS * HEAD_DIM
C_END = V_END + 2 * CONV_CH
IN_COLS = C_END + SSM_CH

N_STATE = SSM_GROUPS * SSM_STATE
LANES = 128
SSM_CHUNK = 8
CHUNK_COLS = SSM_CHUNK * SSM_CH
SSM_PARTS = 4
PART_CH = SSM_CH // SSM_PARTS
PART_GROUPS = PART_CH // SSM_GROUP
PART_COLS = SSM_CHUNK * PART_CH
PART_STATE = PART_GROUPS * SSM_STATE
NEG = -1e30

TM_ROWS = 512
TS_ROWS = 4096
TI_ROWS = 1024
X_RING = 3
FF_CHUNK = 256
OUT_CHUNK = 1024
DEC_BLOCK = 16
CONV_ROWS = 256
ATTN_BLOCKS = 4
VMEM_LIMIT = 56 * 1024 * 1024

F32 = jnp.float32
BF16 = jnp.bfloat16


def _params(n_axes, flags=None):
    return pltpu.CompilerParams(dimension_semantics=("arbitrary",) * n_axes,
                                vmem_limit_bytes=VMEM_LIMIT, flags=flags)


def _resident(shape, index_map):
    return pl.BlockSpec(shape, index_map, pipeline_mode=pl.Buffered(1))


def _rms(x, g):
    return x * lax.rsqrt(jnp.mean(x * x, -1, keepdims=True) + EPS) * g


def _sigmoid(x):
    return 1.0 / (1.0 + jnp.exp(-x))


def _gelu_tanh(x):
    c = math.sqrt(2.0 / math.pi)
    return 0.5 * x * (1.0 + jnp.tanh(c * (x + 0.044715 * (x * x * x))))


def _dot(a, b):
    return jnp.dot(a, b, preferred_element_type=F32)


def _inproj_body(n_steps, tm, x_hbm, g_ref, wq_ref, wr_ref, q_ref, kv_ref, c_ref, u_ref, kvl_ref, x_s, sem):
    i = pl.program_id(0)

    def tile_copy(t):
        rows = pl.ds(pl.multiple_of(t * tm, tm), tm)
        return pltpu.make_async_copy(x_hbm.at[rows, :], x_s.at[t % X_RING], sem.at[t % X_RING])

    @pl.when(i == 0)
    def _():
        tile_copy(0).start()
        if n_steps > 1:
            tile_copy(1).start()

    @pl.when(i + 2 < n_steps)
    def _():
        tile_copy(i + 2).start()

    tile_copy(i).wait()
    h = _rms(x_s[i % X_RING], g_ref[...]).astype(BF16)
    q_ref[...] = _dot(h, wq_ref[...]).astype(BF16)
    zr = _dot(h, wr_ref[...])
    kv = zr[:, :V_END - Q_END]
    kv_ref[...] = kv.astype(BF16)
    kvl_ref[...] = kv[kv.shape[0] - WINDOW:, :]
    c_ref[...] = zr[:, V_END - Q_END:C_END - Q_END]
    u_ref[0] = zr[:, C_END - Q_END:C_END - Q_END + LANES]
    u_ref[1] = zr[:, C_END - Q_END + LANES:]


def _inproj(x, g3, w_q, w_rest, layer, tm, n_seq):
    rows = x.shape[0]
    tiles_per_seq = rows // n_seq // tm
    kv_w = V_END - Q_END
    return pl.pallas_call(
        functools.partial(_inproj_body, rows // tm, tm),
        grid=(rows // tm,),
        in_specs=[
            pl.BlockSpec(memory_space=pl.ANY),
            pl.BlockSpec((None, 1, D_MODEL), lambda i: (layer, 0, 0)),
            _resident((None, D_MODEL, Q_END), lambda i: (layer, 0, 0)),
            _resident((None, D_MODEL, IN_COLS - Q_END), lambda i: (layer, 0, 0)),
        ],
        out_specs=[
            pl.BlockSpec((tm, Q_END), lambda i: (i, 0)),
            pl.BlockSpec((tm, kv_w), lambda i: (i, 0)),
            pl.BlockSpec((tm, 2 * CONV_CH), lambda i: (i, 0)),
            pl.BlockSpec((2, tm, LANES), lambda i: (0, i, 0)),
            pl.BlockSpec((None, WINDOW, kv_w), lambda i: (i // tiles_per_seq, 0, 0)),
        ],
        out_shape=[jax.ShapeDtypeStruct((rows, Q_END), BF16),
                   jax.ShapeDtypeStruct((rows, kv_w), BF16),
                   jax.ShapeDtypeStruct((rows, 2 * CONV_CH), F32),
                   jax.ShapeDtypeStruct((2, rows, LANES), F32),
                   jax.ShapeDtypeStruct((n_seq, WINDOW, kv_w), F32)],
        scratch_shapes=[pltpu.VMEM((X_RING, tm, D_MODEL), F32), pltpu.SemaphoreType.DMA((X_RING,))],
        compiler_params=_params(1),
        name="inproj",
    )(x, g3, w_q, w_rest)


def _tail_steps(x, acts, g_ref, wg_ref, wu_ref, wd_ref, gf_ref, o_ref, x1_s, hf_s, act_s):
    x1 = x
    for act, w_ref in acts:
        x1 = x1 + _dot(act, w_ref[...])
    hf_s[...] = _rms(x1, g_ref[...]).astype(BF16)
    x1_s[...] = x1
    yield
    for c in range(0, D_FF, FF_CHUNK):
        gate = _dot(hf_s[...], wg_ref[:, c:c + FF_CHUNK])
        up = _dot(hf_s[...], wu_ref[:, c:c + FF_CHUNK])
        act_s[:, c:c + FF_CHUNK] = (gate * _sigmoid(gate) * up).astype(BF16)
        yield
    for n in range(0, D_MODEL, OUT_CHUNK):
        o_ref[:, n:n + OUT_CHUNK] = x1_s[:, n:n + OUT_CHUNK] + _dot(act_s[...], wd_ref[:, n:n + OUT_CHUNK])
        yield
    if gf_ref is not None:
        o_ref[...] = _rms(o_ref[...], gf_ref[...])


def _tail_body(n_parts, final, *refs):
    x_ref, g_ref = refs[0], refs[1]
    parts = refs[2:2 + 2 * n_parts]
    wg_ref, wu_ref, wd_ref = refs[2 + 2 * n_parts:5 + 2 * n_parts]
    gf_ref = refs[5 + 2 * n_parts] if final else None
    o_ref, x1_s, hf_s, act_s = refs[-4:]
    acts = []
    for p in range(n_parts):
        act_ref = parts[2 * p]
        if len(act_ref.shape) == 3:
            act = jnp.concatenate([act_ref[i] for i in range(act_ref.shape[0])], -1)
        else:
            act = act_ref[...]
        acts.append((act.astype(BF16), parts[2 * p + 1]))
    for _ in _tail_steps(x_ref[...], acts, g_ref, wg_ref, wu_ref, wd_ref, gf_ref, o_ref, x1_s, hf_s, act_s):
        pass


def _tail(x, g3, parts, wg, wu, wd, layer, tm, final_g=None):
    rows = x.shape[0]
    final = final_g is not None
    in_specs = [pl.BlockSpec((tm, D_MODEL), lambda i: (i, 0)),
                pl.BlockSpec((None, 1, D_MODEL), lambda i: (layer, 0, 0))]
    args = [x, g3]
    for act, spec, w in parts:
        in_specs.append(spec)
        in_specs.append(pl.BlockSpec((None,) + w.shape[1:], lambda i: (layer, 0, 0)))
        args += [act, w]
    in_specs += [
        _resident((None, D_MODEL, D_FF), lambda i: (layer, 0, 0)),
        _resident((None, D_MODEL, D_FF), lambda i: (layer, 0, 0)),
        _resident((None, D_FF, D_MODEL), lambda i: (layer, 0, 0)),
    ]
    args += [wg, wu, wd]
    if final:
        in_specs.append(pl.BlockSpec((1, D_MODEL), lambda i: (0, 0)))
        args.append(final_g)
    return pl.pallas_call(
        functools.partial(_tail_body, len(parts), final),
        grid=(rows // tm,),
        in_specs=in_specs,
        out_specs=pl.BlockSpec((tm, D_MODEL), lambda i: (i, 0)),
        out_shape=jax.ShapeDtypeStruct((rows, D_MODEL), F32),
        scratch_shapes=[pltpu.VMEM((tm, D_MODEL), F32), pltpu.VMEM((tm, D_MODEL), BF16),
                        pltpu.VMEM((tm, D_FF), BF16)],
        compiler_params=_params(1),
        name="tail",
    )(*args)


def _expand_ssm_operands(k1_ref, m2_ref, n4_ref, w1_ref, w2_ref, w4_ref):
    compact = SSM_CHUNK * SSM_GROUP
    col = lax.broadcasted_iota(jnp.int32, (SSM_GROUP, PART_COLS), 1)
    col_chan_group = (col % PART_CH) // SSM_GROUP
    col_state_group = (col % PART_STATE) // SSM_STATE
    e_row = lax.broadcasted_iota(jnp.int32, (compact, PART_COLS), 0)
    e_col = lax.broadcasted_iota(jnp.int32, (compact, PART_COLS), 1)
    spread = ((e_row // SSM_GROUP == e_col // PART_CH) & (e_row % SSM_GROUP == e_col % SSM_GROUP))
    spread = jnp.where(spread, 1.0, 0.0).astype(BF16)
    rows = lax.broadcasted_iota(jnp.int32, (LANES, PART_COLS), 0)
    cols = lax.broadcasted_iota(jnp.int32, (LANES, PART_COLS), 1)
    for q in range(SSM_PARTS):
        for l in range(SSM_CHUNK):
            k1 = k1_ref[q, l * SSM_GROUP:(l + 1) * SSM_GROUP, :]
            m2 = m2_ref[q, l * SSM_GROUP:(l + 1) * SSM_GROUP, :]
            for g in range(PART_GROUPS):
                r0 = l * PART_CH + g * SSM_GROUP
                w1_ref[q, r0:r0 + SSM_GROUP, :] = jnp.where(col_chan_group == g, k1, 0.0).astype(BF16)
                w2_ref[q, r0:r0 + SSM_GROUP, :] = jnp.where(col_state_group == g, m2, 0.0).astype(BF16)
        for b in range(2 * PART_STATE // LANES):
            full = _dot(n4_ref[q, b * LANES:(b + 1) * LANES, :].astype(BF16), spread)
            row_group = ((rows + b * LANES) % PART_STATE) // SSM_STATE
            keep = row_group == (cols % PART_CH) // SSM_GROUP
            w4_ref[q, b * LANES:(b + 1) * LANES, :] = jnp.where(keep, full, 0.0).astype(BF16)


def _ssm_body(n_chunks, u_ref, k1_ref, m2_ref, n4_ref, al_ref, d_ref, gw_ref, gb_ref,
              o_ref, hl_ref, x_s, gh_s, hc_s, w1_ref, w2_ref, w4_ref):
    @pl.when((pl.program_id(0) == 0) & (pl.program_id(1) == 0))
    def _():
        _expand_ssm_operands(k1_ref, m2_ref, n4_ref, w1_ref, w2_ref, w4_ref)

    @pl.when(pl.program_id(1) == 0)
    def _():
        hc_s[...] = jnp.zeros_like(hc_s)

    n_slabs = SSM_CH // LANES

    def steps(l, o):
        return u_ref[o, pl.ds(l, n_chunks, stride=SSM_CHUNK), :]

    low = lax.broadcasted_iota(jnp.int32, (n_chunks, LANES), 1) < PART_CH
    for o in range(n_slabs):
        for l in range(0, SSM_CHUNK, 2):
            a, b = steps(l, o), steps(l + 1, o)
            c0 = l * PART_CH
            first = jnp.where(low, a, pltpu.roll(b, PART_CH, 1))
            second = jnp.where(low, pltpu.roll(a, PART_CH, 1), b)
            x_s[:, 2 * o * PART_COLS + c0:2 * o * PART_COLS + c0 + LANES] = first.astype(BF16)
            x_s[:, (2 * o + 1) * PART_COLS + c0:(2 * o + 1) * PART_COLS + c0 + LANES] = second.astype(BF16)

    for q in range(SSM_PARTS):
        g = _dot(x_s[:, q * PART_COLS:(q + 1) * PART_COLS], w2_ref[q])
        gh_s[:, q * PART_STATE:(q + 1) * PART_STATE] = g[:, :PART_STATE]
        gh_s[:, N_STATE + q * PART_STATE:N_STATE + (q + 1) * PART_STATE] = g[:, PART_STATE:]

    a_re = al_ref[:, :N_STATE]
    a_im = al_ref[:, N_STATE:]

    def step(k, carry):
        h_re, h_im = carry
        g_re = gh_s[pl.ds(k, 1), :N_STATE]
        g_im = gh_s[pl.ds(k, 1), N_STATE:]
        gh_s[pl.ds(k, 1), :N_STATE] = h_re
        gh_s[pl.ds(k, 1), N_STATE:] = h_im
        return (a_re * h_re - a_im * h_im + g_re, a_re * h_im + a_im * h_re + g_im)

    h_re, h_im = lax.fori_loop(0, n_chunks, step, (hc_s[:, :N_STATE], hc_s[:, N_STATE:]))
    hc_s[:, :N_STATE] = h_re
    hc_s[:, N_STATE:] = h_im
    hl_ref[:, :N_STATE] = h_re
    hl_ref[:, N_STATE:] = h_im

    hb = gh_s[...].astype(BF16)
    hcat = [jnp.concatenate([hb[:, q * PART_STATE:(q + 1) * PART_STATE],
                             hb[:, N_STATE + q * PART_STATE:N_STATE + (q + 1) * PART_STATE]], -1)
            for q in range(SSM_PARTS)]
    width = 2 * LANES
    per_dot = width // PART_CH
    for l0 in range(0, SSM_CHUNK, per_dot):
        c0, k1 = l0 * PART_CH, (l0 + per_dot) * PART_CH
        ys = [_dot(x_s[:, q * PART_COLS:q * PART_COLS + k1], w1_ref[q, :k1, c0:c0 + width])
              + _dot(hcat[q], w4_ref[q, :, c0:c0 + width]) for q in range(SSM_PARTS)]
        for dl in range(per_dot):
            l = l0 + dl
            y = jnp.concatenate([ys[q][:, dl * PART_CH:(dl + 1) * PART_CH] for q in range(SSM_PARTS)], -1)
            y = _gelu_tanh(y + d_ref[...] * jnp.concatenate([steps(l, o) for o in range(n_slabs)], -1))
            gate = _dot(y.astype(BF16), gw_ref[...]) + gb_ref[...]
            out = y * _sigmoid(gate)
            for o in range(n_slabs):
                o_ref[o, pl.ds(l, n_chunks, stride=SSM_CHUNK), :] = out[:, o * LANES:(o + 1) * LANES]


def _ssm_prompt(u, k1, m2, n4, al, d3, gw, gb3, layer, n_seq, seq_len):
    ts = TS_ROWS
    n_tiles = seq_len // ts
    n_chunks = ts // SSM_CHUNK
    rows = u.shape[1]
    compact = SSM_CHUNK * SSM_GROUP
    return pl.pallas_call(
        functools.partial(_ssm_body, n_chunks),
        grid=(n_seq, n_tiles),
        in_specs=[
            pl.BlockSpec((SSM_CH // LANES, ts, LANES), lambda s, i: (0, s * n_tiles + i, 0)),
            pl.BlockSpec((None, SSM_PARTS, compact, PART_COLS), lambda s, i: (layer, 0, 0, 0)),
            pl.BlockSpec((None, SSM_PARTS, compact, 2 * PART_STATE), lambda s, i: (layer, 0, 0, 0)),
            pl.BlockSpec((None, SSM_PARTS, 2 * PART_STATE, compact), lambda s, i: (layer, 0, 0, 0)),
            pl.BlockSpec((None, 1, 2 * N_STATE), lambda s, i: (layer, 0, 0)),
            pl.BlockSpec((None, 1, SSM_CH), lambda s, i: (layer, 0, 0)),
            pl.BlockSpec((None, SSM_CH, SSM_CH), lambda s, i: (layer, 0, 0)),
            pl.BlockSpec((None, 1, SSM_CH), lambda s, i: (layer, 0, 0)),
        ],
        out_specs=[
            pl.BlockSpec((SSM_CH // LANES, ts, LANES), lambda s, i: (0, s * n_tiles + i, 0)),
            pl.BlockSpec((None, 1, 2 * N_STATE), lambda s, i: (s, 0, 0)),
        ],
        out_shape=[jax.ShapeDtypeStruct((SSM_CH // LANES, rows, LANES), F32),
                   jax.ShapeDtypeStruct((n_seq, 1, 2 * N_STATE), F32)],
        scratch_shapes=[
            pltpu.VMEM((n_chunks, CHUNK_COLS), BF16),
            pltpu.VMEM((n_chunks, 2 * N_STATE), F32),
            pltpu.VMEM((1, 2 * N_STATE), F32),
            pltpu.VMEM((SSM_PARTS, PART_COLS, PART_COLS), BF16),
            pltpu.VMEM((SSM_PARTS, PART_COLS, 2 * PART_STATE), BF16),
            pltpu.VMEM((SSM_PARTS, 2 * PART_STATE, PART_COLS), BF16),
        ],
        compiler_params=_params(2),
        name="ssm_prompt",
    )(u, k1, m2, n4, al, d3, gw, gb3)


def _mix_steps(tm, first_tile, q_ref, kv_ref, a_ref, gg_ref, sink_ref, cw_ref, cb_ref, lg_ref, lb_ref,
                 o_ref, ctx_ref, kv_s, u_s, ush_s, bias_s):
    pad = 32
    off = pad - (CONV_WIDTH - 1)
    sub = 8

    @pl.when(first_tile)
    def _():
        kv_s[0:WINDOW, :] = jnp.zeros((WINDOW, 2 * LANES), F32)
        u_s[0:pad, :] = jnp.zeros((pad, CONV_CH), F32)
        qi = lax.broadcasted_iota(jnp.int32, (WINDOW, 2 * WINDOW), 0)
        ki = lax.broadcasted_iota(jnp.int32, (WINDOW, 2 * WINDOW), 1)
        dist = qi - ki + WINDOW
        valid = (dist >= 0) & (dist < WINDOW)
        distf = dist.astype(F32)
        for g in range(N_KV_HEADS):
            for r in range(KV_REP):
                slope = 2.0 ** (-8.0 * (g * KV_REP + r + 1) / N_HEADS)
                bias_s[g, r * WINDOW:(r + 1) * WINDOW, :] = jnp.where(valid, -slope * distf, NEG)

    kv_s[WINDOW:, :] = kv_ref[...].astype(F32)
    u_s[pad:, :] = a_ref[...] * _sigmoid(gg_ref[...])
    for b in range(1, sub):
        ush_s[b - 1] = u_s[b:b + tm + pad - sub, :]
    ctx_ref[...] = u_s[tm + off:tm + pad, :]
    yield

    def conv_units():
        for r0 in range(0, tm, CONV_ROWS):
            acc = jnp.zeros((CONV_ROWS, CONV_CH), F32) + cb_ref[...]
            for j in range(CONV_WIDTH):
                a0, b = divmod(off + j, sub)
                lo = r0 + a0 * sub
                rows = u_s[lo:lo + CONV_ROWS, :] if b == 0 else ush_s[b - 1, lo:lo + CONV_ROWS, :]
                acc = acc + cw_ref[j:j + 1, :] * rows
            mu = jnp.mean(acc, -1, keepdims=True)
            cen = acc - mu
            var = jnp.mean(cen * cen, -1, keepdims=True)
            yn = cen * lax.rsqrt(var + EPS) * lg_ref[...] + lb_ref[...]
            o_ref[r0:r0 + CONV_ROWS, ATTN_WIDTH:] = (yn * _sigmoid(yn)).astype(BF16)
            yield

    scale = 1.0 / math.sqrt(HEAD_DIM)
    k_lane_group = lax.broadcasted_iota(jnp.int32, (2 * WINDOW, LANES), 1) // HEAD_DIM
    o_lane_group = lax.broadcasted_iota(jnp.int32, (KV_REP * WINDOW, LANES), 1) // HEAD_DIM
    key_in_prev = lax.broadcasted_iota(jnp.int32, (1, 2 * WINDOW), 1) < WINDOW
    no_prev_block = jnp.logical_and(key_in_prev, first_tile)
    sinks = [jnp.concatenate([jnp.broadcast_to(sink_ref[:, g * KV_REP + r:g * KV_REP + r + 1], (WINDOW, 1))
                              for r in range(KV_REP)], 0) for g in range(N_KV_HEADS)]

    def attention_units():
        n_blocks = tm // WINDOW
        for b0 in range(0, n_blocks, ATTN_BLOCKS):
            blocks = range(b0, min(b0 + ATTN_BLOCKS, n_blocks))
            pairs = [(blk, g) for blk in blocks for g in range(N_KV_HEADS)]
            kblk = {blk: kv_s[blk * WINDOW:(blk + 2) * WINDOW, 0:LANES] for blk in blocks}
            vblk = {blk: kv_s[blk * WINDOW:(blk + 2) * WINDOW, LANES:].astype(BF16) for blk in blocks}
            qs = {blk: (jnp.concatenate([q_ref[blk * WINDOW:(blk + 1) * WINDOW, r * LANES:(r + 1) * LANES]
                                         for r in range(KV_REP)], 0) * jnp.asarray(scale, BF16))
                  for blk in blocks}
            kg = {(blk, g): jnp.where(k_lane_group == g, kblk[blk], 0.0).astype(BF16) for blk, g in pairs}
            s = {(blk, g): lax.dot_general(qs[blk], kg[blk, g], (((1,), (1,)), ((), ())),
                                           preferred_element_type=F32) + bias_s[g] for blk, g in pairs}
            for g in range(N_KV_HEADS):
                if b0 == 0:
                    s[0, g] = jnp.where(no_prev_block, NEG, s[0, g])
            m = {k: jnp.maximum(jnp.max(s[k], -1, keepdims=True), sinks[k[1]]) for k in pairs}
            p = {k: jnp.exp(s[k] - m[k]) for k in pairs}
            denom = {k: jnp.sum(p[k], -1, keepdims=True) + jnp.exp(sinks[k[1]] - m[k]) for k in pairs}
            og = {k: _dot(p[k].astype(BF16), vblk[k[0]]) / denom[k] for k in pairs}
            for blk in blocks:
                o = jnp.where(o_lane_group == 0, og[blk, 0], og[blk, 1])
                for r in range(KV_REP):
                    o_ref[blk * WINDOW:(blk + 1) * WINDOW, r * LANES:(r + 1) * LANES] = (
                        o[r * WINDOW:(r + 1) * WINDOW, :].astype(BF16))
            yield

    yield from conv_units()
    yield from attention_units()

    kv_s[0:WINDOW, :] = kv_s[tm:tm + WINDOW, :]
    u_s[0:pad, :] = u_s[tm:tm + pad, :]


def _mix_body(tm, *refs):
    for _ in _mix_steps(tm, pl.program_id(1) == 0, *refs):
        pass


def _mix_prompt(zq, zkv, zc, sinks3, cw, cb3, lg3, lb3, layer, n_seq, seq_len):
    tm = 2 * TM_ROWS
    n_tiles = seq_len // tm
    rows = n_seq * seq_len
    row = lambda s, i: s * n_tiles + i
    vec = lambda width: pl.BlockSpec((None, 1, width), lambda s, i: (layer, 0, 0))
    return pl.pallas_call(
        functools.partial(_mix_body, tm),
        grid=(n_seq, n_tiles),
        in_specs=[
            pl.BlockSpec((tm, ATTN_WIDTH), lambda s, i: (row(s, i), 0)),
            pl.BlockSpec((tm, 2 * LANES), lambda s, i: (row(s, i), 0)),
            pl.BlockSpec((tm, CONV_CH), lambda s, i: (row(s, i), 0)),
            pl.BlockSpec((tm, CONV_CH), lambda s, i: (row(s, i), 1)),
            vec(N_HEADS),
            pl.BlockSpec((None, CONV_WIDTH, CONV_CH), lambda s, i: (layer, 0, 0)),
            vec(CONV_CH), vec(CONV_CH), vec(CONV_CH),
        ],
        out_specs=[
            pl.BlockSpec((tm, V_END), lambda s, i: (row(s, i), 0)),
            pl.BlockSpec((None, CONV_WIDTH - 1, CONV_CH), lambda s, i: (s, 0, 0)),
        ],
        out_shape=[jax.ShapeDtypeStruct((rows, V_END), BF16),
                   jax.ShapeDtypeStruct((n_seq, CONV_WIDTH - 1, CONV_CH), F32)],
        scratch_shapes=[pltpu.VMEM((tm + WINDOW, 2 * LANES), F32),
                        pltpu.VMEM((tm + 32, CONV_CH), F32),
                        pltpu.VMEM((7, tm + 24, CONV_CH), F32),
                        pltpu.VMEM((N_KV_HEADS, KV_REP * WINDOW, 2 * WINDOW), F32)],
        compiler_params=_params(2),
        name="mix_prompt",
    )(zq, zkv, zc, zc, sinks3, cw, cb3, lg3, lb3)


def _dec_body(q_ref, kn_ref, vn_ref, k2_ref, v2_ref, ck_ref, cv_ref, a_ref, gg_ref, cc_ref, u_ref, hr_ref, hi_ref,
              sink_ref, cw_ref, cb_ref, lg_ref, lb_ref, bm_ref, cm_ref, ab_ref, d_ref, gw_ref, gb_ref,
              kall_ref, vall_ref, o_ref, ok_ref, ov_ref, co_ref, oc_ref, so_ref, or_ref, oi_ref):
    nb = DEC_BLOCK
    win = ck_ref.shape[2]

    q3 = q_ref[...].reshape(nb, N_HEADS, LANES)
    kn = kn_ref[...]
    vn = vn_ref[...]
    ck = ck_ref[...]
    cv = cv_ref[...]
    s = jnp.einsum("nsc,ncj->nsj", q3, ck.astype(BF16), preferred_element_type=F32)
    si = lax.broadcasted_iota(jnp.int32, (N_HEADS, win), 0)
    ji = lax.broadcasted_iota(jnp.int32, (N_HEADS, win), 1)
    head = (si % 2) * KV_REP + si // 2
    slope = jnp.zeros((N_HEADS, win), F32)
    for h in range(N_HEADS):
        slope = jnp.where(head == h, 2.0 ** (-8.0 * (h + 1) / N_HEADS), slope)
    dist = win - ji
    bias = jnp.where(dist < WINDOW, -slope * dist.astype(F32), NEG)
    s = s + bias[None]
    s_new = jnp.sum(q3.astype(F32) * kn.astype(BF16).astype(F32), -1, keepdims=True)
    sink = sink_ref[...][None]
    m = jnp.maximum(jnp.maximum(jnp.max(s, -1, keepdims=True), s_new), sink)
    p = jnp.exp(s - m)
    p_new = jnp.exp(s_new - m)
    denom = jnp.sum(p, -1, keepdims=True) + p_new + jnp.exp(sink - m)
    o = jnp.einsum("nsj,ncj->nsc", p.astype(BF16), cv.astype(BF16), preferred_element_type=F32)
    o = o + p_new.astype(BF16).astype(F32) * vn.astype(BF16).astype(F32)
    o_ref[...] = (o / denom).reshape(nb * N_HEADS, LANES)

    last = lax.broadcasted_iota(jnp.int32, (LANES, win), 1) == win - 1
    fill = jnp.zeros((LANES - nb, LANES), F32)
    knt = jnp.concatenate([k2_ref[...], fill], 0).T
    vnt = jnp.concatenate([v2_ref[...], fill], 0).T
    for i in range(nb):
        ok_ref[i] = jnp.where(last, pltpu.roll(knt, win - 1 - i, 1), pltpu.roll(ck[i], win - 1, 1))
        ov_ref[i] = jnp.where(last, pltpu.roll(vnt, win - 1 - i, 1), pltpu.roll(cv[i], win - 1, 1))

    @pl.when(pl.program_id(0) == 0)
    def _():
        u = a_ref[...] * _sigmoid(gg_ref[...])
        acc = cb_ref[...] + cw_ref[CONV_WIDTH - 1:CONV_WIDTH, :] * u
        for j in range(CONV_WIDTH - 1):
            acc = acc + cw_ref[j:j + 1, :] * cc_ref[j]
        mu = jnp.mean(acc, -1, keepdims=True)
        cen = acc - mu
        var = jnp.mean(cen * cen, -1, keepdims=True)
        yn = cen * lax.rsqrt(var + EPS) * lg_ref[...] + lb_ref[...]
        co_ref[...] = yn * _sigmoid(yn)
        for j in range(CONV_WIDTH - 2):
            oc_ref[j] = cc_ref[j + 1]
        oc_ref[CONV_WIDTH - 2] = u

        us = jnp.concatenate([u_ref[0], u_ref[1]], -1)
        bu = _dot(us.astype(BF16), bm_ref[...])
        a_re = ab_ref[:, :N_STATE]
        a_im = ab_ref[:, N_STATE:]
        h_re = hr_ref[...]
        h_im = hi_ref[...]
        n_re = a_re * h_re - a_im * h_im + bu[:, :N_STATE]
        n_im = a_re * h_im + a_im * h_re + bu[:, N_STATE:]
        or_ref[...] = n_re
        oi_ref[...] = n_im
        hcat = jnp.concatenate([n_re, n_im], -1).astype(BF16)
        y = _gelu_tanh(_dot(hcat, cm_ref[...]) + d_ref[...] * us)
        gate = _dot(y.astype(BF16), gw_ref[...]) + gb_ref[...]
        so_ref[...] = y * _sigmoid(gate)


def _dec_mix(q3, kn3, vn3, kvn, ck, cv, zc, cct, u, hr, hi, sinks3, cw, cb3, lg3, lb3,
             bmat, cmat, ab3, d3, gw, gb3, k_all, v_all, layer):
    n = kn3.shape[0]
    win = ck.shape[3]
    nb = DEC_BLOCK
    vec = lambda width: pl.BlockSpec((None, 1, width), lambda i: (layer, 0, 0))
    mat = lambda r, c: pl.BlockSpec((None, r, c), lambda i: (layer, 0, 0))
    cache = pl.BlockSpec((None, nb, LANES, win), lambda i: (layer, i, 0, 0))
    ctx = pl.BlockSpec((None, CONV_WIDTH - 1, n, CONV_CH), lambda i: (layer, 0, 0, 0))
    state = pl.BlockSpec((None, n, N_STATE), lambda i: (layer, 0, 0))
    allrows = lambda width: pl.BlockSpec((n, width), lambda i: (0, 0))
    new3 = pl.BlockSpec((nb, 1, LANES), lambda i: (i, 0, 0))
    return pl.pallas_call(
        _dec_body,
        grid=(n // nb,),
        in_specs=[
            pl.BlockSpec((nb * N_HEADS, LANES), lambda i: (i, 0)), new3, new3,
            pl.BlockSpec((nb, LANES), lambda i: (i, 0)),
            pl.BlockSpec((nb, LANES), lambda i: (i, 1)),
            cache, cache,
            pl.BlockSpec((n, CONV_CH), lambda i: (0, 0)),
            pl.BlockSpec((n, CONV_CH), lambda i: (0, 1)),
            ctx,
            pl.BlockSpec((2, n, LANES), lambda i: (0, 0, 0)),
            state, state,
            mat(N_HEADS, 1), mat(CONV_WIDTH, CONV_CH), vec(CONV_CH), vec(CONV_CH), vec(CONV_CH),
            mat(SSM_CH, 2 * N_STATE), mat(2 * N_STATE, SSM_CH), vec(2 * N_STATE), vec(SSM_CH),
            mat(SSM_CH, SSM_CH), vec(SSM_CH),
            pl.BlockSpec(memory_space=pl.ANY), pl.BlockSpec(memory_space=pl.ANY),
        ],
        out_specs=[
            pl.BlockSpec((nb * N_HEADS, LANES), lambda i: (i, 0)),
            cache, cache,
            allrows(CONV_CH),
            pl.BlockSpec((CONV_WIDTH - 1, n, CONV_CH), lambda i: (0, 0, 0)),
            allrows(SSM_CH), allrows(N_STATE), allrows(N_STATE),
        ],
        out_shape=[
            jax.ShapeDtypeStruct((n * N_HEADS, LANES), F32),
            jax.ShapeDtypeStruct(k_all.shape, F32),
            jax.ShapeDtypeStruct(v_all.shape, F32),
            jax.ShapeDtypeStruct((n, CONV_CH), F32),
            jax.ShapeDtypeStruct((CONV_WIDTH - 1, n, CONV_CH), F32),
            jax.ShapeDtypeStruct((n, SSM_CH), F32),
            jax.ShapeDtypeStruct((n, N_STATE), F32),
            jax.ShapeDtypeStruct((n, N_STATE), F32),
        ],
        input_output_aliases={24: 1, 25: 2},
        compiler_params=_params(1),
        name="dec_mix",
    )(q3, kn3, vn3, kvn, kvn, ck, cv, zc, zc, cct, u, hr, hi, sinks3, cw, cb3, lg3, lb3,
      bmat, cmat, ab3, d3, gw, gb3, k_all, v_all)


def _ssm_operands(a_re, a_im, log_dt, b_re, b_im, c_re, c_im):
    hi = lax.Precision.HIGHEST
    l_n, g_n, p_n, c_n = SSM_CHUNK, SSM_GROUPS, SSM_STATE, SSM_GROUP
    dt = jnp.exp(log_dt)[:, None]
    lam_re, lam_im = a_re * dt, a_im * dt
    steps = jnp.arange(l_n + 1, dtype=F32)[:, None, None]
    mag = jnp.exp(steps * lam_re)
    pw_re, pw_im = mag * jnp.cos(steps * lam_im), mag * jnp.sin(steps * lam_im)
    ab_re, ab_im = pw_re[1], pw_im[1]
    den = a_re * a_re + a_im * a_im
    q_re = ((ab_re - 1.0) * a_re + ab_im * a_im) / den
    q_im = (ab_im * a_re - (ab_re - 1.0) * a_im) / den
    bt_re, bt_im = jnp.transpose(b_re, (2, 0, 1)), jnp.transpose(b_im, (2, 0, 1))
    bb_re = q_re * bt_re - q_im * bt_im
    bb_im = q_re * bt_im + q_im * bt_re
    pl_re, pl_im = pw_re[:l_n, None], pw_im[:l_n, None]
    pb_re = pl_re * bb_re - pl_im * bb_im
    pb_im = pl_re * bb_im + pl_im * bb_re

    ktau = (jnp.einsum("gop,tcgp->tcgo", c_re, pb_re, precision=hi)
            - jnp.einsum("gop,tcgp->tcgo", c_im, pb_im, precision=hi)).reshape(l_n, c_n, SSM_CH)
    lag = jnp.arange(l_n)[None, :] - jnp.arange(l_n)[:, None]
    k1 = jnp.where((lag >= 0)[:, :, None, None], jnp.take(ktau, jnp.maximum(lag, 0), axis=0), 0.0)
    k1 = jnp.transpose(k1, (0, 2, 1, 3)).reshape(l_n * c_n, CHUNK_COLS)
    back = l_n - 1 - jnp.arange(l_n)
    m2 = jnp.concatenate([jnp.take(pb_re, back, axis=0).reshape(l_n, c_n, N_STATE),
                          jnp.take(pb_im, back, axis=0).reshape(l_n, c_n, N_STATE)], -1)
    m2 = m2.reshape(l_n * c_n, 2 * N_STATE)
    ct_re, ct_im = jnp.transpose(c_re, (0, 2, 1)), jnp.transpose(c_im, (0, 2, 1))
    pn_re = jnp.transpose(pw_re[1:], (1, 2, 0))[..., None]
    pn_im = jnp.transpose(pw_im[1:], (1, 2, 0))[..., None]
    n4_re = (ct_re[:, :, None, :] * pn_re - ct_im[:, :, None, :] * pn_im).reshape(N_STATE, l_n * c_n)
    n4_im = (ct_re[:, :, None, :] * pn_im + ct_im[:, :, None, :] * pn_re).reshape(N_STATE, l_n * c_n)
    n4 = jnp.concatenate([n4_re, -n4_im], 0)

    chan_group = jnp.arange(SSM_CH) // c_n
    state_group = (jnp.arange(2 * N_STATE) % N_STATE) // p_n
    bmat = jnp.where(chan_group[:, None] == state_group[None, :],
                     jnp.tile(m2[(l_n - 1) * c_n:], (g_n, 1)), 0.0)
    cc = jnp.concatenate([ct_re.reshape(N_STATE, c_n), -ct_im.reshape(N_STATE, c_n)], 0)
    cmat = jnp.where(state_group[:, None] == chan_group[None, :], jnp.tile(cc, (1, g_n)), 0.0)

    k1_h = jnp.transpose(k1.reshape(l_n * c_n, l_n, SSM_PARTS, PART_CH), (2, 0, 1, 3)).reshape(
        SSM_PARTS, l_n * c_n, PART_COLS)
    m2_h = jnp.transpose(m2.reshape(l_n * c_n, 2, SSM_PARTS, PART_STATE), (2, 0, 1, 3)).reshape(
        SSM_PARTS, l_n * c_n, 2 * PART_STATE)
    n4_h = jnp.transpose(n4.reshape(2, SSM_PARTS, PART_STATE, l_n * c_n), (1, 0, 2, 3)).reshape(
        SSM_PARTS, 2 * PART_STATE, l_n * c_n)

    flat = lambda re, im: jnp.concatenate([re.reshape(1, N_STATE), im.reshape(1, N_STATE)], -1)
    return (k1_h, m2_h, n4_h, bmat.astype(BF16), cmat.astype(BF16),
            flat(pw_re[l_n], pw_im[l_n]), flat(ab_re, ab_im))


def _decode_head_order():
    s = jnp.arange(N_HEADS)
    return (s % 2) * KV_REP + s // 2


def kernel(x_prompt, x_sample, cache_swa_k, cache_swa_v, cache_conv, state_ssm_re, state_ssm_im,
           norm_mix_g, w_in, attn_sinks, conv_dw_w, conv_dw_b, conv_ln_g, conv_ln_b,
           ssm_a_re, ssm_a_im, ssm_log_dt, ssm_b_re, ssm_b_im, ssm_c_re, ssm_c_im,
           ssm_d, ssm_glu_w, ssm_glu_b, w_out, norm_ffn_g, w_ff_gate, w_ff_up, w_ff_down,
           norm_final_g):
    n_seq, seq_len, _ = x_prompt.shape
    n_dec = x_sample.shape[0]
    win = cache_swa_k.shape[2]
    assert x_sample.shape[1] == 1 and win == WINDOW
    assert n_dec == WINDOW
    assert seq_len % TS_ROWS == 0 and n_dec % DEC_BLOCK == 0

    row3 = lambda v: v.reshape(DEPTH, 1, -1)
    g_mix, g_ffn = row3(norm_mix_g), row3(norm_ffn_g)
    sinks3 = row3(attn_sinks)
    order = _decode_head_order()
    sinks_dec = attn_sinks[:, order][:, :, None]
    cb3, lg3, lb3 = row3(conv_dw_b), row3(conv_ln_g), row3(conv_ln_b)
    d3, gb3 = row3(ssm_d), row3(ssm_glu_b)
    w_in_q = jnp.transpose(w_in[:, :, :Q_END].astype(BF16).reshape(DEPTH, D_MODEL, N_KV_HEADS, KV_REP, HEAD_DIM),
                           (0, 1, 3, 2, 4)).reshape(DEPTH, D_MODEL, Q_END)
    w_in_rest = w_in[:, :, Q_END:].astype(BF16)
    w_out_b = w_out.astype(BF16)
    wg_b, wu_b, wd_b = w_ff_gate.astype(BF16), w_ff_up.astype(BF16), w_ff_down.astype(BF16)
    gw_b = ssm_glu_w.astype(BF16)
    k1, m2, n4, bmat, cmat, a_chunk, a_step = jax.vmap(_ssm_operands)(
        ssm_a_re, ssm_a_im, ssm_log_dt, ssm_b_re, ssm_b_im, ssm_c_re, ssm_c_im)

    wo_heads = w_out_b[:, :ATTN_WIDTH].reshape(DEPTH, N_HEADS, HEAD_DIM, D_MODEL)[:, order]
    own = (jnp.arange(N_HEADS)[:, None] % 2) == jnp.arange(N_KV_HEADS)[None, :]
    wo_dec = jnp.where(own[None, :, :, None, None], wo_heads[:, :, None], 0).reshape(
        DEPTH, N_HEADS * LANES, D_MODEL)
    wo_attn = jnp.transpose(w_out_b[:, :ATTN_WIDTH].reshape(DEPTH, N_KV_HEADS, KV_REP, HEAD_DIM, D_MODEL),
                            (0, 2, 1, 3, 4)).reshape(DEPTH, ATTN_WIDTH, D_MODEL)
    wo_attn_conv = jnp.concatenate([wo_attn, w_out_b[:, ATTN_WIDTH:V_END]], 1)
    wo_conv = w_out_b[:, ATTN_WIDTH:V_END]
    wo_ssm = w_out_b[:, V_END:]

    ck = jnp.transpose(cache_swa_k, (0, 1, 3, 4, 2)).reshape(DEPTH, n_dec, LANES, win)
    cv = jnp.transpose(cache_swa_v, (0, 1, 3, 4, 2)).reshape(DEPTH, n_dec, LANES, win)
    cct = jnp.transpose(cache_conv, (0, 2, 1, 3))
    hr = state_ssm_re.reshape(DEPTH, n_dec, N_STATE)
    hi = state_ssm_im.reshape(DEPTH, n_dec, N_STATE)

    xp = x_prompt.reshape(n_seq * seq_len, D_MODEL)
    xs = x_sample.reshape(n_dec, D_MODEL)
    tm = TM_ROWS
    decspec = lambda width: pl.BlockSpec((n_dec, width), lambda i: (i, 0))
    scale = 1.0 / math.sqrt(HEAD_DIM)
    own_lane = ((jnp.arange(LANES) // HEAD_DIM)[None, None, None, :]
                == jnp.arange(N_KV_HEADS)[None, None, :, None])

    cache_out = lambda c: jnp.transpose(c.reshape(DEPTH, n_dec, N_KV_HEADS, HEAD_DIM, win), (0, 1, 4, 2, 3))
    k_all = jnp.zeros((DEPTH, n_dec, LANES, win), F32)
    v_all = jnp.zeros((DEPTH, n_dec, LANES, win), F32)
    kp, vp, cp, hrp, hip = [], [], [], [], []
    cs, hrs, his = [], [], []
    for l in range(DEPTH):
        final = norm_final_g.reshape(1, D_MODEL) if l == DEPTH - 1 else None

        zq, zkv, zc, u, kvl = _inproj(xp, g_mix, w_in_q, w_in_rest, l, TI_ROWS, n_seq)
        ssm, h_last = _ssm_prompt(u, k1, m2, n4, a_chunk, d3, gw_b, gb3, l, n_seq, seq_len)
        mix, ctx = _mix_prompt(zq, zkv, zc, sinks3, conv_dw_w, cb3, lg3, lb3, l, n_seq, seq_len)
        xp = _tail(xp, g_ffn,
                   [(mix, pl.BlockSpec((tm, V_END), lambda i: (i, 0)), wo_attn_conv),
                    (ssm, pl.BlockSpec((2, tm, LANES), lambda i: (0, i, 0)), wo_ssm)],
                   wg_b, wu_b, wd_b, l, tm, final)
        kp.append(kvl[..., :LANES].reshape(n_seq, WINDOW, N_KV_HEADS, HEAD_DIM))
        vp.append(kvl[..., LANES:].reshape(n_seq, WINDOW, N_KV_HEADS, HEAD_DIM))
        cp.append(ctx)
        hrp.append(h_last[:, 0, :N_STATE].reshape(n_seq, SSM_GROUPS, SSM_STATE))
        hip.append(h_last[:, 0, N_STATE:].reshape(n_seq, SSM_GROUPS, SSM_STATE))

        zq_s, _, zc_s, us, kvn = _inproj(xs, g_mix, w_in_q, w_in_rest, l, n_dec, 1)
        kvn = kvn.reshape(n_dec, V_END - Q_END)
        zq4 = (zq_s * jnp.asarray(scale, BF16)).reshape(n_dec, KV_REP, 1, LANES)
        q3 = jnp.where(own_lane, zq4, jnp.zeros((), BF16)).reshape(n_dec * N_HEADS, LANES)
        kn3 = kvn[:, :LANES].reshape(n_dec, 1, LANES)
        vn3 = kvn[:, LANES:].reshape(n_dec, 1, LANES)
        o3, k_all, v_all, conv_s, nct, ssm_s, nhr, nhi = _dec_mix(
            q3, kn3, vn3, kvn, ck, cv, zc_s, cct, us, hr, hi, sinks_dec, conv_dw_w, cb3, lg3, lb3,
            bmat, cmat, a_step, d3, gw_b, gb3, k_all, v_all, l)
        xs = _tail(xs, g_ffn,
                   [(o3.reshape(n_dec, N_HEADS * LANES), decspec(N_HEADS * LANES), wo_dec),
                    (conv_s, decspec(CONV_CH), wo_conv),
                    (ssm_s, decspec(SSM_CH), wo_ssm)],
                   wg_b, wu_b, wd_b, l, n_dec, final)
        cs.append(jnp.transpose(nct, (1, 0, 2)))
        hrs.append(nhr.reshape(n_dec, SSM_GROUPS, SSM_STATE))
        his.append(nhi.reshape(n_dec, SSM_GROUPS, SSM_STATE))

    return (xp.reshape(n_seq, seq_len, D_MODEL), xs.reshape(n_dec, 1, D_MODEL),
            jnp.stack(kp), jnp.stack(vp), jnp.stack(cp), jnp.stack(hrp), jnp.stack(hip),
            cache_out(k_all), cache_out(v_all), jnp.stack(cs), jnp.stack(hrs), jnp.stack(his))
```
